```python
import math
import jax
import jax.numpy as jnp
from jax import lax
import numpy as np

D_MODEL = 1024
BATCH = 4
SEQ = 4096
DEPTH = 1
DEC_BATCH = 32
DEC_SEQ = 1
PAST_LEN = 8192
PAGE_SIZE = 128

MIX_W = D_MODEL
ATTN_W = MIX_W // 2
POOL_W = MIX_W - ATTN_W
HEAD_DIM = 64
N_HEADS = ATTN_W // (2 * HEAD_DIM)
V_DIM = 2 * HEAD_DIM
IN_COLS = 3 * ATTN_W + POOL_W
POOL_WINDOWS = (2, 4, 8, 16)
N_POOL_GROUPS = len(POOL_WINDOWS)
POOL_GW = POOL_W // N_POOL_GROUPS
POOL_BUF = max(POOL_WINDOWS) - 1
ROPE_THETA = 10000.0
Q_BLOCK = 128
N_EXPERTS = 64
TOP_K = 8
N_EXPERT_GROUPS = 8
TOPK_GROUPS = 4
EXPERT_DIM = 256
SHARED_DIM = 256
ROUTED_SCALE = 2.5
MOE_BLOCK = 128
EPS = 1e-6

kernel_name = "hybrid_diffattn_pool_moe_step"


def rmsnorm(x, g):
    xf = x.astype(jnp.float32)
    y = xf * lax.rsqrt(jnp.mean(xf * xf, axis=-1, keepdims=True) + EPS)
    return (y * g.astype(jnp.float32)).astype(x.dtype)


def ada_mod(c, w, b):
    mod = jax.nn.silu(c) @ w + b
    return jnp.split(mod, 6, axis=-1)


def modulate(h, shift, scale):
    return h * (1.0 + scale[:, None, :]) + shift[:, None, :]


def in_proj(h, w_in):
    proj = h @ w_in
    lead = proj.shape[:-1]
    q = proj[..., :ATTN_W].reshape(*lead, N_HEADS, 2, HEAD_DIM)
    k = proj[..., ATTN_W:2 * ATTN_W].reshape(*lead, N_HEADS, 2, HEAD_DIM)
    v = proj[..., 2 * ATTN_W:3 * ATTN_W].reshape(*lead, N_HEADS, V_DIM)
    z = proj[..., 3 * ATTN_W:]
    return q, k, v, z


def rope(x, pos):
    half = HEAD_DIM // 2
    inv_freq = ROPE_THETA ** (-jnp.arange(half, dtype=jnp.float32) / half)
    ang = pos.astype(jnp.float32)[:, None] * inv_freq[None, :]
    cos = jnp.cos(ang)[None, :, None, None, :]
    sin = jnp.sin(ang)[None, :, None, None, :]
    x1 = x[..., :half].astype(jnp.float32)
    x2 = x[..., half:].astype(jnp.float32)
    return jnp.concatenate([x1 * cos - x2 * sin, x2 * cos + x1 * sin], axis=-1).astype(x.dtype)


def diff_lambda(lq1, lk1, lq2, lk2, lam_init):
    return (jnp.exp(jnp.sum(lq1 * lk1).astype(jnp.float32))
            - jnp.exp(jnp.sum(lq2 * lk2).astype(jnp.float32)) + lam_init)


def diff_weights(logits, mask, lam):
    p = jax.nn.softmax(jnp.where(mask, logits, -jnp.inf), axis=-1)
    return p[0] - lam * p[1]


def diff_attn_prompt(q, k, v, lam):
    b, s = q.shape[:2]
    nb = s // Q_BLOCK
    q_blocks = jnp.moveaxis(q.reshape(b, nb, Q_BLOCK, N_HEADS, 2, HEAD_DIM), 1, 0)
    k_pos = jnp.arange(s)
    scale = HEAD_DIM ** -0.5

    def one_block(args):
        qb, i = args
        q_pos = i * Q_BLOCK + jnp.arange(Q_BLOCK)
        causal = k_pos[None, :] <= q_pos[:, None]
        logits = jnp.einsum('bqhmd,bkhmd->mbhqk', qb, k).astype(jnp.float32) * scale
        wts = diff_weights(logits, causal, lam)
        return jnp.einsum('bhqk,bkhe->bqhe', wts.astype(v.dtype), v)

    out = lax.map(one_block, (q_blocks, jnp.arange(nb)))
    return jnp.moveaxis(out, 0, 1).reshape(b, s, N_HEADS, V_DIM)


def diff_attn_sample(q, k, v, lam, k_past, v_past):
    n_new = q.shape[1]
    n_past = k_past.shape[1]
    scale = HEAD_DIM ** -0.5
    l_past = jnp.einsum('bqhmd,bkhmd->mbhqk', q, k_past)
    l_new = jnp.einsum('bqhmd,bkhmd->mbhqk', q, k)
    logits = jnp.concatenate([l_past, l_new], axis=-1).astype(jnp.float32) * scale
    t = jnp.arange(n_new)
    mask = jnp.concatenate([jnp.ones((n_new, n_past), dtype=bool), t[None, :] <= t[:, None]], axis=-1)
    wts = diff_weights(logits, mask, lam).astype(v.dtype)
    return (jnp.einsum('bhqk,bkhe->bqhe', wts[..., :n_past], v_past)
            + jnp.einsum('bhqk,bkhe->bqhe', wts[..., n_past:], v))


def pool_mix(z, pos, w_pool, pool_scale):
    b, L = z.shape[:2]
    zf = z.astype(jnp.float32).reshape(b, L, N_POOL_GROUPS, POOL_GW)
    cs = jnp.cumsum(zf, axis=1)
    outs = []
    for g, w in enumerate(POOL_WINDOWS):
        csg = cs[:, :, g]
        prev = jnp.pad(csg, ((0, 0), (w, 0), (0, 0)))[:, :L]
        cnt = jnp.minimum(pos + 1, w).astype(jnp.float32)[None, :, None]
        outs.append((csg - prev) / cnt - zf[:, :, g])
    d = jnp.stack(outs, axis=2)
    y = jnp.einsum('blgc,gcf->blgf', d, w_pool.astype(jnp.float32)).reshape(b, L, POOL_W)
    return (y * pool_scale.astype(jnp.float32)).astype(z.dtype)


def pool_prompt(z, pos, w_pool, pool_scale):
    return pool_mix(z, pos, w_pool, pool_scale), z[:, -POOL_BUF:]


def pool_sample(z, buf, past_len, w_pool, pool_scale):
    zc = jnp.concatenate([buf.astype(z.dtype), z], axis=1)
    pos = past_len - POOL_BUF + jnp.arange(zc.shape[1])
    return pool_mix(zc, pos, w_pool, pool_scale)[:, POOL_BUF:], zc[:, -POOL_BUF:]


def route(x, w_router, router_bias):
    t = x.shape[0]
    scores = jax.nn.sigmoid((x @ w_router).astype(jnp.float32))
    sel = (scores + router_bias.astype(jnp.float32)).reshape(t, N_EXPERT_GROUPS, -1)
    group_score = jnp.sum(lax.top_k(sel, 2)[0], axis=-1)
    _, gidx = lax.top_k(group_score, TOPK_GROUPS)
    gmask = jnp.zeros((t, N_EXPERT_GROUPS), dtype=bool).at[jnp.arange(t)[:, None], gidx].set(True)
    masked = jnp.where(gmask[:, :, None], sel, -jnp.inf).reshape(t, N_EXPERTS)
    _, idx = lax.top_k(masked, TOP_K)
    w = jnp.take_along_axis(scores, idx, axis=-1)
    w = w / jnp.sum(w, axis=-1, keepdims=True) * ROUTED_SCALE
    return idx, w


def routed_experts(x, idx, wts, w_gate, w_up, w_down):
    t, d = x.shape
    a = t * TOP_K
    flat_e = idx.reshape(a)
    flat_t = jnp.repeat(jnp.arange(t, dtype=jnp.int32), TOP_K)
    flat_w = wts.reshape(a)
    order = jnp.argsort(flat_e)
    e_s, t_s, w_s = flat_e[order], flat_t[order], flat_w[order]
    counts = jnp.bincount(flat_e, length=N_EXPERTS)
    pcounts = (counts + MOE_BLOCK - 1) // MOE_BLOCK * MOE_BLOCK
    start = jnp.cumsum(counts) - counts
    pend = jnp.cumsum(pcounts)
    pstart = pend - pcounts
    dest = pstart[e_s] + jnp.arange(a) - start[e_s]
    n_blocks = -(-a // MOE_BLOCK) + N_EXPERTS
    n_rows = n_blocks * MOE_BLOCK
    tok_p = jnp.full((n_rows,), t, dtype=jnp.int32).at[dest].set(t_s)
    w_p = jnp.zeros((n_rows,), dtype=wts.dtype).at[dest].set(w_s)
    blk_e = jnp.minimum(jnp.searchsorted(pend, jnp.arange(n_blocks) * MOE_BLOCK, side='right'), N_EXPERTS - 1)
    x_pad = jnp.concatenate([x, jnp.zeros((1, d), x.dtype)], axis=0)

    def one_block(args):
        tok_b, w_b, e = args
        xb = x_pad[tok_b]
        hb = jax.nn.silu(xb @ w_gate[e]) * (xb @ w_up[e])
        return (hb @ w_down[e]) * w_b[:, None]

    out = lax.map(one_block, (tok_p.reshape(n_blocks, MOE_BLOCK), w_p.reshape(n_blocks, MOE_BLOCK), blk_e))
    return jax.ops.segment_sum(out.reshape(n_rows, d), tok_p, num_segments=t + 1)[:t]


def moe_ffn(h, lw):
    lead = h.shape[:-1]
    x = h.reshape(-1, h.shape[-1])
    idx, wts = route(x, lw['router_w'], lw['router_bias'])
    routed = routed_experts(x, idx, wts.astype(x.dtype), lw['expert_w_gate'], lw['expert_w_up'], lw['expert_w_down'])
    shared = (jax.nn.silu(x @ lw['shared_w_gate']) * (x @ lw['shared_w_up'])) @ lw['shared_w_down']
    return (routed + shared).reshape(*lead, -1)


def trunk_layer(x, c, pos, lam_init, attend, pool, lw):
    sh1, sc1, g1, sh2, sc2, g2 = ada_mod(c, lw['ada_w'], lw['ada_b'])
    h = modulate(rmsnorm(x, lw['pre_mix_g']), sh1, sc1)
    q, k, v, z = in_proj(h, lw['w_in'])
    q = rope(q, pos)
    k = rope(k, pos)
    lam = diff_lambda(lw['lambda_q1'], lw['lambda_k1'], lw['lambda_q2'], lw['lambda_k2'], lam_init)
    o = attend(q, k, v, lam)
    o = rmsnorm(o, lw['subln_g']) * (1.0 - lam_init)
    pz, pool_state = pool(z)
    m = jnp.concatenate([o.reshape(*o.shape[:2], ATTN_W), pz], axis=-1) @ lw['w_out']
    x = x + g1[:, None, :] * rmsnorm(m, lw['post_mix_g'])
    h2 = modulate(rmsnorm(x, lw['pre_ffn_g']), sh2, sc2)
    f = moe_ffn(h2, lw)
    x = x + g2[:, None, :] * rmsnorm(f, lw['post_ffn_g'])
    k_rows = k.reshape(*k.shape[:3], 2 * HEAD_DIM)
    return x, k_rows, v, pool_state


def setup_inputs(seed: int = 0) -> dict:
    key = jax.random.key(seed)
    ks = jax.random.split(key, 32)
    f32 = jnp.float32

    def nrm(k, shape, s):
        return jax.random.normal(k, shape, f32) * s

    n_pages = PAST_LEN // PAGE_SIZE
    n_used = DEC_BATCH * n_pages
    n_phys = n_used + n_used // 4
    page_table = jax.random.permutation(ks[0], n_phys)[:n_used].reshape(DEC_BATCH, n_pages).astype(jnp.int32)
    kv_shape = (DEPTH, n_phys, PAGE_SIZE, N_HEADS, 2 * HEAD_DIM)
    return {
        'x_prompt': nrm(ks[1], (BATCH, SEQ, D_MODEL), 1.0),
        'x_sample': nrm(ks[2], (DEC_BATCH, DEC_SEQ, D_MODEL), 1.0),
        'c_prompt': nrm(ks[3], (BATCH, D_MODEL), 1.0),
        'c_sample': nrm(ks[4], (DEC_BATCH, D_MODEL), 1.0),
        'cache_k': nrm(ks[5], kv_shape, 1.0),
        'cache_v': nrm(ks[6], kv_shape, 1.0),
        'state_pool': nrm(ks[7], (DEPTH, DEC_BATCH, POOL_BUF, POOL_W), 1.0),
        'page_table': page_table,
        'ada_w': nrm(ks[8], (DEPTH, D_MODEL, 6 * D_MODEL), 0.2 * D_MODEL ** -0.5),
        'ada_b': nrm(ks[9], (DEPTH, 6 * D_MODEL), 0.02),
        'pre_mix_g': 1.0 + nrm(ks[10], (DEPTH, D_MODEL), 0.02),
        'post_mix_g': 1.0 + nrm(ks[11], (DEPTH, D_MODEL), 0.02),
        'pre_ffn_g': 1.0 + nrm(ks[12], (DEPTH, D_MODEL), 0.02),
        'post_ffn_g': 1.0 + nrm(ks[13], (DEPTH, D_MODEL), 0.02),
        'w_in': nrm(ks[14], (DEPTH, D_MODEL, IN_COLS), D_MODEL ** -0.5),
        'w_out': nrm(ks[15], (DEPTH, MIX_W, D_MODEL), MIX_W ** -0.5),
        'lambda_q1': nrm(ks[16], (DEPTH, HEAD_DIM), 0.1),
        'lambda_k1': nrm(ks[17], (DEPTH, HEAD_DIM), 0.1),
        'lambda_q2': nrm(ks[18], (DEPTH, HEAD_DIM), 0.1),
        'lambda_k2': nrm(ks[19], (DEPTH, HEAD_DIM), 0.1),
        'subln_g': 1.0 + nrm(ks[20], (DEPTH, V_DIM), 0.02),
        'pool_w': nrm(ks[21], (DEPTH, N_POOL_GROUPS, POOL_GW, POOL_GW), POOL_GW ** -0.5),
        'pool_scale': 1.0 + nrm(ks[22], (DEPTH, POOL_W), 0.1),
        'router_w': nrm(ks[23], (DEPTH, D_MODEL, N_EXPERTS), D_MODEL ** -0.5),
        'router_bias': nrm(ks[24], (DEPTH, N_EXPERTS), 0.01),
        'expert_w_gate': nrm(ks[25], (DEPTH, N_EXPERTS, D_MODEL, EXPERT_DIM), D_MODEL ** -0.5),
        'expert_w_up': nrm(ks[26], (DEPTH, N_EXPERTS, D_MODEL, EXPERT_DIM), D_MODEL ** -0.5),
        'expert_w_down': nrm(ks[27], (DEPTH, N_EXPERTS, EXPERT_DIM, D_MODEL), EXPERT_DIM ** -0.5),
        'shared_w_gate': nrm(ks[28], (DEPTH, D_MODEL, SHARED_DIM), D_MODEL ** -0.5),
        'shared_w_up': nrm(ks[29], (DEPTH, D_MODEL, SHARED_DIM), D_MODEL ** -0.5),
        'shared_w_down': nrm(ks[30], (DEPTH, SHARED_DIM, D_MODEL), SHARED_DIM ** -0.5),
    }


def reference(x_prompt, x_sample, c_prompt, c_sample, cache_k, cache_v, state_pool, page_table,
              ada_w, ada_b, pre_mix_g, post_mix_g, pre_ffn_g, post_ffn_g, w_in, w_out,
              lambda_q1, lambda_k1, lambda_q2, lambda_k2, subln_g, pool_w, pool_scale,
              router_w, router_bias, expert_w_gate, expert_w_up, expert_w_down,
              shared_w_gate, shared_w_up, shared_w_down):
    seq = x_prompt.shape[1]
    dec_batch, dec_seq = x_sample.shape[:2]
    past_len = page_table.shape[1] * cache_k.shape[2]
    pos_p = jnp.arange(seq)
    pos_s = past_len + jnp.arange(dec_seq)
    y_p, y_s = x_prompt, x_sample
    kp, vp, pp, ksm, vsm, psm = [], [], [], [], [], []
    for l in range(DEPTH):
        lw = {
            'ada_w': ada_w[l], 'ada_b': ada_b[l],
            'pre_mix_g': pre_mix_g[l], 'post_mix_g': post_mix_g[l],
            'pre_ffn_g': pre_ffn_g[l], 'post_ffn_g': post_ffn_g[l],
            'w_in': w_in[l], 'w_out': w_out[l],
            'lambda_q1': lambda_q1[l], 'lambda_k1': lambda_k1[l],
            'lambda_q2': lambda_q2[l], 'lambda_k2': lambda_k2[l],
            'subln_g': subln_g[l],
            'router_w': router_w[l], 'router_bias': router_bias[l],
            'expert_w_gate': expert_w_gate[l], 'expert_w_up': expert_w_up[l],
            'expert_w_down': expert_w_down[l],
            'shared_w_gate': shared_w_gate[l], 'shared_w_up': shared_w_up[l],
            'shared_w_down': shared_w_down[l],
        }
        lam_init = 0.8 - 0.6 * math.exp(-0.3 * l)
        y_p, k_rows, v_rows, buf = trunk_layer(
            y_p, c_prompt, pos_p, lam_init, diff_attn_prompt,
            lambda z: pool_prompt(z, pos_p, pool_w[l], pool_scale[l]), lw)
        kp.append(k_rows)
        vp.append(v_rows)
        pp.append(buf)
        k_past = cache_k[l, page_table].reshape(dec_batch, past_len, N_HEADS, 2, HEAD_DIM)
        v_past = cache_v[l, page_table].reshape(dec_batch, past_len, N_HEADS, V_DIM)
        y_s, k_rows, v_rows, buf = trunk_layer(
            y_s, c_sample, pos_s, lam_init,
            lambda q, k, v, lam: diff_attn_sample(q, k, v, lam, k_past, v_past),
            lambda z: pool_sample(z, state_pool[l], past_len, pool_w[l], pool_scale[l]), lw)
        ksm.append(k_rows)
        vsm.append(v_rows)
        psm.append(buf)
    return (y_p, y_s, jnp.stack(kp), jnp.stack(vp), jnp.stack(pp), jnp.stack(ksm), jnp.stack(vsm), jnp.stack(psm))
```

```python
import functools
import math

import jax
import jax.numpy as jnp
from jax import lax
from jax.experimental import pallas as pl
from jax.experimental.pallas import tpu as pltpu

F32 = jnp.float32
BF16 = jnp.bfloat16

D_MODEL = 1024
ATTN_W = 512
POOL_W = 512
HEAD_DIM = 64
N_HEADS = 4
V_DIM = 2 * HEAD_DIM
IN_COLS = 3 * ATTN_W + POOL_W
POOL_WINDOWS = (2, 4, 8, 16)
POOL_GW = 128
POOL_BUF = 15
POOL_HALO = 16
ROPE_THETA = 10000.0
N_EXPERTS = 64
TOP_K = 8
N_EXPERT_GROUPS = 8
GROUP_SIZE = N_EXPERTS // N_EXPERT_GROUPS
TOPK_GROUPS = 4
EXPERT_DIM = 256
ROUTED_SCALE = 2.5
EPS = 1e-6
LANES = 128
NEG_INF = float("-inf")
VMEM_LIMIT = 56 * 1024 * 1024

TM_TOKEN = 512
TQ_ATTN = 256
TM_MOE = 2048
TM_FINAL = 1024
DEC_CHUNK_PAGES = 16


def _silu(x):
    return x * jax.nn.sigmoid(x)


def _rmsnorm(x, g):
    ms = jnp.mean(x * x, axis=-1, keepdims=True)
    return x * lax.rsqrt(ms + EPS) * g


def _dot(a, b):
    return jnp.dot(a, b, preferred_element_type=F32)


def _dot_nt(a, b):
    return lax.dot_general(a, b, (((1,), (1,)), ((), ())), preferred_element_type=F32)


def _ada_kernel(c_ref, w_ref, b_ref, o_ref):
    s = _silu(c_ref[...])
    o_ref[...] = _dot(s.astype(BF16), w_ref[...].astype(BF16)) + b_ref[...]


def _ada(c_pad, w, b):
    rows, n = c_pad.shape[0], w.shape[1]
    tn = 1536
    return pl.pallas_call(
        _ada_kernel,
        grid=(n // tn,),
        in_specs=[pl.BlockSpec((rows, D_MODEL), lambda j: (0, 0)),
                  pl.BlockSpec((D_MODEL, tn), lambda j: (0, j)),
                  pl.BlockSpec((1, tn), lambda j: (0, j))],
        out_specs=pl.BlockSpec((rows, tn), lambda j: (0, j)),
        out_shape=jax.ShapeDtypeStruct((rows, n), F32),
        compiler_params=pltpu.CompilerParams(dimension_semantics=("arbitrary",), vmem_limit_bytes=VMEM_LIMIT),
        name="ada_mod",
    )(c_pad, w, b)


def _rope(x, cos, sin_signed):
    outs = []
    for c in range(x.shape[1] // LANES):
        blk = x[:, c * LANES:(c + 1) * LANES]
        lane = lax.broadcasted_iota(jnp.int32, blk.shape, 1)
        partner = jnp.where((lane % HEAD_DIM) < HEAD_DIM // 2,
                            pltpu.roll(blk, LANES - HEAD_DIM // 2, 1),
                            pltpu.roll(blk, HEAD_DIM // 2, 1))
        outs.append(blk * cos + partner * sin_signed)
    return jnp.concatenate(outs, axis=1)


def _premix_kernel(x_ref, sh_ref, sc_ref, g_ref, w_ref, cos_ref, sin_ref,
                   k_ref, v_ref, z_ref, qb_ref, kb_ref, *rest, n_kv):
    h = _rmsnorm(x_ref[...], g_ref[...]) * (1.0 + sc_ref[0]) + sh_ref[0]
    proj = _dot(h.astype(BF16), w_ref[...])
    cos, sin = cos_ref[...], sin_ref[...]
    q = _rope(proj[:, :ATTN_W], cos, sin)
    k = _rope(proj[:, ATTN_W:2 * ATTN_W], cos, sin)
    v = proj[:, 2 * ATTN_W:3 * ATTN_W]
    k_ref[...] = k
    v_ref[...] = v
    z_ref[...] = proj[:, 3 * ATTN_W:]
    qb_ref[...] = (q * (HEAD_DIM ** -0.5)).astype(BF16)
    kb_ref[...] = k.astype(BF16)
    if n_kv:
        vt_ref = rest[0]
        vt = v.T.astype(BF16)
        tk = vt.shape[1] // n_kv
        for hh in range(N_HEADS):
            for c in range(n_kv):
                vt_ref[0, hh, c] = vt[hh * V_DIM:(hh + 1) * V_DIM, c * tk:(c + 1) * tk]


def _premix(x, shift, scale, gain, w_in_b, cos, sin, *, tm, rows_per_mod, seq_tiles, tk):
    t = x.shape[0]
    n_tiles = t // tm
    mod_rows = shift.shape[1]
    mod_idx = (lambda i: (i // rows_per_mod, 0, 0))
    n_kv = tm // tk if tk else 0
    out_shape = [jax.ShapeDtypeStruct((t, ATTN_W), F32), jax.ShapeDtypeStruct((t, ATTN_W), F32),
                 jax.ShapeDtypeStruct((t, POOL_W), F32), jax.ShapeDtypeStruct((t, ATTN_W), BF16),
                 jax.ShapeDtypeStruct((t, ATTN_W), BF16)]
    row_spec = pl.BlockSpec((tm, ATTN_W), lambda i: (i, 0))
    out_specs = [row_spec] * 5
    if n_kv:
        nb = n_tiles // seq_tiles
        out_shape.append(jax.ShapeDtypeStruct((nb, N_HEADS, seq_tiles * n_kv, V_DIM, tk), BF16))
        out_specs.append(pl.BlockSpec((1, N_HEADS, n_kv, V_DIM, tk),
                                      lambda i: (i // seq_tiles, 0, i % seq_tiles, 0, 0)))
    return pl.pallas_call(
        functools.partial(_premix_kernel, n_kv=n_kv),
        grid=(n_tiles,),
        in_specs=[pl.BlockSpec((tm, D_MODEL), lambda i: (i, 0)),
                  pl.BlockSpec((1, mod_rows, D_MODEL), mod_idx),
                  pl.BlockSpec((1, mod_rows, D_MODEL), mod_idx),
                  pl.BlockSpec((1, D_MODEL), lambda i: (0, 0)),
                  pl.BlockSpec((D_MODEL, IN_COLS), lambda i: (0, 0)),
                  pl.BlockSpec((tm, LANES), lambda i: (i % seq_tiles, 0)),
                  pl.BlockSpec((tm, LANES), lambda i: (i % seq_tiles, 0))],
        out_specs=out_specs,
        out_shape=out_shape,
        compiler_params=pltpu.CompilerParams(dimension_semantics=("parallel",), vmem_limit_bytes=VMEM_LIMIT),
        name="premix",
    )(x, shift, scale, gain, w_in_b, cos, sin)


def _diff_lambda(lam_ref, lam_init):
    lam = lam_ref[...]
    s1 = jnp.sum(lam[0:1] * lam[1:2], axis=1, keepdims=True)
    s2 = jnp.sum(lam[2:3] * lam[3:4], axis=1, keepdims=True)
    return jnp.exp(s1) - jnp.exp(s2) + lam_init


def _split_maps(q):
    lane = lax.broadcasted_iota(jnp.int32, q.shape, 1)
    zero = jnp.zeros_like(q)
    return jnp.concatenate([jnp.where(lane < HEAD_DIM, q, zero), jnp.where(lane >= HEAD_DIM, q, zero)], axis=0)


def _attn_kernel(q_ref, k_ref, vt_ref, lam_ref, g_ref, o_ref, *, tq, lam_init):
    i = pl.program_id(2)
    qpad = _split_maps(q_ref[0])

    def step(j, carry, masked):
        m, l, acc = carry
        kj = k_ref[0, pl.ds(pl.multiple_of(j * tq, tq), tq), :]
        s = _dot_nt(kj, qpad)
        if masked:
            kpos = lax.broadcasted_iota(jnp.int32, s.shape, 0)
            qpos = lax.broadcasted_iota(jnp.int32, s.shape, 1) % tq
            s = jnp.where(kpos <= qpos, s, NEG_INF)
        m_new = jnp.maximum(m, jnp.max(s, axis=0, keepdims=True))
        alpha = jnp.exp(m - m_new)
        p = jnp.exp(s - m_new)
        l = alpha * l + jnp.sum(p, axis=0, keepdims=True)
        acc = alpha * acc + _dot(vt_ref[0, 0, j], p.astype(BF16))
        return m_new, l, acc

    init = (jnp.full((1, 2 * tq), -1e30, F32), jnp.zeros((1, 2 * tq), F32), jnp.zeros((V_DIM, 2 * tq), F32))
    carry = lax.fori_loop(0, i, lambda j, c: step(j, c, False), init)
    _, l, acc = step(i, carry, True)
    o = acc / l
    od = o[:, :tq] - _diff_lambda(lam_ref, lam_init) * o[:, tq:]
    ms = jnp.mean(od * od, axis=0, keepdims=True)
    on = od * lax.rsqrt(ms + EPS) * g_ref[...] * (1.0 - lam_init)
    o_ref[0] = on.T.astype(BF16)


def _attn_prompt(qb, kb, vt, lam4, subln_col, *, tq, lam_init):
    b, s, _ = qb.shape
    nq = s // tq
    return pl.pallas_call(
        functools.partial(_attn_kernel, tq=tq, lam_init=lam_init),
        grid=(b, N_HEADS, nq),
        in_specs=[pl.BlockSpec((1, tq, V_DIM), lambda bb, h, i: (bb, i, h)),
                  pl.BlockSpec((1, s, V_DIM), lambda bb, h, i: (bb, 0, h)),
                  pl.BlockSpec((1, 1, nq, V_DIM, tq), lambda bb, h, i: (bb, h, 0, 0, 0)),
                  pl.BlockSpec((4, HEAD_DIM), lambda bb, h, i: (0, 0)),
                  pl.BlockSpec((V_DIM, 1), lambda bb, h, i: (0, 0))],
        out_specs=pl.BlockSpec((1, tq, V_DIM), lambda bb, h, i: (bb, i, h)),
        out_shape=jax.ShapeDtypeStruct((b, s, ATTN_W), BF16),
        compiler_params=pltpu.CompilerParams(dimension_semantics=("parallel", "parallel", "arbitrary"),
                                             vmem_limit_bytes=VMEM_LIMIT),
        name="attn_prompt",
    )(qb, kb, vt, lam4, subln_col)


def _decode_kernel(pt_ref, q_ref, kn_ref, vn_ref, lam_ref, g_ref, k_hbm, v_hbm, o_ref,
                   kbuf, vbuf, sem, m_sc, l_sc, acc_sc, *, n_chunks, chunk_pages, page_rows, lam_init):
    b = pl.program_id(0)
    c = pl.program_id(1)
    nb = pl.num_programs(0)
    step = b * n_chunks + c
    slot = step % 2

    def chunk_copies(bb, cc, sl):
        copies = []
        for p in range(chunk_pages):
            phys = pt_ref[bb * (n_chunks * chunk_pages) + cc * chunk_pages + p]
            src = pl.ds(pl.multiple_of(phys * page_rows, page_rows), page_rows)
            dst = pl.ds(p * page_rows, page_rows)
            copies.append(pltpu.make_async_copy(k_hbm.at[src, :], kbuf.at[sl, dst, :], sem.at[0, sl]))
            copies.append(pltpu.make_async_copy(v_hbm.at[src, :], vbuf.at[sl, dst, :], sem.at[1, sl]))
        return copies

    @pl.when(step == 0)
    def _():
        for cp in chunk_copies(b, c, slot):
            cp.start()

    @pl.when(step + 1 < nb * n_chunks)
    def _():
        nxt = step + 1
        for cp in chunk_copies(nxt // n_chunks, nxt % n_chunks, 1 - slot):
            cp.start()

    @pl.when(c == 0)
    def _():
        m_sc[...] = jnp.full(m_sc.shape, -1e30, F32)
        l_sc[...] = jnp.zeros(l_sc.shape, F32)
        acc_sc[...] = jnp.zeros(acc_sc.shape, F32)

    qblk = _split_maps(q_ref[0])
    for cp in chunk_copies(b, c, slot):
        cp.wait()
    kc = kbuf[slot].astype(BF16)
    vc = vbuf[slot].astype(BF16)
    s = _dot_nt(qblk, kc)
    row_head = lax.broadcasted_iota(jnp.int32, s.shape, 0) % N_HEADS
    col_head = lax.broadcasted_iota(jnp.int32, s.shape, 1) % N_HEADS
    s = jnp.where(row_head == col_head, s, NEG_INF)
    m_old = m_sc[...]
    m_new = jnp.maximum(m_old, jnp.max(s, axis=1, keepdims=True))
    alpha = jnp.exp(m_old - m_new)
    p = jnp.exp(s - m_new)
    l_new = alpha * l_sc[...] + jnp.sum(p, axis=1, keepdims=True)
    acc_new = alpha * acc_sc[...] + _dot(p.astype(BF16), vc)
    m_sc[...] = m_new
    l_sc[...] = l_new
    acc_sc[...] = acc_new

    @pl.when(c == n_chunks - 1)
    def _():
        kn = jnp.concatenate([kn_ref[0], kn_ref[0]], axis=0).astype(F32)
        vn = jnp.concatenate([vn_ref[0], vn_ref[0]], axis=0).astype(F32)
        s_n = jnp.sum(qblk.astype(F32) * kn, axis=1, keepdims=True)
        m_f = jnp.maximum(m_new, s_n)
        a_f = jnp.exp(m_new - m_f)
        p_n = jnp.exp(s_n - m_f)
        l_f = a_f * l_new + p_n
        o = (a_f * acc_new + p_n * vn) / l_f
        od = o[:N_HEADS] - _diff_lambda(lam_ref, lam_init) * o[N_HEADS:]
        o_ref[0] = (_rmsnorm(od, g_ref[...]) * (1.0 - lam_init)).astype(BF16)


def _attn_decode(page_table, q3, kn3, vn3, lam4, subln_row, k2, v2, *, page_rows, lam_init):
    nb, n_pages = page_table.shape
    chunk_pages = DEC_CHUNK_PAGES
    n_chunks = n_pages // chunk_pages
    rows = chunk_pages * page_rows
    head_spec = pl.BlockSpec((1, N_HEADS, V_DIM), lambda b, c, pt: (b, 0, 0))
    grid_spec = pltpu.PrefetchScalarGridSpec(
        num_scalar_prefetch=1,
        grid=(nb, n_chunks),
        in_specs=[head_spec, head_spec, head_spec,
                  pl.BlockSpec((4, HEAD_DIM), lambda b, c, pt: (0, 0)),
                  pl.BlockSpec((1, V_DIM), lambda b, c, pt: (0, 0)),
                  pl.BlockSpec(memory_space=pl.ANY),
                  pl.BlockSpec(memory_space=pl.ANY)],
        out_specs=head_spec,
        scratch_shapes=[pltpu.VMEM((2, rows, V_DIM), F32), pltpu.VMEM((2, rows, V_DIM), F32),
                        pltpu.SemaphoreType.DMA((2, 2)),
                        pltpu.VMEM((2 * N_HEADS, 1), F32), pltpu.VMEM((2 * N_HEADS, 1), F32),
                        pltpu.VMEM((2 * N_HEADS, V_DIM), F32)])
    return pl.pallas_call(
        functools.partial(_decode_kernel, n_chunks=n_chunks, chunk_pages=chunk_pages,
                          page_rows=page_rows, lam_init=lam_init),
        grid_spec=grid_spec,
        out_shape=jax.ShapeDtypeStruct((nb, N_HEADS, V_DIM), BF16),
        compiler_params=pltpu.CompilerParams(dimension_semantics=("arbitrary", "arbitrary"),
                                             vmem_limit_bytes=VMEM_LIMIT),
        name="attn_decode",
    )(page_table.reshape(-1), q3, kn3, vn3, lam4, subln_row, k2, v2)


def _pool_project(d_groups, pw_ref, ps_ref):
    ys = [_dot(d.astype(BF16), pw_ref[g].astype(BF16)) for g, d in enumerate(d_groups)]
    return jnp.concatenate(ys, axis=1) * ps_ref[...]


def _argmax_first(x, iota, n):
    mx = jnp.max(x, axis=0, keepdims=True)
    ix = jnp.min(jnp.where(x == mx, iota, float(n)), axis=0, keepdims=True)
    return mx, ix


def _row_iota(shape):
    return lax.broadcasted_iota(jnp.int32, shape, 0).astype(F32)


def _route(h2, rwt_ref, rb_ref):
    rw = rwt_ref[...]
    rw_hi = rw.astype(BF16)
    rw_lo = (rw - rw_hi.astype(F32)).astype(BF16)
    h_hi = h2.astype(BF16)
    h_lo = (h2 - h_hi.astype(F32)).astype(BF16)
    logits = _dot_nt(rw_hi, h_hi) + (_dot_nt(rw_hi, h_lo) + _dot_nt(rw_lo, h_hi))
    scores = jax.nn.sigmoid(logits)
    sel = scores + rb_ref[...]
    tm = sel.shape[1]
    iota_g = _row_iota((GROUP_SIZE, tm))
    group_rows = []
    for g in range(N_EXPERT_GROUPS):
        xg = sel[g * GROUP_SIZE:(g + 1) * GROUP_SIZE]
        m1, i1 = _argmax_first(xg, iota_g, GROUP_SIZE)
        m2 = jnp.max(jnp.where(iota_g == i1, NEG_INF, xg), axis=0, keepdims=True)
        group_rows.append(m1 + m2)
    gscore = jnp.concatenate(group_rows, axis=0)
    iota_gr = _row_iota(gscore.shape)
    gmask = jnp.zeros(gscore.shape, F32)
    for _ in range(TOPK_GROUPS):
        _, ig = _argmax_first(gscore, iota_gr, N_EXPERT_GROUPS)
        hit = iota_gr == ig
        gmask = jnp.where(hit, 1.0, gmask)
        gscore = jnp.where(hit, NEG_INF, gscore)
    masked = jnp.concatenate(
        [jnp.where(gmask[g:g + 1] > 0.5, sel[g * GROUP_SIZE:(g + 1) * GROUP_SIZE], NEG_INF)
         for g in range(N_EXPERT_GROUPS)], axis=0)
    iota_e = _row_iota(masked.shape)
    w = jnp.zeros(masked.shape, F32)
    for _ in range(TOP_K):
        _, ie = _argmax_first(masked, iota_e, N_EXPERTS)
        hit = iota_e == ie
        w = jnp.where(hit, scores, w)
        masked = jnp.where(hit, NEG_INF, masked)
    w = w / jnp.sum(w, axis=0, keepdims=True) * ROUTED_SCALE
    return w.T


def _mix_tail(o_b, pz, x, g1, sh2, sc2, wout_ref, gpost_ref, gpre_ref, rwt_ref, rb_ref,
              x1_ref, h2_ref, wt_ref):
    m = _dot(o_b, wout_ref[:ATTN_W, :]) + _dot(pz.astype(BF16), wout_ref[ATTN_W:, :])
    x1 = x + g1 * _rmsnorm(m, gpost_ref[...])
    h2 = _rmsnorm(x1, gpre_ref[...]) * (1.0 + sc2) + sh2
    x1_ref[...] = x1
    h2_ref[...] = h2.astype(BF16)
    wt_ref[...] = _route(h2, rwt_ref, rb_ref)


def _postmix_prompt_kernel(o_ref, z_ref, zprev_ref, x_ref, g1_ref, sh2_ref, sc2_ref,
                           wout_ref, pw_ref, ps_ref, gpost_ref, gpre_ref, rwt_ref, rb_ref,
                           x1_ref, h2_ref, wt_ref, zbuf, *, seq_tiles):
    tm = z_ref.shape[0]
    it = pl.program_id(0) % seq_tiles
    zt = z_ref[...]
    zbuf[0:POOL_HALO, :] = jnp.where(it == 0, 0.0, zprev_ref[...])
    zbuf[POOL_HALO:, :] = zt
    pos = it * tm + lax.broadcasted_iota(jnp.int32, (tm, 1), 0)
    d_groups = []
    for g, w in enumerate(POOL_WINDOWS):
        cols = slice(g * POOL_GW, (g + 1) * POOL_GW)
        win = zt[:, cols]
        for s in range(1, w):
            win = win + zbuf[POOL_HALO - s:POOL_HALO - s + tm, cols]
        cnt = jnp.minimum(pos + 1, w).astype(F32)
        d_groups.append(win / cnt - zt[:, cols])
    pz = _pool_project(d_groups, pw_ref, ps_ref)
    _mix_tail(o_ref[...], pz, x_ref[...], g1_ref[0], sh2_ref[0], sc2_ref[0], wout_ref, gpost_ref, gpre_ref,
              rwt_ref, rb_ref, x1_ref, h2_ref, wt_ref)


def _postmix_sample_kernel(o_ref, z_ref, st_ref, x_ref, g1_ref, sh2_ref, sc2_ref,
                           wout_ref, pw_ref, ps_ref, gpost_ref, gpre_ref, rwt_ref, rb_ref,
                           x1_ref, h2_ref, wt_ref, *, past_len):
    zt = z_ref[...]
    d_groups = []
    for g, w in enumerate(POOL_WINDOWS):
        cols = slice(g * POOL_GW, (g + 1) * POOL_GW)
        win = zt[:, cols]
        for s in range(1, w):
            win = win + st_ref[POOL_BUF - s][:, cols]
        d_groups.append(win / float(min(past_len + 1, w)) - zt[:, cols])
    pz = _pool_project(d_groups, pw_ref, ps_ref)
    _mix_tail(o_ref[...], pz, x_ref[...], g1_ref[0], sh2_ref[0], sc2_ref[0], wout_ref, gpost_ref, gpre_ref,
              rwt_ref, rb_ref, x1_ref, h2_ref, wt_ref)


def _postmix(o, z, hist, x, g1, sh2, sc2, wout_b, pool_w, pool_scale, g_post, g_pre, rwt, rb_col,
             *, tm, rows_per_mod, seq_tiles, past_len):
    t = x.shape[0]
    n_tiles = t // tm
    mod_rows = g1.shape[1]
    mod_spec = pl.BlockSpec((1, mod_rows, D_MODEL), lambda i: (i // rows_per_mod, 0, 0))
    full = lambda shape: pl.BlockSpec(shape, lambda i: (0,) * len(shape))
    if past_len is None:
        body = functools.partial(_postmix_prompt_kernel, seq_tiles=seq_tiles)
        halo_blocks = tm // POOL_HALO
        hist_spec = pl.BlockSpec((POOL_HALO, POOL_W), lambda i: (jnp.maximum(i * halo_blocks - 1, 0), 0))
        scratch = [pltpu.VMEM((POOL_HALO + tm, POOL_W), F32)]
    else:
        body = functools.partial(_postmix_sample_kernel, past_len=past_len)
        hist_spec = full(hist.shape)
        scratch = []
    return pl.pallas_call(
        body,
        grid=(n_tiles,),
        in_specs=[pl.BlockSpec((tm, ATTN_W), lambda i: (i, 0)),
                  pl.BlockSpec((tm, POOL_W), lambda i: (i, 0)),
                  hist_spec,
                  pl.BlockSpec((tm, D_MODEL), lambda i: (i, 0)),
                  mod_spec, mod_spec, mod_spec,
                  full((D_MODEL, D_MODEL)), full(pool_w.shape), full((1, POOL_W)),
                  full((1, D_MODEL)), full((1, D_MODEL)), full((N_EXPERTS, D_MODEL)), full((N_EXPERTS, 1))],
        out_specs=[pl.BlockSpec((tm, D_MODEL), lambda i: (i, 0)),
                   pl.BlockSpec((tm, D_MODEL), lambda i: (i, 0)),
                   pl.BlockSpec((tm, N_EXPERTS), lambda i: (i, 0))],
        out_shape=[jax.ShapeDtypeStruct((t, D_MODEL), F32), jax.ShapeDtypeStruct((t, D_MODEL), BF16),
                   jax.ShapeDtypeStruct((t, N_EXPERTS), F32)],
        scratch_shapes=scratch,
        compiler_params=pltpu.CompilerParams(dimension_semantics=("parallel",), vmem_limit_bytes=VMEM_LIMIT),
        name="postmix",
    )(o, z, hist, x, g1, sh2, sc2, wout_b, pool_w, pool_scale, g_post, g_pre, rwt, rb_col)


def _swiglu(xb, wg, wu):
    return _silu(_dot(xb, wg.astype(BF16))) * _dot(xb, wu.astype(BF16))


def _moe_kernel(h_ref, wt_ref, wg_ref, wu_ref, wd_ref, sg_ref, su_ref, sd_ref, f_ref):
    e = pl.program_id(1)
    xb = h_ref[...]

    @pl.when(e == 0)
    def _():
        f_ref[...] = _dot(_swiglu(xb, sg_ref[...], su_ref[...]).astype(BF16), sd_ref[...].astype(BF16))

    wt = wt_ref[...]
    lane = lax.broadcasted_iota(jnp.int32, wt.shape, 1)
    wcol = jnp.sum(jnp.where(lane == e, wt, 0.0), axis=1, keepdims=True)
    hh = _swiglu(xb, wg_ref[0], wu_ref[0]) * wcol
    f_ref[...] += _dot(hh.astype(BF16), wd_ref[0].astype(BF16))


def _moe(h2b, wt, wg, wu, wd, sg, su, sd, *, tm):
    t = h2b.shape[0]
    full = lambda shape: pl.BlockSpec(shape, lambda i, e: (0,) * len(shape))
    return pl.pallas_call(
        _moe_kernel,
        grid=(t // tm, N_EXPERTS),
        in_specs=[pl.BlockSpec((tm, D_MODEL), lambda i, e: (i, 0)),
                  pl.BlockSpec((tm, N_EXPERTS), lambda i, e: (i, 0)),
                  pl.BlockSpec((1, D_MODEL, EXPERT_DIM), lambda i, e: (e, 0, 0)),
                  pl.BlockSpec((1, D_MODEL, EXPERT_DIM), lambda i, e: (e, 0, 0)),
                  pl.BlockSpec((1, EXPERT_DIM, D_MODEL), lambda i, e: (e, 0, 0)),
                  full(sg.shape), full(su.shape), full(sd.shape)],
        out_specs=pl.BlockSpec((tm, D_MODEL), lambda i, e: (i, 0)),
        out_shape=jax.ShapeDtypeStruct((t, D_MODEL), F32),
        compiler_params=pltpu.CompilerParams(dimension_semantics=("parallel", "arbitrary"),
                                             vmem_limit_bytes=VMEM_LIMIT),
        name="moe_experts",
    )(h2b, wt, wg, wu, wd, sg, su, sd)


def _final_kernel(x1_ref, f_ref, g2_ref, gain_ref, y_ref):
    y_ref[...] = x1_ref[...] + g2_ref[0] * _rmsnorm(f_ref[...], gain_ref[...])


def _final(x1, f, g2, gain, *, tm, rows_per_mod):
    t = x1.shape[0]
    mod_rows = g2.shape[1]
    row = pl.BlockSpec((tm, D_MODEL), lambda i: (i, 0))
    return pl.pallas_call(
        _final_kernel,
        grid=(t // tm,),
        in_specs=[row, row,
                  pl.BlockSpec((1, mod_rows, D_MODEL), lambda i: (i // rows_per_mod, 0, 0)),
                  pl.BlockSpec((1, D_MODEL), lambda i: (0, 0))],
        out_specs=row,
        out_shape=jax.ShapeDtypeStruct((t, D_MODEL), F32),
        compiler_params=pltpu.CompilerParams(dimension_semantics=("parallel",), vmem_limit_bytes=VMEM_LIMIT),
        name="final_residual",
    )(x1, f, g2, gain)


def _rope_tables(pos):
    half = HEAD_DIM // 2
    inv_freq = ROPE_THETA ** (-jnp.arange(half, dtype=F32) / half)
    ang = pos.astype(F32)[:, None] * inv_freq[None, :]
    cos, sin = jnp.cos(ang), jnp.sin(ang)
    reps = LANES // HEAD_DIM
    return jnp.tile(jnp.concatenate([cos, cos], axis=1), (1, reps)), jnp.tile(jnp.concatenate([-sin, sin], axis=1), (1, reps))


def kernel(x_prompt, x_sample, c_prompt, c_sample, cache_k, cache_v, state_pool, page_table, ada_w, ada_b, pre_mix_g, post_mix_g, pre_ffn_g, post_ffn_g, w_in, w_out, lambda_q1, lambda_k1, lambda_q2, lambda_k2, subln_g, pool_w, pool_scale, router_w, router_bias, expert_w_gate, expert_w_up, expert_w_down, shared_w_gate, shared_w_up, shared_w_down):
    depth = ada_w.shape[0]
    assert depth == 1, "single-layer step"
    nb, seq, _ = x_prompt.shape
    db, dec_seq, _ = x_sample.shape
    assert dec_seq == 1
    page_size = cache_k.shape[2]
    n_pages = page_table.shape[1]
    past_len = n_pages * page_size
    lam_init = 0.8 - 0.6 * math.exp(-0.3 * 0)
    l = 0

    c_all = jnp.concatenate([c_prompt, c_sample], axis=0)
    pad = (-c_all.shape[0]) % 16
    mod = _ada(jnp.pad(c_all, ((0, pad), (0, 0))), ada_w[l], ada_b[l][None, :])
    mod_p = [m.reshape(nb, 1, D_MODEL) for m in jnp.split(mod[:nb], 6, axis=-1)]
    mod_s = [m.reshape(1, db, D_MODEL) for m in jnp.split(mod[nb:nb + db], 6, axis=-1)]

    w_in_b = w_in[l].astype(BF16)
    w_out_b = w_out[l].astype(BF16)
    lam4 = jnp.stack([lambda_q1[l], lambda_k1[l], lambda_q2[l], lambda_k2[l]])
    row = lambda v: v.reshape(1, -1)
    rwt = router_w[l].T
    rb_col = router_bias[l].reshape(N_EXPERTS, 1)
    experts = (expert_w_gate[l], expert_w_up[l], expert_w_down[l], shared_w_gate[l], shared_w_up[l], shared_w_down[l])

    t_p = nb * seq
    tm = TM_TOKEN
    seq_tiles = seq // tm
    cos_p, sin_p = _rope_tables(jnp.arange(seq))
    xp = x_prompt.reshape(t_p, D_MODEL)
    k_p, v_p, z_p, qb, kb, vt = _premix(xp, mod_p[0], mod_p[1], row(pre_mix_g[l]), w_in_b, cos_p, sin_p,
                                        tm=tm, rows_per_mod=seq_tiles, seq_tiles=seq_tiles, tk=TQ_ATTN)
    o_p = _attn_prompt(qb.reshape(nb, seq, ATTN_W), kb.reshape(nb, seq, ATTN_W), vt, lam4,
                       subln_g[l].reshape(V_DIM, 1), tq=TQ_ATTN, lam_init=lam_init)
    x1_p, h2_p, wt_p = _postmix(o_p.reshape(t_p, ATTN_W), z_p, z_p, xp, mod_p[2], mod_p[3], mod_p[4], w_out_b,
                                pool_w[l], row(pool_scale[l]), row(post_mix_g[l]), row(pre_ffn_g[l]), rwt, rb_col,
                                tm=tm, rows_per_mod=seq_tiles, seq_tiles=seq_tiles, past_len=None)
    f_p = _moe(h2_p, wt_p, *experts, tm=TM_MOE)
    y_p = _final(x1_p, f_p, mod_p[5], row(post_ffn_g[l]), tm=TM_FINAL, rows_per_mod=seq // TM_FINAL)

    cos_s, sin_s = _rope_tables(jnp.full((db,), past_len))
    xs = x_sample.reshape(db, D_MODEL)
    k_s, v_s, z_s, qb_s, kb_s = _premix(xs, mod_s[0], mod_s[1], row(pre_mix_g[l]), w_in_b, cos_s, sin_s,
                                        tm=db, rows_per_mod=1, seq_tiles=1, tk=0)
    page_rows = page_size * N_HEADS
    k2 = cache_k[l].reshape(-1, V_DIM)
    v2 = cache_v[l].reshape(-1, V_DIM)
    o_s = _attn_decode(page_table, qb_s.reshape(db, N_HEADS, V_DIM), kb_s.reshape(db, N_HEADS, V_DIM),
                       v_s.reshape(db, N_HEADS, V_DIM), lam4, row(subln_g[l]), k2, v2,
                       page_rows=page_rows, lam_init=lam_init)
    hist_s = jnp.transpose(state_pool[l], (1, 0, 2))
    x1_s, h2_s, wt_s = _postmix(o_s.reshape(db, ATTN_W), z_s, hist_s, xs, mod_s[2], mod_s[3], mod_s[4], w_out_b,
                                pool_w[l], row(pool_scale[l]), row(post_mix_g[l]), row(pre_ffn_g[l]), rwt, rb_col,
                                tm=db, rows_per_mod=1, seq_tiles=1, past_len=past_len)
    f_s = _moe(h2_s, wt_s, *experts, tm=db)
    y_s = _final(x1_s, f_s, mod_s[5], row(post_ffn_g[l]), tm=db, rows_per_mod=1)

    pool_p = z_p.reshape(nb, seq, POOL_W)[:, seq - POOL_BUF:]
    pool_s = jnp.concatenate([state_pool[l][:, 1:], z_s[:, None, :]], axis=1)
    return (y_p.reshape(nb, seq, D_MODEL), y_s.reshape(db, 1, D_MODEL),
            k_p.reshape(1, nb, seq, N_HEADS, V_DIM), v_p.reshape(1, nb, seq, N_HEADS, V_DIM), pool_p[None],
            k_s.reshape(1, db, 1, N_HEADS, V_DIM), v_s.reshape(1, db, 1, N_HEADS, V_DIM), pool_s[None])
```

```python
import functools
import math

import jax
import jax.numpy as jnp
from jax import lax
from jax.experimental import pallas as pl
from jax.experimental.pallas import tpu as pltpu

F32 = jnp.float32
BF16 = jnp.bfloat16

D_MODEL = 1024
ATTN_W = 512
POOL_W = 512
HEAD_DIM = 64
N_HEADS = 4
V_DIM = 2 * HEAD_DIM
IN_COLS = 3 * ATTN_W + POOL_W
POOL_WINDOWS = (2, 4, 8, 16)
POOL_GW = 128
POOL_BUF = 15
POOL_HALO = 16
ROPE_THETA = 10000.0
N_EXPERTS = 64
TOP_K = 8
N_EXPERT_GROUPS = 8
GROUP_SIZE = N_EXPERTS // N_EXPERT_GROUPS
TOPK_GROUPS = 4
EXPERT_DIM = 256
ROUTED_SCALE = 2.5
EPS = 1e-6
LANES = 128
NEG_INF = float("-inf")
VMEM_LIMIT = 56 * 1024 * 1024

TM_TOKEN = 512
TQ_ATTN = 256
TK_ATTN = 256
HEADS_PER_STEP = 4
TM_MOE = 2048
TM_FINAL = 1024
DEC_CHUNK_PAGES = 16


def _silu(x):
    return x * jax.nn.sigmoid(x)


def _rmsnorm(x, g):
    ms = jnp.mean(x * x, axis=-1, keepdims=True)
    return x * lax.rsqrt(ms + EPS) * g


def _dot(a, b):
    return jnp.dot(a, b, preferred_element_type=F32)


def _dot_nt(a, b):
    return lax.dot_general(a, b, (((1,), (1,)), ((), ())), preferred_element_type=F32)


def _ada_kernel(c_ref, w_ref, b_ref, o_ref):
    s = _silu(c_ref[...])
    o_ref[...] = _dot(s.astype(BF16), w_ref[...].astype(BF16)) + b_ref[...]


def _ada(c_pad, w, b):
    rows, n = c_pad.shape[0], w.shape[1]
    tn = 1536
    return pl.pallas_call(
        _ada_kernel,
        grid=(n // tn,),
        in_specs=[pl.BlockSpec((rows, D_MODEL), lambda j: (0, 0)),
                  pl.BlockSpec((D_MODEL, tn), lambda j: (0, j)),
                  pl.BlockSpec((1, tn), lambda j: (0, j))],
        out_specs=pl.BlockSpec((rows, tn), lambda j: (0, j)),
        out_shape=jax.ShapeDtypeStruct((rows, n), F32),
        compiler_params=pltpu.CompilerParams(dimension_semantics=("arbitrary",), vmem_limit_bytes=VMEM_LIMIT),
        name="ada_mod",
    )(c_pad, w, b)


def _rope(x, cos, sin_signed):
    outs = []
    for c in range(x.shape[1] // LANES):
        blk = x[:, c * LANES:(c + 1) * LANES]
        lane = lax.broadcasted_iota(jnp.int32, blk.shape, 1)
        partner = jnp.where((lane % HEAD_DIM) < HEAD_DIM // 2,
                            pltpu.roll(blk, LANES - HEAD_DIM // 2, 1),
                            pltpu.roll(blk, HEAD_DIM // 2, 1))
        outs.append(blk * cos + partner * sin_signed)
    return jnp.concatenate(outs, axis=1)


def _premix_kernel(x_ref, sh_ref, sc_ref, g_ref, w_ref, cos_ref, sin_ref,
                   k_ref, v_ref, z_ref, qb_ref, kb_ref, *rest, n_kv):
    h = _rmsnorm(x_ref[...], g_ref[...]) * (1.0 + sc_ref[0]) + sh_ref[0]
    proj = _dot(h.astype(BF16), w_ref[...])
    cos, sin = cos_ref[...], sin_ref[...]
    q = _rope(proj[:, :ATTN_W], cos, sin)
    k = _rope(proj[:, ATTN_W:2 * ATTN_W], cos, sin)
    v = proj[:, 2 * ATTN_W:3 * ATTN_W]
    k_ref[...] = k
    v_ref[...] = v
    z_ref[...] = proj[:, 3 * ATTN_W:]
    qb_ref[...] = (q * (HEAD_DIM ** -0.5)).astype(BF16)
    kb_ref[...] = k.astype(BF16)
    if n_kv:
        vt_ref = rest[0]
        vt = v.T.astype(BF16)
        tk = vt.shape[1] // n_kv
        for hh in range(N_HEADS):
            for c in range(n_kv):
                vt_ref[0, hh, c] = vt[hh * V_DIM:(hh + 1) * V_DIM, c * tk:(c + 1) * tk]


def _premix(x, shift, scale, gain, w_in_b, cos, sin, *, tm, rows_per_mod, seq_tiles, tk):
    t = x.shape[0]
    n_tiles = t // tm
    mod_rows = shift.shape[1]
    mod_idx = (lambda i: (i // rows_per_mod, 0, 0))
    n_kv = tm // tk if tk else 0
    out_shape = [jax.ShapeDtypeStruct((t, ATTN_W), F32), jax.ShapeDtypeStruct((t, ATTN_W), F32),
                 jax.ShapeDtypeStruct((t, POOL_W), F32), jax.ShapeDtypeStruct((t, ATTN_W), BF16),
                 jax.ShapeDtypeStruct((t, ATTN_W), BF16)]
    row_spec = pl.BlockSpec((tm, ATTN_W), lambda i: (i, 0))
    out_specs = [row_spec] * 5
    if n_kv:
        nb = n_tiles // seq_tiles
        out_shape.append(jax.ShapeDtypeStruct((nb, N_HEADS, seq_tiles * n_kv, V_DIM, tk), BF16))
        out_specs.append(pl.BlockSpec((1, N_HEADS, n_kv, V_DIM, tk),
                                      lambda i: (i // seq_tiles, 0, i % seq_tiles, 0, 0)))
    return pl.pallas_call(
        functools.partial(_premix_kernel, n_kv=n_kv),
        grid=(n_tiles,),
        in_specs=[pl.BlockSpec((tm, D_MODEL), lambda i: (i, 0)),
                  pl.BlockSpec((1, mod_rows, D_MODEL), mod_idx),
                  pl.BlockSpec((1, mod_rows, D_MODEL), mod_idx),
                  pl.BlockSpec((1, D_MODEL), lambda i: (0, 0)),
                  pl.BlockSpec((D_MODEL, IN_COLS), lambda i: (0, 0)),
                  pl.BlockSpec((tm, LANES), lambda i: (i % seq_tiles, 0)),
                  pl.BlockSpec((tm, LANES), lambda i: (i % seq_tiles, 0))],
        out_specs=out_specs,
        out_shape=out_shape,
        compiler_params=pltpu.CompilerParams(dimension_semantics=("parallel",), vmem_limit_bytes=VMEM_LIMIT),
        name="premix",
    )(x, shift, scale, gain, w_in_b, cos, sin)


def _diff_lambda(lam_ref, lam_init):
    lam = lam_ref[...]
    s1 = jnp.sum(lam[0:1] * lam[1:2], axis=1, keepdims=True)
    s2 = jnp.sum(lam[2:3] * lam[3:4], axis=1, keepdims=True)
    return jnp.exp(s1) - jnp.exp(s2) + lam_init


def _split_maps(q):
    lane = lax.broadcasted_iota(jnp.int32, q.shape, 1)
    zero = jnp.zeros_like(q)
    return jnp.concatenate([jnp.where(lane < HEAD_DIM, q, zero), jnp.where(lane >= HEAD_DIM, q, zero)], axis=0)


def _attn_kernel(q_ref, k_ref, vt_ref, lam_ref, g_ref, o_ref, *, tq, tk, hps, lam_init):
    i = pl.program_id(2)
    r = tq // tk
    heads = range(hps)
    cols = [slice(hh * V_DIM, (hh + 1) * V_DIM) for hh in heads]
    qpads = [_split_maps(q_ref[0, :, cols[hh]]) for hh in heads]

    def step(j, carry, masked):
        rows = pl.ds(pl.multiple_of(j * tk, tk), tk)
        ss = [_dot_nt(k_ref[0, rows, cols[hh]], qpads[hh]) for hh in heads]
        ps, stats = [], []
        for hh in heads:
            m, l, _ = carry[hh]
            s = ss[hh]
            if masked:
                kpos = j * tk + lax.broadcasted_iota(jnp.int32, s.shape, 0)
                qpos = i * tq + lax.broadcasted_iota(jnp.int32, s.shape, 1) % tq
                s = jnp.where(kpos <= qpos, s, NEG_INF)
            m_new = jnp.maximum(m, jnp.max(s, axis=0, keepdims=True))
            alpha = jnp.exp(m - m_new)
            p = jnp.exp(s - m_new)
            stats.append((m_new, alpha * l + jnp.sum(p, axis=0, keepdims=True), alpha))
            ps.append(p.astype(BF16))
        out = []
        for hh in heads:
            m_new, l_new, alpha = stats[hh]
            acc = alpha * carry[hh][2] + _dot(vt_ref[0, hh, j], ps[hh])
            out.append((m_new, l_new, acc))
        return tuple(out)

    init = tuple((jnp.full((1, 2 * tq), -1e30, F32), jnp.zeros((1, 2 * tq), F32),
                  jnp.zeros((V_DIM, 2 * tq), F32)) for _ in heads)
    carry = lax.fori_loop(0, i * r, lambda j, c: step(j, c, False), init)
    for jj in range(r):
        carry = step(i * r + jj, carry, True)
    lam = _diff_lambda(lam_ref, lam_init)
    for hh in heads:
        _, l, acc = carry[hh]
        o = acc / l
        od = o[:, :tq] - lam * o[:, tq:]
        ms = jnp.mean(od * od, axis=0, keepdims=True)
        on = od * lax.rsqrt(ms + EPS) * g_ref[...] * (1.0 - lam_init)
        o_ref[0, :, cols[hh]] = on.T.astype(BF16)


def _attn_prompt(qb, kb, vt, lam4, subln_col, *, tq, tk, hps, lam_init):
    b, s, _ = qb.shape
    nq, nkv = s // tq, s // tk
    w = hps * V_DIM
    return pl.pallas_call(
        functools.partial(_attn_kernel, tq=tq, tk=tk, hps=hps, lam_init=lam_init),
        grid=(b, N_HEADS // hps, nq),
        in_specs=[pl.BlockSpec((1, tq, w), lambda bb, h, i: (bb, i, h)),
                  pl.BlockSpec((1, s, w), lambda bb, h, i: (bb, 0, h)),
                  pl.BlockSpec((1, hps, nkv, V_DIM, tk), lambda bb, h, i: (bb, h, 0, 0, 0)),
                  pl.BlockSpec((4, HEAD_DIM), lambda bb, h, i: (0, 0)),
                  pl.BlockSpec((V_DIM, 1), lambda bb, h, i: (0, 0))],
        out_specs=pl.BlockSpec((1, tq, w), lambda bb, h, i: (bb, i, h)),
        out_shape=jax.ShapeDtypeStruct((b, s, ATTN_W), BF16),
        compiler_params=pltpu.CompilerParams(dimension_semantics=("parallel", "parallel", "arbitrary"),
                                             vmem_limit_bytes=VMEM_LIMIT),
        name="attn_prompt",
    )(qb, kb, vt, lam4, subln_col)


def _decode_kernel(pt_ref, q_ref, kn_ref, vn_ref, lam_ref, g_ref, k_hbm, v_hbm, o_ref,
                   kbuf, vbuf, sem, m_sc, l_sc, acc_sc, *, n_chunks, chunk_pages, page_rows, lam_init):
    b = pl.program_id(0)
    c = pl.program_id(1)
    nb = pl.num_programs(0)
    step = b * n_chunks + c
    slot = step % 2

    def chunk_copies(bb, cc, sl):
        copies = []
        for p in range(chunk_pages):
            phys = pt_ref[bb * (n_chunks * chunk_pages) + cc * chunk_pages + p]
            src = pl.ds(pl.multiple_of(phys * page_rows, page_rows), page_rows)
            dst = pl.ds(p * page_rows, page_rows)
            copies.append(pltpu.make_async_copy(k_hbm.at[src, :], kbuf.at[sl, dst, :], sem.at[0, sl]))
            copies.append(pltpu.make_async_copy(v_hbm.at[src, :], vbuf.at[sl, dst, :], sem.at[1, sl]))
        return copies

    @pl.when(step == 0)
    def _():
        for cp in chunk_copies(b, c, slot):
            cp.start()

    @pl.when(step + 1 < nb * n_chunks)
    def _():
        nxt = step + 1
        for cp in chunk_copies(nxt // n_chunks, nxt % n_chunks, 1 - slot):
            cp.start()

    @pl.when(c == 0)
    def _():
        m_sc[...] = jnp.full(m_sc.shape, -1e30, F32)
        l_sc[...] = jnp.zeros(l_sc.shape, F32)
        acc_sc[...] = jnp.zeros(acc_sc.shape, F32)

    qblk = _split_maps(q_ref[0])
    for cp in chunk_copies(b, c, slot):
        cp.wait()
    kc = kbuf[slot].astype(BF16)
    vc = vbuf[slot].astype(BF16)
    s = _dot_nt(qblk, kc)
    row_head = lax.broadcasted_iota(jnp.int32, s.shape, 0) % N_HEADS
    col_head = lax.broadcasted_iota(jnp.int32, s.shape, 1) % N_HEADS
    s = jnp.where(row_head == col_head, s, NEG_INF)
    m_old = m_sc[...]
    m_new = jnp.maximum(m_old, jnp.max(s, axis=1, keepdims=True))
    alpha = jnp.exp(m_old - m_new)
    p = jnp.exp(s - m_new)
    l_new = alpha * l_sc[...] + jnp.sum(p, axis=1, keepdims=True)
    acc_new = alpha * acc_sc[...] + _dot(p.astype(BF16), vc)
    m_sc[...] = m_new
    l_sc[...] = l_new
    acc_sc[...] = acc_new

    @pl.when(c == n_chunks - 1)
    def _():
        kn = jnp.concatenate([kn_ref[0], kn_ref[0]], axis=0).astype(F32)
        vn = jnp.concatenate([vn_ref[0], vn_ref[0]], axis=0).astype(F32)
        s_n = jnp.sum(qblk.astype(F32) * kn, axis=1, keepdims=True)
        m_f = jnp.maximum(m_new, s_n)
        a_f = jnp.exp(m_new - m_f)
        p_n = jnp.exp(s_n - m_f)
        l_f = a_f * l_new + p_n
        o = (a_f * acc_new + p_n * vn) / l_f
        od = o[:N_HEADS] - _diff_lambda(lam_ref, lam_init) * o[N_HEADS:]
        o_ref[0] = (_rmsnorm(od, g_ref[...]) * (1.0 - lam_init)).astype(BF16)


def _attn_decode(page_table, q3, kn3, vn3, lam4, subln_row, k2, v2, *, page_rows, lam_init):
    nb, n_pages = page_table.shape
    chunk_pages = DEC_CHUNK_PAGES
    n_chunks = n_pages // chunk_pages
    rows = chunk_pages * page_rows
    head_spec = pl.BlockSpec((1, N_HEADS, V_DIM), lambda b, c, pt: (b, 0, 0))
    grid_spec = pltpu.PrefetchScalarGridSpec(
        num_scalar_prefetch=1,
        grid=(nb, n_chunks),
        in_specs=[head_spec, head_spec, head_spec,
                  pl.BlockSpec((4, HEAD_DIM), lambda b, c, pt: (0, 0)),
                  pl.BlockSpec((1, V_DIM), lambda b, c, pt: (0, 0)),
                  pl.BlockSpec(memory_space=pl.ANY),
                  pl.BlockSpec(memory_space=pl.ANY)],
        out_specs=head_spec,
        scratch_shapes=[pltpu.VMEM((2, rows, V_DIM), F32), pltpu.VMEM((2, rows, V_DIM), F32),
                        pltpu.SemaphoreType.DMA((2, 2)),
                        pltpu.VMEM((2 * N_HEADS, 1), F32), pltpu.VMEM((2 * N_HEADS, 1), F32),
                        pltpu.VMEM((2 * N_HEADS, V_DIM), F32)])
    return pl.pallas_call(
        functools.partial(_decode_kernel, n_chunks=n_chunks, chunk_pages=chunk_pages,
                          page_rows=page_rows, lam_init=lam_init),
        grid_spec=grid_spec,
        out_shape=jax.ShapeDtypeStruct((nb, N_HEADS, V_DIM), BF16),
        compiler_params=pltpu.CompilerParams(dimension_semantics=("arbitrary", "arbitrary"),
                                             vmem_limit_bytes=VMEM_LIMIT),
        name="attn_decode",
    )(page_table.reshape(-1), q3, kn3, vn3, lam4, subln_row, k2, v2)


def _pool_project(d_groups, pw_ref, ps_ref):
    ys = [_dot(d.astype(BF16), pw_ref[g].astype(BF16)) for g, d in enumerate(d_groups)]
    return jnp.concatenate(ys, axis=1) * ps_ref[...]


def _argmax_first(x, iota, n):
    mx = jnp.max(x, axis=0, keepdims=True)
    ix = jnp.min(jnp.where(x == mx, iota, float(n)), axis=0, keepdims=True)
    return mx, ix


def _row_iota(shape):
    return lax.broadcasted_iota(jnp.int32, shape, 0).astype(F32)


def _route(h2, rwt_ref, rb_ref):
    rw = rwt_ref[...]
    rw_hi = rw.astype(BF16)
    rw_lo = (rw - rw_hi.astype(F32)).astype(BF16)
    h_hi = h2.astype(BF16)
    h_lo = (h2 - h_hi.astype(F32)).astype(BF16)
    logits = _dot_nt(rw_hi, h_hi) + (_dot_nt(rw_hi, h_lo) + _dot_nt(rw_lo, h_hi))
    scores = jax.nn.sigmoid(logits)
    sel = scores + rb_ref[...]
    tm = sel.shape[1]
    iota_g = _row_iota((GROUP_SIZE, tm))
    group_rows = []
    for g in range(N_EXPERT_GROUPS):
        xg = sel[g * GROUP_SIZE:(g + 1) * GROUP_SIZE]
        m1, i1 = _argmax_first(xg, iota_g, GROUP_SIZE)
        m2 = jnp.max(jnp.where(iota_g == i1, NEG_INF, xg), axis=0, keepdims=True)
        group_rows.append(m1 + m2)
    gscore = jnp.concatenate(group_rows, axis=0)
    iota_gr = _row_iota(gscore.shape)
    gmask = jnp.zeros(gscore.shape, F32)
    for _ in range(TOPK_GROUPS):
        _, ig = _argmax_first(gscore, iota_gr, N_EXPERT_GROUPS)
        hit = iota_gr == ig
        gmask = jnp.where(hit, 1.0, gmask)
        gscore = jnp.where(hit, NEG_INF, gscore)
    masked = jnp.concatenate(
        [jnp.where(gmask[g:g + 1] > 0.5, sel[g * GROUP_SIZE:(g + 1) * GROUP_SIZE], NEG_INF)
         for g in range(N_EXPERT_GROUPS)], axis=0)
    iota_e = _row_iota(masked.shape)
    w = jnp.zeros(masked.shape, F32)
    for _ in range(TOP_K):
        _, ie = _argmax_first(masked, iota_e, N_EXPERTS)
        hit = iota_e == ie
        w = jnp.where(hit, scores, w)
        masked = jnp.where(hit, NEG_INF, masked)
    w = w / jnp.sum(w, axis=0, keepdims=True) * ROUTED_SCALE
    return w.T


def _mix_tail(o_b, pz, x, g1, sh2, sc2, wout_ref, gpost_ref, gpre_ref, rwt_ref, rb_ref,
              x1_ref, h2_ref, wt_ref):
    m = _dot(o_b, wout_ref[:ATTN_W, :]) + _dot(pz.astype(BF16), wout_ref[ATTN_W:, :])
    x1 = x + g1 * _rmsnorm(m, gpost_ref[...])
    h2 = _rmsnorm(x1, gpre_ref[...]) * (1.0 + sc2) + sh2
    x1_ref[...] = x1
    h2_ref[...] = h2.astype(BF16)
    wt_ref[...] = _route(h2, rwt_ref, rb_ref)


def _postmix_prompt_kernel(o_ref, z_ref, zprev_ref, x_ref, g1_ref, sh2_ref, sc2_ref,
                           wout_ref, pw_ref, ps_ref, gpost_ref, gpre_ref, rwt_ref, rb_ref,
                           x1_ref, h2_ref, wt_ref, zbuf, *, seq_tiles):
    tm = z_ref.shape[0]
    it = pl.program_id(0) % seq_tiles
    zt = z_ref[...]
    zbuf[0:POOL_HALO, :] = jnp.where(it == 0, 0.0, zprev_ref[...])
    zbuf[POOL_HALO:, :] = zt
    pos = it * tm + lax.broadcasted_iota(jnp.int32, (tm, 1), 0)
    d_groups = []
    for g, w in enumerate(POOL_WINDOWS):
        cols = slice(g * POOL_GW, (g + 1) * POOL_GW)
        win = zt[:, cols]
        for s in range(1, w):
            win = win + zbuf[POOL_HALO - s:POOL_HALO - s + tm, cols]
        cnt = jnp.minimum(pos + 1, w).astype(F32)
        d_groups.append(win / cnt - zt[:, cols])
    pz = _pool_project(d_groups, pw_ref, ps_ref)
    _mix_tail(o_ref[...], pz, x_ref[...], g1_ref[0], sh2_ref[0], sc2_ref[0], wout_ref, gpost_ref, gpre_ref,
              rwt_ref, rb_ref, x1_ref, h2_ref, wt_ref)


def _postmix_sample_kernel(o_ref, z_ref, st_ref, x_ref, g1_ref, sh2_ref, sc2_ref,
                           wout_ref, pw_ref, ps_ref, gpost_ref, gpre_ref, rwt_ref, rb_ref,
                           x1_ref, h2_ref, wt_ref, *, past_len):
    zt = z_ref[...]
    d_groups = []
    for g, w in enumerate(POOL_WINDOWS):
        cols = slice(g * POOL_GW, (g + 1) * POOL_GW)
        win = zt[:, cols]
        for s in range(1, w):
            win = win + st_ref[POOL_BUF - s][:, cols]
        d_groups.append(win / float(min(past_len + 1, w)) - zt[:, cols])
    pz = _pool_project(d_groups, pw_ref, ps_ref)
    _mix_tail(o_ref[...], pz, x_ref[...], g1_ref[0], sh2_ref[0], sc2_ref[0], wout_ref, gpost_ref, gpre_ref,
              rwt_ref, rb_ref, x1_ref, h2_ref, wt_ref)


def _postmix(o, z, hist, x, g1, sh2, sc2, wout_b, pool_w, pool_scale, g_post, g_pre, rwt, rb_col,
             *, tm, rows_per_mod, seq_tiles, past_len):
    t = x.shape[0]
    n_tiles = t // tm
    mod_rows = g1.shape[1]
    mod_spec = pl.BlockSpec((1, mod_rows, D_MODEL), lambda i: (i // rows_per_mod, 0, 0))
    full = lambda shape: pl.BlockSpec(shape, lambda i: (0,) * len(shape))
    if past_len is None:
        body = functools.partial(_postmix_prompt_kernel, seq_tiles=seq_tiles)
        halo_blocks = tm // POOL_HALO
        hist_spec = pl.BlockSpec((POOL_HALO, POOL_W), lambda i: (jnp.maximum(i * halo_blocks - 1, 0), 0))
        scratch = [pltpu.VMEM((POOL_HALO + tm, POOL_W), F32)]
    else:
        body = functools.partial(_postmix_sample_kernel, past_len=past_len)
        hist_spec = full(hist.shape)
        scratch = []
    return pl.pallas_call(
        body,
        grid=(n_tiles,),
        in_specs=[pl.BlockSpec((tm, ATTN_W), lambda i: (i, 0)),
                  pl.BlockSpec((tm, POOL_W), lambda i: (i, 0)),
                  hist_spec,
                  pl.BlockSpec((tm, D_MODEL), lambda i: (i, 0)),
                  mod_spec, mod_spec, mod_spec,
                  full((D_MODEL, D_MODEL)), full(pool_w.shape), full((1, POOL_W)),
                  full((1, D_MODEL)), full((1, D_MODEL)), full((N_EXPERTS, D_MODEL)), full((N_EXPERTS, 1))],
        out_specs=[pl.BlockSpec((tm, D_MODEL), lambda i: (i, 0)),
                   pl.BlockSpec((tm, D_MODEL), lambda i: (i, 0)),
                   pl.BlockSpec((tm, N_EXPERTS), lambda i: (i, 0))],
        out_shape=[jax.ShapeDtypeStruct((t, D_MODEL), F32), jax.ShapeDtypeStruct((t, D_MODEL), BF16),
                   jax.ShapeDtypeStruct((t, N_EXPERTS), F32)],
        scratch_shapes=scratch,
        compiler_params=pltpu.CompilerParams(dimension_semantics=("parallel",), vmem_limit_bytes=VMEM_LIMIT),
        name="postmix",
    )(o, z, hist, x, g1, sh2, sc2, wout_b, pool_w, pool_scale, g_post, g_pre, rwt, rb_col)


def _swiglu(xb, wg, wu):
    return _silu(_dot(xb, wg.astype(BF16))) * _dot(xb, wu.astype(BF16))


def _moe_kernel(h_ref, wt_ref, wg_ref, wu_ref, wd_ref, sg_ref, su_ref, sd_ref, f_ref):
    e = pl.program_id(1)
    xb = h_ref[...]

    @pl.when(e == 0)
    def _():
        f_ref[...] = _dot(_swiglu(xb, sg_ref[...], su_ref[...]).astype(BF16), sd_ref[...].astype(BF16))

    wt = wt_ref[...]
    lane = lax.broadcasted_iota(jnp.int32, wt.shape, 1)
    wcol = jnp.sum(jnp.where(lane == e, wt, 0.0), axis=1, keepdims=True)
    hh = _swiglu(xb, wg_ref[0], wu_ref[0]) * wcol
    f_ref[...] += _dot(hh.astype(BF16), wd_ref[0].astype(BF16))


def _moe(h2b, wt, wg, wu, wd, sg, su, sd, *, tm):
    t = h2b.shape[0]
    full = lambda shape: pl.BlockSpec(shape, lambda i, e: (0,) * len(shape))
    return pl.pallas_call(
        _moe_kernel,
        grid=(t // tm, N_EXPERTS),
        in_specs=[pl.BlockSpec((tm, D_MODEL), lambda i, e: (i, 0)),
                  pl.BlockSpec((tm, N_EXPERTS), lambda i, e: (i, 0)),
                  pl.BlockSpec((1, D_MODEL, EXPERT_DIM), lambda i, e: (e, 0, 0)),
                  pl.BlockSpec((1, D_MODEL, EXPERT_DIM), lambda i, e: (e, 0, 0)),
                  pl.BlockSpec((1, EXPERT_DIM, D_MODEL), lambda i, e: (e, 0, 0)),
                  full(sg.shape), full(su.shape), full(sd.shape)],
        out_specs=pl.BlockSpec((tm, D_MODEL), lambda i, e: (i, 0)),
        out_shape=jax.ShapeDtypeStruct((t, D_MODEL), F32),
        compiler_params=pltpu.CompilerParams(dimension_semantics=("parallel", "arbitrary"),
                                             vmem_limit_bytes=VMEM_LIMIT),
        name="moe_experts",
    )(h2b, wt, wg, wu, wd, sg, su, sd)


def _final_kernel(x1_ref, f_ref, g2_ref, gain_ref, y_ref):
    y_ref[...] = x1_ref[...] + g2_ref[0] * _rmsnorm(f_ref[...], gain_ref[...])


def _final(x1, f, g2, gain, *, tm, rows_per_mod):
    t = x1.shape[0]
    mod_rows = g2.shape[1]
    row = pl.BlockSpec((tm, D_MODEL), lambda i: (i, 0))
    return pl.pallas_call(
        _final_kernel,
        grid=(t // tm,),
        in_specs=[row, row,
                  pl.BlockSpec((1, mod_rows, D_MODEL), lambda i: (i // rows_per_mod, 0, 0)),
                  pl.BlockSpec((1, D_MODEL), lambda i: (0, 0))],
        out_specs=row,
        out_shape=jax.ShapeDtypeStruct((t, D_MODEL), F32),
        compiler_params=pltpu.CompilerParams(dimension_semantics=("parallel",), vmem_limit_bytes=VMEM_LIMIT),
        name="final_residual",
    )(x1, f, g2, gain)


def _rope_tables(pos):
    half = HEAD_DIM // 2
    inv_freq = ROPE_THETA ** (-jnp.arange(half, dtype=F32) / half)
    ang = pos.astype(F32)[:, None] * inv_freq[None, :]
    cos, sin = jnp.cos(ang), jnp.sin(ang)
    reps = LANES // HEAD_DIM
    return jnp.tile(jnp.concatenate([cos, cos], axis=1), (1, reps)), jnp.tile(jnp.concatenate([-sin, sin], axis=1), (1, reps))


def kernel(x_prompt, x_sample, c_prompt, c_sample, cache_k, cache_v, state_pool, page_table, ada_w, ada_b, pre_mix_g, post_mix_g, pre_ffn_g, post_ffn_g, w_in, w_out, lambda_q1, lambda_k1, lambda_q2, lambda_k2, subln_g, pool_w, pool_scale, router_w, router_bias, expert_w_gate, expert_w_up, expert_w_down, shared_w_gate, shared_w_up, shared_w_down):
    depth = ada_w.shape[0]
    assert depth == 1, "single-layer step"
    nb, seq, _ = x_prompt.shape
    db, dec_seq, _ = x_sample.shape
    assert dec_seq == 1
    page_size = cache_k.shape[2]
    n_pages = page_table.shape[1]
    past_len = n_pages * page_size
    lam_init = 0.8 - 0.6 * math.exp(-0.3 * 0)
    l = 0

    c_all = jnp.concatenate([c_prompt, c_sample], axis=0)
    pad = (-c_all.shape[0]) % 16
    mod = _ada(jnp.pad(c_all, ((0, pad), (0, 0))), ada_w[l], ada_b[l][None, :])
    mod_p = [m.reshape(nb, 1, D_MODEL) for m in jnp.split(mod[:nb], 6, axis=-1)]
    mod_s = [m.reshape(1, db, D_MODEL) for m in jnp.split(mod[nb:nb + db], 6, axis=-1)]

    w_in_b = w_in[l].astype(BF16)
    w_out_b = w_out[l].astype(BF16)
    lam4 = jnp.stack([lambda_q1[l], lambda_k1[l], lambda_q2[l], lambda_k2[l]])
    row = lambda v: v.reshape(1, -1)
    rwt = router_w[l].T
    rb_col = router_bias[l].reshape(N_EXPERTS, 1)
    experts = (expert_w_gate[l], expert_w_up[l], expert_w_down[l], shared_w_gate[l], shared_w_up[l], shared_w_down[l])

    t_p = nb * seq
    tm = TM_TOKEN
    seq_tiles = seq // tm
    cos_p, sin_p = _rope_tables(jnp.arange(seq))
    xp = x_prompt.reshape(t_p, D_MODEL)
    k_p, v_p, z_p, qb, kb, vt = _premix(xp, mod_p[0], mod_p[1], row(pre_mix_g[l]), w_in_b, cos_p, sin_p,
                                        tm=tm, rows_per_mod=seq_tiles, seq_tiles=seq_tiles, tk=TK_ATTN)
    o_p = _attn_prompt(qb.reshape(nb, seq, ATTN_W), kb.reshape(nb, seq, ATTN_W), vt, lam4,
                       subln_g[l].reshape(V_DIM, 1), tq=TQ_ATTN, tk=TK_ATTN, hps=HEADS_PER_STEP,
                       lam_init=lam_init)
    x1_p, h2_p, wt_p = _postmix(o_p.reshape(t_p, ATTN_W), z_p, z_p, xp, mod_p[2], mod_p[3], mod_p[4], w_out_b,
                                pool_w[l], row(pool_scale[l]), row(post_mix_g[l]), row(pre_ffn_g[l]), rwt, rb_col,
                                tm=tm, rows_per_mod=seq_tiles, seq_tiles=seq_tiles, past_len=None)
    f_p = _moe(h2_p, wt_p, *experts, tm=TM_MOE)
    y_p = _final(x1_p, f_p, mod_p[5], row(post_ffn_g[l]), tm=TM_FINAL, rows_per_mod=seq // TM_FINAL)

    cos_s, sin_s = _rope_tables(jnp.full((db,), past_len))
    xs = x_sample.reshape(db, D_MODEL)
    k_s, v_s, z_s, qb_s, kb_s = _premix(xs, mod_s[0], mod_s[1], row(pre_mix_g[l]), w_in_b, cos_s, sin_s,
                                        tm=db, rows_per_mod=1, seq_tiles=1, tk=0)
    page_rows = page_size * N_HEADS
    k2 = cache_k[l].reshape(-1, V_DIM)
    v2 = cache_v[l].reshape(-1, V_DIM)
    o_s = _attn_decode(page_table, qb_s.reshape(db, N_HEADS, V_DIM), kb_s.reshape(db, N_HEADS, V_DIM),
                       v_s.reshape(db, N_HEADS, V_DIM), lam4, row(subln_g[l]), k2, v2,
                       page_rows=page_rows, lam_init=lam_init)
    hist_s = jnp.transpose(state_pool[l], (1, 0, 2))
    x1_s, h2_s, wt_s = _postmix(o_s.reshape(db, ATTN_W), z_s, hist_s, xs, mod_s[2], mod_s[3], mod_s[4], w_out_b,
                                pool_w[l], row(pool_scale[l]), row(post_mix_g[l]), row(pre_ffn_g[l]), rwt, rb_col,
                                tm=db, rows_per_mod=1, seq_tiles=1, past_len=past_len)
    f_s = _moe(h2_s, wt_s, *experts, tm=db)
    y_s = _final(x1_s, f_s, mod_s[5], row(post_ffn_g[l]), tm=db, rows_per_mod=1)

    pool_p = z_p.reshape(nb, seq, POOL_W)[:, seq - POOL_BUF:]
    pool_s = jnp.concatenate([state_pool[l][:, 1:], z_s[:, None, :]], axis=1)
    return (y_p.reshape(nb, seq, D_MODEL), y_s.reshape(db, 1, D_MODEL),
            k_p.reshape(1, nb, seq, N_HEADS, V_DIM), v_p.reshape(1, nb, seq, N_HEADS, V_DIM), pool_p[None],
            k_s.reshape(1, db, 1, N_HEADS, V_DIM), v_s.reshape(1, db, 1, N_HEADS, V_DIM), pool_s[None])
```

```python
import functools
import math

import jax
import jax.numpy as jnp
from jax import lax
from jax.experimental import pallas as pl
from jax.experimental.pallas import tpu as pltpu

F32 = jnp.float32
BF16 = jnp.bfloat16

D_MODEL = 1024
ATTN_W = 512
POOL_W = 512
HEAD_DIM = 64
N_HEADS = 4
V_DIM = 2 * HEAD_DIM
IN_COLS = 3 * ATTN_W + POOL_W
POOL_WINDOWS = (2, 4, 8, 16)
POOL_GW = 128
POOL_BUF = 15
POOL_HALO = 16
ROPE_THETA = 10000.0
N_EXPERTS = 64
TOP_K = 8
N_EXPERT_GROUPS = 8
GROUP_SIZE = N_EXPERTS // N_EXPERT_GROUPS
TOPK_GROUPS = 4
EXPERT_DIM = 256
ROUTED_SCALE = 2.5
EPS = 1e-6
LANES = 128
NEG_INF = float("-inf")
VMEM_LIMIT = 56 * 1024 * 1024

TM_TOKEN = 512
TQ_ATTN = 256
TK_ATTN = 256
HEADS_PER_STEP = 4
MOE_BM = 128
SCATTER_BATCH = 8
TM_FINAL = 512
DEC_CHUNK_PAGES = 16


def _silu(x):
    return x * jax.nn.sigmoid(x)


def _rmsnorm(x, g):
    ms = jnp.mean(x * x, axis=-1, keepdims=True)
    return x * lax.rsqrt(ms + EPS) * g


def _dot(a, b):
    return jnp.dot(a, b, preferred_element_type=F32)


def _dot_nt(a, b):
    return lax.dot_general(a, b, (((1,), (1,)), ((), ())), preferred_element_type=F32)


def _ada_kernel(c_ref, w_ref, b_ref, o_ref):
    s = _silu(c_ref[...])
    o_ref[...] = _dot(s.astype(BF16), w_ref[...].astype(BF16)) + b_ref[...]


def _ada(c_pad, w, b):
    rows, n = c_pad.shape[0], w.shape[1]
    tn = 1536
    return pl.pallas_call(
        _ada_kernel,
        grid=(n // tn,),
        in_specs=[pl.BlockSpec((rows, D_MODEL), lambda j: (0, 0)),
                  pl.BlockSpec((D_MODEL, tn), lambda j: (0, j)),
                  pl.BlockSpec((1, tn), lambda j: (0, j))],
        out_specs=pl.BlockSpec((rows, tn), lambda j: (0, j)),
        out_shape=jax.ShapeDtypeStruct((rows, n), F32),
        compiler_params=pltpu.CompilerParams(dimension_semantics=("arbitrary",), vmem_limit_bytes=VMEM_LIMIT),
        name="ada_mod",
    )(c_pad, w, b)


def _rope(x, cos, sin_signed):
    outs = []
    for c in range(x.shape[1] // LANES):
        blk = x[:, c * LANES:(c + 1) * LANES]
        lane = lax.broadcasted_iota(jnp.int32, blk.shape, 1)
        partner = jnp.where((lane % HEAD_DIM) < HEAD_DIM // 2,
                            pltpu.roll(blk, LANES - HEAD_DIM // 2, 1),
                            pltpu.roll(blk, HEAD_DIM // 2, 1))
        outs.append(blk * cos + partner * sin_signed)
    return jnp.concatenate(outs, axis=1)


def _premix_kernel(x_ref, sh_ref, sc_ref, g_ref, w_ref, cos_ref, sin_ref,
                   k_ref, v_ref, z_ref, qb_ref, kb_ref, *rest, n_kv):
    h = _rmsnorm(x_ref[...], g_ref[...]) * (1.0 + sc_ref[0]) + sh_ref[0]
    proj = _dot(h.astype(BF16), w_ref[...])
    cos, sin = cos_ref[...], sin_ref[...]
    q = _rope(proj[:, :ATTN_W], cos, sin)
    k = _rope(proj[:, ATTN_W:2 * ATTN_W], cos, sin)
    v = proj[:, 2 * ATTN_W:3 * ATTN_W]
    for out_ref, val in ((k_ref, k), (v_ref, v)):
        for i in range(val.shape[0] // 8):
            for hh in range(N_HEADS):
                out_ref[pl.ds(i * 8 * N_HEADS + hh, 8, stride=N_HEADS), :] = val[8 * i:8 * i + 8, hh * V_DIM:(hh + 1) * V_DIM]
    z_ref[...] = proj[:, 3 * ATTN_W:]
    qb_ref[...] = (q * (HEAD_DIM ** -0.5)).astype(BF16)
    kb_ref[...] = k.astype(BF16)
    if n_kv:
        vt_ref = rest[0]
        vt = v.T.astype(BF16)
        tk = vt.shape[1] // n_kv
        for hh in range(N_HEADS):
            for c in range(n_kv):
                vt_ref[0, hh, c] = vt[hh * V_DIM:(hh + 1) * V_DIM, c * tk:(c + 1) * tk]


def _premix(x, shift, scale, gain, w_in_b, cos, sin, *, tm, rows_per_mod, seq_tiles, tk):
    t = x.shape[0]
    n_tiles = t // tm
    mod_rows = shift.shape[1]
    mod_idx = (lambda i: (i // rows_per_mod, 0, 0))
    n_kv = tm // tk if tk else 0
    out_shape = [jax.ShapeDtypeStruct((t * N_HEADS, V_DIM), F32), jax.ShapeDtypeStruct((t * N_HEADS, V_DIM), F32),
                 jax.ShapeDtypeStruct((t, POOL_W), F32), jax.ShapeDtypeStruct((t, ATTN_W), BF16),
                 jax.ShapeDtypeStruct((t, ATTN_W), BF16)]
    row_spec = pl.BlockSpec((tm, ATTN_W), lambda i: (i, 0))
    head_rows = pl.BlockSpec((tm * N_HEADS, V_DIM), lambda i: (i, 0))
    out_specs = [head_rows, head_rows, row_spec, row_spec, row_spec]
    if n_kv:
        nb = n_tiles // seq_tiles
        out_shape.append(jax.ShapeDtypeStruct((nb, N_HEADS, seq_tiles * n_kv, V_DIM, tk), BF16))
        out_specs.append(pl.BlockSpec((1, N_HEADS, n_kv, V_DIM, tk),
                                      lambda i: (i // seq_tiles, 0, i % seq_tiles, 0, 0)))
    return pl.pallas_call(
        functools.partial(_premix_kernel, n_kv=n_kv),
        grid=(n_tiles,),
        in_specs=[pl.BlockSpec((tm, D_MODEL), lambda i: (i, 0)),
                  pl.BlockSpec((1, mod_rows, D_MODEL), mod_idx),
                  pl.BlockSpec((1, mod_rows, D_MODEL), mod_idx),
                  pl.BlockSpec((1, D_MODEL), lambda i: (0, 0)),
                  pl.BlockSpec((D_MODEL, IN_COLS), lambda i: (0, 0)),
                  pl.BlockSpec((tm, LANES), lambda i: (i % seq_tiles, 0)),
                  pl.BlockSpec((tm, LANES), lambda i: (i % seq_tiles, 0))],
        out_specs=out_specs,
        out_shape=out_shape,
        compiler_params=pltpu.CompilerParams(dimension_semantics=("parallel",), vmem_limit_bytes=VMEM_LIMIT),
        name="premix",
    )(x, shift, scale, gain, w_in_b, cos, sin)


def _diff_lambda(lam_ref, lam_init):
    lam = lam_ref[...]
    s1 = jnp.sum(lam[0:1] * lam[1:2], axis=1, keepdims=True)
    s2 = jnp.sum(lam[2:3] * lam[3:4], axis=1, keepdims=True)
    return jnp.exp(s1) - jnp.exp(s2) + lam_init


def _split_maps(q):
    lane = lax.broadcasted_iota(jnp.int32, q.shape, 1)
    zero = jnp.zeros_like(q)
    return jnp.concatenate([jnp.where(lane < HEAD_DIM, q, zero), jnp.where(lane >= HEAD_DIM, q, zero)], axis=0)


def _attn_kernel(q_ref, k_ref, vt_ref, lam_ref, g_ref, o_ref, *, tq, tk, hps, lam_init):
    i = pl.program_id(2)
    r = tq // tk
    heads = range(hps)
    cols = [slice(hh * V_DIM, (hh + 1) * V_DIM) for hh in heads]
    qpads = [_split_maps(q_ref[0, :, cols[hh]]) for hh in heads]

    def step(j, carry, masked):
        rows = pl.ds(pl.multiple_of(j * tk, tk), tk)
        ss = [_dot_nt(k_ref[0, rows, cols[hh]], qpads[hh]) for hh in heads]
        ps, stats = [], []
        for hh in heads:
            m, l, _ = carry[hh]
            s = ss[hh]
            if masked:
                kpos = j * tk + lax.broadcasted_iota(jnp.int32, s.shape, 0)
                qpos = i * tq + lax.broadcasted_iota(jnp.int32, s.shape, 1) % tq
                s = jnp.where(kpos <= qpos, s, NEG_INF)
            m_new = jnp.maximum(m, jnp.max(s, axis=0, keepdims=True))
            alpha = jnp.exp(m - m_new)
            p = jnp.exp(s - m_new)
            stats.append((m_new, alpha * l + jnp.sum(p, axis=0, keepdims=True), alpha))
            ps.append(p.astype(BF16))
        out = []
        for hh in heads:
            m_new, l_new, alpha = stats[hh]
            acc = alpha * carry[hh][2] + _dot(vt_ref[0, hh, j], ps[hh])
            out.append((m_new, l_new, acc))
        return tuple(out)

    init = tuple((jnp.full((1, 2 * tq), -1e30, F32), jnp.zeros((1, 2 * tq), F32),
                  jnp.zeros((V_DIM, 2 * tq), F32)) for _ in heads)
    carry = lax.fori_loop(0, i * r, lambda j, c: step(j, c, False), init)
    for jj in range(r):
        carry = step(i * r + jj, carry, True)
    lam = _diff_lambda(lam_ref, lam_init)
    for hh in heads:
        _, l, acc = carry[hh]
        o = acc / l
        od = o[:, :tq] - lam * o[:, tq:]
        ms = jnp.mean(od * od, axis=0, keepdims=True)
        on = od * lax.rsqrt(ms + EPS) * g_ref[...] * (1.0 - lam_init)
        o_ref[0, :, cols[hh]] = on.T.astype(BF16)


def _attn_prompt(qb, kb, vt, lam4, subln_col, *, tq, tk, hps, lam_init):
    b, s, _ = qb.shape
    nq, nkv = s // tq, s // tk
    w = hps * V_DIM
    return pl.pallas_call(
        functools.partial(_attn_kernel, tq=tq, tk=tk, hps=hps, lam_init=lam_init),
        grid=(b, N_HEADS // hps, nq),
        in_specs=[pl.BlockSpec((1, tq, w), lambda bb, h, i: (bb, i, h)),
                  pl.BlockSpec((1, s, w), lambda bb, h, i: (bb, 0, h)),
                  pl.BlockSpec((1, hps, nkv, V_DIM, tk), lambda bb, h, i: (bb, h, 0, 0, 0)),
                  pl.BlockSpec((4, HEAD_DIM), lambda bb, h, i: (0, 0)),
                  pl.BlockSpec((V_DIM, 1), lambda bb, h, i: (0, 0))],
        out_specs=pl.BlockSpec((1, tq, w), lambda bb, h, i: (bb, i, h)),
        out_shape=jax.ShapeDtypeStruct((b, s, ATTN_W), BF16),
        compiler_params=pltpu.CompilerParams(dimension_semantics=("parallel", "parallel", "arbitrary"),
                                             vmem_limit_bytes=VMEM_LIMIT),
        name="attn_prompt",
    )(qb, kb, vt, lam4, subln_col)


def _decode_kernel(pt_ref, q_ref, kn_ref, vn_ref, lam_ref, g_ref, k_hbm, v_hbm, o_ref,
                   kbuf, vbuf, sem, m_sc, l_sc, acc_sc, *, n_chunks, chunk_pages, page_rows, lam_init):
    b = pl.program_id(0)
    c = pl.program_id(1)
    nb = pl.num_programs(0)
    step = b * n_chunks + c
    slot = step % 2

    def chunk_copies(bb, cc, sl):
        copies = []
        for p in range(chunk_pages):
            phys = pt_ref[bb * (n_chunks * chunk_pages) + cc * chunk_pages + p]
            src = pl.ds(pl.multiple_of(phys * page_rows, page_rows), page_rows)
            dst = pl.ds(p * page_rows, page_rows)
            copies.append(pltpu.make_async_copy(k_hbm.at[src, :], kbuf.at[sl, dst, :], sem.at[0, sl]))
            copies.append(pltpu.make_async_copy(v_hbm.at[src, :], vbuf.at[sl, dst, :], sem.at[1, sl]))
        return copies

    @pl.when(step == 0)
    def _():
        for cp in chunk_copies(b, c, slot):
            cp.start()

    @pl.when(step + 1 < nb * n_chunks)
    def _():
        nxt = step + 1
        for cp in chunk_copies(nxt // n_chunks, nxt % n_chunks, 1 - slot):
            cp.start()

    @pl.when(c == 0)
    def _():
        m_sc[...] = jnp.full(m_sc.shape, -1e30, F32)
        l_sc[...] = jnp.zeros(l_sc.shape, F32)
        acc_sc[...] = jnp.zeros(acc_sc.shape, F32)

    qblk = _split_maps(q_ref[0])
    for cp in chunk_copies(b, c, slot):
        cp.wait()
    kc = kbuf[slot].astype(BF16)
    vc = vbuf[slot].astype(BF16)
    s = _dot_nt(qblk, kc)
    row_head = lax.broadcasted_iota(jnp.int32, s.shape, 0) % N_HEADS
    col_head = lax.broadcasted_iota(jnp.int32, s.shape, 1) % N_HEADS
    s = jnp.where(row_head == col_head, s, NEG_INF)
    m_old = m_sc[...]
    m_new = jnp.maximum(m_old, jnp.max(s, axis=1, keepdims=True))
    alpha = jnp.exp(m_old - m_new)
    p = jnp.exp(s - m_new)
    l_new = alpha * l_sc[...] + jnp.sum(p, axis=1, keepdims=True)
    acc_new = alpha * acc_sc[...] + _dot(p.astype(BF16), vc)
    m_sc[...] = m_new
    l_sc[...] = l_new
    acc_sc[...] = acc_new

    @pl.when(c == n_chunks - 1)
    def _():
        kn = jnp.concatenate([kn_ref[0], kn_ref[0]], axis=0).astype(F32)
        vn = jnp.concatenate([vn_ref[0], vn_ref[0]], axis=0).astype(F32)
        s_n = jnp.sum(qblk.astype(F32) * kn, axis=1, keepdims=True)
        m_f = jnp.maximum(m_new, s_n)
        a_f = jnp.exp(m_new - m_f)
        p_n = jnp.exp(s_n - m_f)
        l_f = a_f * l_new + p_n
        o = (a_f * acc_new + p_n * vn) / l_f
        od = o[:N_HEADS] - _diff_lambda(lam_ref, lam_init) * o[N_HEADS:]
        o_ref[0] = (_rmsnorm(od, g_ref[...]) * (1.0 - lam_init)).astype(BF16)


def _attn_decode(page_table, q3, kn3, vn3, lam4, subln_row, k2, v2, *, page_rows, lam_init):
    nb, n_pages = page_table.shape
    chunk_pages = DEC_CHUNK_PAGES
    n_chunks = n_pages // chunk_pages
    rows = chunk_pages * page_rows
    head_spec = pl.BlockSpec((1, N_HEADS, V_DIM), lambda b, c, pt: (b, 0, 0))
    grid_spec = pltpu.PrefetchScalarGridSpec(
        num_scalar_prefetch=1,
        grid=(nb, n_chunks),
        in_specs=[head_spec, head_spec, head_spec,
                  pl.BlockSpec((4, HEAD_DIM), lambda b, c, pt: (0, 0)),
                  pl.BlockSpec((1, V_DIM), lambda b, c, pt: (0, 0)),
                  pl.BlockSpec(memory_space=pl.ANY),
                  pl.BlockSpec(memory_space=pl.ANY)],
        out_specs=head_spec,
        scratch_shapes=[pltpu.VMEM((2, rows, V_DIM), F32), pltpu.VMEM((2, rows, V_DIM), F32),
                        pltpu.SemaphoreType.DMA((2, 2)),
                        pltpu.VMEM((2 * N_HEADS, 1), F32), pltpu.VMEM((2 * N_HEADS, 1), F32),
                        pltpu.VMEM((2 * N_HEADS, V_DIM), F32)])
    return pl.pallas_call(
        functools.partial(_decode_kernel, n_chunks=n_chunks, chunk_pages=chunk_pages,
                          page_rows=page_rows, lam_init=lam_init),
        grid_spec=grid_spec,
        out_shape=jax.ShapeDtypeStruct((nb, N_HEADS, V_DIM), BF16),
        compiler_params=pltpu.CompilerParams(dimension_semantics=("arbitrary", "arbitrary"),
                                             vmem_limit_bytes=VMEM_LIMIT),
        name="attn_decode",
    )(page_table.reshape(-1), q3, kn3, vn3, lam4, subln_row, k2, v2)


def _pool_project(d_groups, pw_ref, ps_ref):
    ys = [_dot(d.astype(BF16), pw_ref[g].astype(BF16)) for g, d in enumerate(d_groups)]
    return jnp.concatenate(ys, axis=1) * ps_ref[...]


def _argmax_first(x, iota, n):
    mx = jnp.max(x, axis=0, keepdims=True)
    ix = jnp.min(jnp.where(x == mx, iota, float(n)), axis=0, keepdims=True)
    return mx, ix


def _row_iota(shape):
    return lax.broadcasted_iota(jnp.int32, shape, 0).astype(F32)


def _route(h2, rwt_ref, rb_ref):
    rw = rwt_ref[...]
    rw_hi = rw.astype(BF16)
    rw_lo = (rw - rw_hi.astype(F32)).astype(BF16)
    h_hi = h2.astype(BF16)
    h_lo = (h2 - h_hi.astype(F32)).astype(BF16)
    logits = _dot_nt(rw_hi, h_hi) + (_dot_nt(rw_hi, h_lo) + _dot_nt(rw_lo, h_hi))
    scores = jax.nn.sigmoid(logits)
    sel = scores + rb_ref[...]
    tm = sel.shape[1]
    iota_g = _row_iota((GROUP_SIZE, tm))
    group_rows = []
    for g in range(N_EXPERT_GROUPS):
        xg = sel[g * GROUP_SIZE:(g + 1) * GROUP_SIZE]
        m1, i1 = _argmax_first(xg, iota_g, GROUP_SIZE)
        m2 = jnp.max(jnp.where(iota_g == i1, NEG_INF, xg), axis=0, keepdims=True)
        group_rows.append(m1 + m2)
    gscore = jnp.concatenate(group_rows, axis=0)
    iota_gr = _row_iota(gscore.shape)
    gmask = jnp.zeros(gscore.shape, F32)
    for _ in range(TOPK_GROUPS):
        _, ig = _argmax_first(gscore, iota_gr, N_EXPERT_GROUPS)
        hit = iota_gr == ig
        gmask = jnp.where(hit, 1.0, gmask)
        gscore = jnp.where(hit, NEG_INF, gscore)
    masked = jnp.concatenate(
        [jnp.where(gmask[g:g + 1] > 0.5, sel[g * GROUP_SIZE:(g + 1) * GROUP_SIZE], NEG_INF)
         for g in range(N_EXPERT_GROUPS)], axis=0)
    iota_e = _row_iota(masked.shape)
    w = jnp.zeros(masked.shape, F32)
    idx_rows, score_rows = [], []
    for _ in range(TOP_K):
        _, ie = _argmax_first(masked, iota_e, N_EXPERTS)
        hit = iota_e == ie
        w = jnp.where(hit, scores, w)
        masked = jnp.where(hit, NEG_INF, masked)
        idx_rows.append(ie)
        score_rows.append(jnp.sum(jnp.where(hit, scores, 0.0), axis=0, keepdims=True))
    wsum = jnp.sum(w, axis=0, keepdims=True)
    dense = w / wsum * ROUTED_SCALE
    idx_t = jnp.concatenate(idx_rows, axis=0).astype(jnp.int32)
    w_t = jnp.concatenate(score_rows, axis=0) / wsum * ROUTED_SCALE
    return dense, idx_t, w_t


def _to_token_rows(x, out_ref):
    n_chunks = x.shape[1] // LANES
    for i in range(x.shape[0] // 8):
        for j in range(n_chunks):
            out_ref[0, pl.ds(i * 8 * n_chunks + j, 8, stride=n_chunks), :] = x[8 * i:8 * i + 8, j * LANES:(j + 1) * LANES]


def _from_token_rows(in_ref, tm):
    n_chunks = D_MODEL // LANES
    return jnp.concatenate([in_ref[0, pl.ds(j, tm, stride=n_chunks), :] for j in range(n_chunks)], axis=1)


def _mix_tail(o_b, pz, x, g1, sh2, sc2, wout_ref, gpost_ref, gpre_ref, rwt_ref, rb_ref, x1_ref, h2_ref, route_refs):
    m = _dot(o_b, wout_ref[:ATTN_W, :]) + _dot(pz.astype(BF16), wout_ref[ATTN_W:, :])
    x1 = x + g1 * _rmsnorm(m, gpost_ref[...])
    h2 = _rmsnorm(x1, gpre_ref[...]) * (1.0 + sc2) + sh2
    x1_ref[...] = x1
    h2_ref[...] = h2.astype(BF16)
    dense, idx_t, w_t = _route(h2, rwt_ref, rb_ref)
    if len(route_refs) == 1:
        route_refs[0][...] = dense.T
    else:
        idx_ref, w_ref, h2r_ref = route_refs
        idx_ref[...] = idx_t
        w_ref[...] = w_t
        _to_token_rows(h2, h2r_ref)


def _postmix_prompt_kernel(o_ref, z_ref, zprev_ref, x_ref, g1_ref, sh2_ref, sc2_ref,
                           wout_ref, pw_ref, ps_ref, gpost_ref, gpre_ref, rwt_ref, rb_ref,
                           x1_ref, h2_ref, idx_ref, w_ref, h2r_ref, zbuf, *, seq_tiles):
    tm = z_ref.shape[0]
    it = pl.program_id(0) % seq_tiles
    zt = z_ref[...]
    zbuf[0:POOL_HALO, :] = jnp.where(it == 0, 0.0, zprev_ref[...])
    zbuf[POOL_HALO:, :] = zt
    pos = it * tm + lax.broadcasted_iota(jnp.int32, (tm, 1), 0)
    d_groups = []
    for g, w in enumerate(POOL_WINDOWS):
        cols = slice(g * POOL_GW, (g + 1) * POOL_GW)
        win = zt[:, cols]
        for s in range(1, w):
            win = win + zbuf[POOL_HALO - s:POOL_HALO - s + tm, cols]
        cnt = jnp.minimum(pos + 1, w).astype(F32)
        d_groups.append(win / cnt - zt[:, cols])
    pz = _pool_project(d_groups, pw_ref, ps_ref)
    _mix_tail(o_ref[...], pz, x_ref[...], g1_ref[0], sh2_ref[0], sc2_ref[0], wout_ref, gpost_ref, gpre_ref,
              rwt_ref, rb_ref, x1_ref, h2_ref, (idx_ref, w_ref, h2r_ref))


def _postmix_sample_kernel(o_ref, z_ref, st_ref, x_ref, g1_ref, sh2_ref, sc2_ref,
                           wout_ref, pw_ref, ps_ref, gpost_ref, gpre_ref, rwt_ref, rb_ref,
                           x1_ref, h2_ref, wt_ref, *, past_len):
    zt = z_ref[...]
    d_groups = []
    for g, w in enumerate(POOL_WINDOWS):
        cols = slice(g * POOL_GW, (g + 1) * POOL_GW)
        win = zt[:, cols]
        for s in range(1, w):
            win = win + st_ref[POOL_BUF - s][:, cols]
        d_groups.append(win / float(min(past_len + 1, w)) - zt[:, cols])
    pz = _pool_project(d_groups, pw_ref, ps_ref)
    _mix_tail(o_ref[...], pz, x_ref[...], g1_ref[0], sh2_ref[0], sc2_ref[0], wout_ref, gpost_ref, gpre_ref,
              rwt_ref, rb_ref, x1_ref, h2_ref, (wt_ref,))


def _postmix(o, z, hist, x, g1, sh2, sc2, wout_b, pool_w, pool_scale, g_post, g_pre, rwt, rb_col,
             *, tm, rows_per_mod, seq_tiles, past_len):
    t = x.shape[0]
    n_tiles = t // tm
    mod_rows = g1.shape[1]
    mod_spec = pl.BlockSpec((1, mod_rows, D_MODEL), lambda i: (i // rows_per_mod, 0, 0))
    full = lambda shape: pl.BlockSpec(shape, lambda i: (0,) * len(shape))
    if past_len is None:
        body = functools.partial(_postmix_prompt_kernel, seq_tiles=seq_tiles)
        halo_blocks = tm // POOL_HALO
        hist_spec = pl.BlockSpec((POOL_HALO, POOL_W), lambda i: (jnp.maximum(i * halo_blocks - 1, 0), 0))
        scratch = [pltpu.VMEM((POOL_HALO + tm, POOL_W), F32)]
        n_seq = n_tiles // seq_tiles
        rows8 = D_MODEL // LANES
        route_specs = [pl.BlockSpec((TOP_K, tm), lambda i: (0, i)), pl.BlockSpec((TOP_K, tm), lambda i: (0, i)),
                       pl.BlockSpec((1, tm * rows8, LANES), lambda i: (i // seq_tiles, i % seq_tiles, 0))]
        route_shapes = [jax.ShapeDtypeStruct((TOP_K, t), jnp.int32), jax.ShapeDtypeStruct((TOP_K, t), F32),
                        jax.ShapeDtypeStruct((n_seq, seq_tiles * tm * rows8, LANES), F32)]
    else:
        body = functools.partial(_postmix_sample_kernel, past_len=past_len)
        hist_spec = full(hist.shape)
        scratch = []
        route_specs = [pl.BlockSpec((tm, N_EXPERTS), lambda i: (i, 0))]
        route_shapes = [jax.ShapeDtypeStruct((t, N_EXPERTS), F32)]
    return pl.pallas_call(
        body,
        grid=(n_tiles,),
        in_specs=[pl.BlockSpec((tm, ATTN_W), lambda i: (i, 0)),
                  pl.BlockSpec((tm, POOL_W), lambda i: (i, 0)),
                  hist_spec,
                  pl.BlockSpec((tm, D_MODEL), lambda i: (i, 0)),
                  mod_spec, mod_spec, mod_spec,
                  full((D_MODEL, D_MODEL)), full(pool_w.shape), full((1, POOL_W)),
                  full((1, D_MODEL)), full((1, D_MODEL)), full((N_EXPERTS, D_MODEL)), full((N_EXPERTS, 1))],
        out_specs=[pl.BlockSpec((tm, D_MODEL), lambda i: (i, 0)),
                   pl.BlockSpec((tm, D_MODEL), lambda i: (i, 0))] + route_specs,
        out_shape=[jax.ShapeDtypeStruct((t, D_MODEL), F32), jax.ShapeDtypeStruct((t, D_MODEL), BF16)] + route_shapes,
        scratch_shapes=scratch,
        compiler_params=pltpu.CompilerParams(dimension_semantics=("parallel",), vmem_limit_bytes=VMEM_LIMIT),
        name="postmix",
    )(o, z, hist, x, g1, sh2, sc2, wout_b, pool_w, pool_scale, g_post, g_pre, rwt, rb_col)


def _swiglu(xb, wg, wu):
    return _silu(_dot(xb, wg.astype(BF16))) * _dot(xb, wu.astype(BF16))


def _moe_kernel(h_ref, wt_ref, wg_ref, wu_ref, wd_ref, sg_ref, su_ref, sd_ref, f_ref):
    e = pl.program_id(1)
    xb = h_ref[...]

    @pl.when(e == 0)
    def _():
        f_ref[...] = _dot(_swiglu(xb, sg_ref[...], su_ref[...]).astype(BF16), sd_ref[...].astype(BF16))

    wt = wt_ref[...]
    lane = lax.broadcasted_iota(jnp.int32, wt.shape, 1)
    wcol = jnp.sum(jnp.where(lane == e, wt, 0.0), axis=1, keepdims=True)
    hh = _swiglu(xb, wg_ref[0], wu_ref[0]) * wcol
    f_ref[...] += _dot(hh.astype(BF16), wd_ref[0].astype(BF16))


def _moe(h2b, wt, wg, wu, wd, sg, su, sd, *, tm):
    t = h2b.shape[0]
    full = lambda shape: pl.BlockSpec(shape, lambda i, e: (0,) * len(shape))
    return pl.pallas_call(
        _moe_kernel,
        grid=(t // tm, N_EXPERTS),
        in_specs=[pl.BlockSpec((tm, D_MODEL), lambda i, e: (i, 0)),
                  pl.BlockSpec((tm, N_EXPERTS), lambda i, e: (i, 0)),
                  pl.BlockSpec((1, D_MODEL, EXPERT_DIM), lambda i, e: (e, 0, 0)),
                  pl.BlockSpec((1, D_MODEL, EXPERT_DIM), lambda i, e: (e, 0, 0)),
                  pl.BlockSpec((1, EXPERT_DIM, D_MODEL), lambda i, e: (e, 0, 0)),
                  full(sg.shape), full(su.shape), full(sd.shape)],
        out_specs=pl.BlockSpec((tm, D_MODEL), lambda i, e: (i, 0)),
        out_shape=jax.ShapeDtypeStruct((t, D_MODEL), F32),
        compiler_params=pltpu.CompilerParams(dimension_semantics=("parallel", "arbitrary"),
                                             vmem_limit_bytes=VMEM_LIMIT),
        name="moe_experts",
    )(h2b, wt, wg, wu, wd, sg, su, sd)


def _dispatch_tables(idx_t, w_t, n_ranges, rtok, bm):
    k, _ = idx_t.shape
    per = k * rtok
    e = idx_t.reshape(k, n_ranges, rtok).transpose(1, 0, 2).reshape(n_ranges, per)
    w = w_t.reshape(k, n_ranges, rtok).transpose(1, 0, 2).reshape(n_ranges, per)
    order = jnp.argsort(e, axis=1)
    e_s = jnp.take_along_axis(e, order, axis=1)
    tok_s = (order % rtok).astype(jnp.int32)
    w_s = jnp.take_along_axis(w, order, axis=1)
    experts = jnp.arange(N_EXPERTS, dtype=jnp.int32)
    start = jax.vmap(lambda row: jnp.searchsorted(row, experts, side="left"))(e_s).astype(jnp.int32)
    end = jax.vmap(lambda row: jnp.searchsorted(row, experts, side="right"))(e_s).astype(jnp.int32)
    counts = end - start
    cap = rtok + 3 * bm
    q = jnp.arange(cap, dtype=jnp.int32)
    p = q - bm
    src = jnp.clip(start[:, :, None] + p, 0, per - 1)
    valid = jnp.logical_and(p >= 0, p < counts[:, :, None])
    tok = jnp.take_along_axis(tok_s[:, None, :], src, axis=2)
    src8 = jnp.where(valid, tok, 0) * 8
    dst8 = jnp.where(valid, tok, rtok + q % bm) * 8
    wl = jnp.where(valid, jnp.take_along_axis(w_s[:, None, :], src, axis=2), 0.0)
    g = n_ranges * N_EXPERTS
    return counts.reshape(g), src8.reshape(g, 1, cap), dst8.reshape(g, 1, cap), wl.reshape(g, 1, cap)


def _moe_sorted_kernel(cnt_ref, src_ref, tok_ref, w_ref, h_ref, wg_ref, wu_ref, wd_ref, acc_ref,
                       wg_b, wu_b, wd_b, xg, ys, *, bm):
    r = pl.program_id(0)
    e = pl.program_id(1)
    n_chunks = D_MODEL // LANES

    @pl.when(e == 0)
    def _():
        acc_ref[...] = jnp.zeros(acc_ref.shape, F32)

    wg_b[...] = wg_ref[0].astype(BF16)
    wu_b[...] = wu_ref[0].astype(BF16)
    wd_b[...] = wd_ref[0].astype(BF16)
    n = cnt_ref[r * N_EXPERTS + e]
    n_blocks = (n + bm - 1) // bm

    def gather(q0):
        for rr in range(bm):
            t8 = pl.multiple_of(src_ref[0, 0, q0 + rr], n_chunks)
            xg[rr * n_chunks:(rr + 1) * n_chunks, :] = h_ref[0, pl.ds(t8, n_chunks), :]

    def scatter(q0):
        for b0 in range(0, bm, SCATTER_BATCH):
            rows = range(b0, b0 + SCATTER_BATCH)
            toks = [pl.multiple_of(tok_ref[0, 0, q0 + rr], n_chunks) for rr in rows]
            wts = [w_ref[0, 0, q0 + rr] for rr in rows]
            olds = [acc_ref[0, pl.ds(t8, n_chunks), :] for t8 in toks]
            news = [o + wv * ys[rr * n_chunks:(rr + 1) * n_chunks, :] for o, wv, rr in zip(olds, wts, rows)]
            for t8, nv in zip(toks, news):
                acc_ref[0, pl.ds(t8, n_chunks), :] = nv

    def block(b, carry):
        x = jnp.concatenate([xg[pl.ds(j, bm, stride=n_chunks), :] for j in range(n_chunks)], axis=1).astype(BF16)
        gather((b + 2) * bm)
        scatter(b * bm)
        hh = _silu(_dot(x, wg_b[...])) * _dot(x, wu_b[...])
        y = _dot(hh.astype(BF16), wd_b[...])
        for i in range(bm // 8):
            for j in range(n_chunks):
                ys[pl.ds(i * 8 * n_chunks + j, 8, stride=n_chunks), :] = y[8 * i:8 * i + 8, j * LANES:(j + 1) * LANES]
        return carry

    ys[...] = jnp.zeros(ys.shape, F32)
    gather(bm)
    lax.fori_loop(0, n_blocks, block, 0)
    scatter(n_blocks * bm)


def _moe_sorted(counts, src8, dst8, wl, h2r, wg, wu, wd, *, bm):
    n_ranges, rows, _ = h2r.shape
    n_chunks = D_MODEL // LANES
    rtok = rows // n_chunks
    cap = src8.shape[2]
    group = lambda r, e, c: (r * N_EXPERTS + e, 0, 0)
    grid_spec = pltpu.PrefetchScalarGridSpec(
        num_scalar_prefetch=1,
        grid=(n_ranges, N_EXPERTS),
        in_specs=[pl.BlockSpec((1, 1, cap), group, memory_space=pltpu.SMEM),
                  pl.BlockSpec((1, 1, cap), group, memory_space=pltpu.SMEM),
                  pl.BlockSpec((1, 1, cap), group, memory_space=pltpu.SMEM),
                  pl.BlockSpec((1, rows, LANES), lambda r, e, c: (r, 0, 0), pipeline_mode=pl.Buffered(1)),
                  pl.BlockSpec((1, D_MODEL, EXPERT_DIM), lambda r, e, c: (e, 0, 0)),
                  pl.BlockSpec((1, D_MODEL, EXPERT_DIM), lambda r, e, c: (e, 0, 0)),
                  pl.BlockSpec((1, EXPERT_DIM, D_MODEL), lambda r, e, c: (e, 0, 0))],
        out_specs=pl.BlockSpec((1, (rtok + bm) * n_chunks, LANES), lambda r, e, c: (r, 0, 0),
                               pipeline_mode=pl.Buffered(1)),
        scratch_shapes=[pltpu.VMEM((D_MODEL, EXPERT_DIM), BF16), pltpu.VMEM((D_MODEL, EXPERT_DIM), BF16),
                        pltpu.VMEM((EXPERT_DIM, D_MODEL), BF16),
                        pltpu.VMEM((bm * n_chunks, LANES), F32), pltpu.VMEM((bm * n_chunks, LANES), F32)])
    return pl.pallas_call(
        functools.partial(_moe_sorted_kernel, bm=bm),
        grid_spec=grid_spec,
        out_shape=jax.ShapeDtypeStruct((n_ranges, (rtok + bm) * n_chunks, LANES), F32),
        compiler_params=pltpu.CompilerParams(dimension_semantics=("arbitrary", "arbitrary"),
                                             vmem_limit_bytes=VMEM_LIMIT),
        name="moe_grouped",
    )(counts, src8, dst8, wl, h2r, wg, wu, wd)


def _final_grouped_kernel(x1_ref, fr_ref, h_ref, sg_ref, su_ref, sd_ref, g2_ref, gain_ref, y_ref):
    shared = _dot(_swiglu(h_ref[...], sg_ref[...], su_ref[...]).astype(BF16), sd_ref[...].astype(BF16))
    f = _from_token_rows(fr_ref, x1_ref.shape[0]) + shared
    y_ref[...] = x1_ref[...] + g2_ref[0] * _rmsnorm(f, gain_ref[...])


def _final_grouped(x1, fr, h2b, sg, su, sd, g2, gain, *, tm, seq_tiles):
    t = x1.shape[0]
    n_chunks = D_MODEL // LANES
    row = pl.BlockSpec((tm, D_MODEL), lambda i: (i, 0))
    full = lambda shape: pl.BlockSpec(shape, lambda i: (0,) * len(shape))
    return pl.pallas_call(
        _final_grouped_kernel,
        grid=(t // tm,),
        in_specs=[row,
                  pl.BlockSpec((1, tm * n_chunks, LANES), lambda i: (i // seq_tiles, i % seq_tiles, 0)),
                  row, full(sg.shape), full(su.shape), full(sd.shape),
                  pl.BlockSpec((1, 1, D_MODEL), lambda i: (i // seq_tiles, 0, 0)),
                  pl.BlockSpec((1, D_MODEL), lambda i: (0, 0))],
        out_specs=row,
        out_shape=jax.ShapeDtypeStruct((t, D_MODEL), F32),
        compiler_params=pltpu.CompilerParams(dimension_semantics=("parallel",), vmem_limit_bytes=VMEM_LIMIT),
        name="final_grouped",
    )(x1, fr, h2b, sg, su, sd, g2, gain)


def _final_kernel(x1_ref, f_ref, g2_ref, gain_ref, y_ref):
    y_ref[...] = x1_ref[...] + g2_ref[0] * _rmsnorm(f_ref[...], gain_ref[...])


def _final(x1, f, g2, gain, *, tm, rows_per_mod):
    t = x1.shape[0]
    mod_rows = g2.shape[1]
    row = pl.BlockSpec((tm, D_MODEL), lambda i: (i, 0))
    return pl.pallas_call(
        _final_kernel,
        grid=(t // tm,),
        in_specs=[row, row,
                  pl.BlockSpec((1, mod_rows, D_MODEL), lambda i: (i // rows_per_mod, 0, 0)),
                  pl.BlockSpec((1, D_MODEL), lambda i: (0, 0))],
        out_specs=row,
        out_shape=jax.ShapeDtypeStruct((t, D_MODEL), F32),
        compiler_params=pltpu.CompilerParams(dimension_semantics=("parallel",), vmem_limit_bytes=VMEM_LIMIT),
        name="final_residual",
    )(x1, f, g2, gain)


def _rope_tables(pos):
    half = HEAD_DIM // 2
    inv_freq = ROPE_THETA ** (-jnp.arange(half, dtype=F32) / half)
    ang = pos.astype(F32)[:, None] * inv_freq[None, :]
    cos, sin = jnp.cos(ang), jnp.sin(ang)
    reps = LANES // HEAD_DIM
    return jnp.tile(jnp.concatenate([cos, cos], axis=1), (1, reps)), jnp.tile(jnp.concatenate([-sin, sin], axis=1), (1, reps))


def kernel(x_prompt, x_sample, c_prompt, c_sample, cache_k, cache_v, state_pool, page_table, ada_w, ada_b, pre_mix_g, post_mix_g, pre_ffn_g, post_ffn_g, w_in, w_out, lambda_q1, lambda_k1, lambda_q2, lambda_k2, subln_g, pool_w, pool_scale, router_w, router_bias, expert_w_gate, expert_w_up, expert_w_down, shared_w_gate, shared_w_up, shared_w_down):
    depth = ada_w.shape[0]
    assert depth == 1, "single-layer step"
    nb, seq, _ = x_prompt.shape
    db, dec_seq, _ = x_sample.shape
    assert dec_seq == 1
    page_size = cache_k.shape[2]
    n_pages = page_table.shape[1]
    past_len = n_pages * page_size
    lam_init = 0.8 - 0.6 * math.exp(-0.3 * 0)
    l = 0

    c_all = jnp.concatenate([c_prompt, c_sample], axis=0)
    pad = (-c_all.shape[0]) % 16
    mod = _ada(jnp.pad(c_all, ((0, pad), (0, 0))), ada_w[l], ada_b[l][None, :])
    mod_p = [m.reshape(nb, 1, D_MODEL) for m in jnp.split(mod[:nb], 6, axis=-1)]
    mod_s = [m.reshape(1, db, D_MODEL) for m in jnp.split(mod[nb:nb + db], 6, axis=-1)]

    w_in_b = w_in[l].astype(BF16)
    w_out_b = w_out[l].astype(BF16)
    lam4 = jnp.stack([lambda_q1[l], lambda_k1[l], lambda_q2[l], lambda_k2[l]])
    row = lambda v: v.reshape(1, -1)
    rwt = router_w[l].T
    rb_col = router_bias[l].reshape(N_EXPERTS, 1)
    experts = (expert_w_gate[l], expert_w_up[l], expert_w_down[l], shared_w_gate[l], shared_w_up[l], shared_w_down[l])

    t_p = nb * seq
    tm = TM_TOKEN
    seq_tiles = seq // tm
    cos_p, sin_p = _rope_tables(jnp.arange(seq))
    xp = x_prompt.reshape(t_p, D_MODEL)
    k_p, v_p, z_p, qb, kb, vt = _premix(xp, mod_p[0], mod_p[1], row(pre_mix_g[l]), w_in_b, cos_p, sin_p,
                                        tm=tm, rows_per_mod=seq_tiles, seq_tiles=seq_tiles, tk=TK_ATTN)
    o_p = _attn_prompt(qb.reshape(nb, seq, ATTN_W), kb.reshape(nb, seq, ATTN_W), vt, lam4,
                       subln_g[l].reshape(V_DIM, 1), tq=TQ_ATTN, tk=TK_ATTN, hps=HEADS_PER_STEP,
                       lam_init=lam_init)
    x1_p, h2_p, idx_t, w_t, h2r = _postmix(o_p.reshape(t_p, ATTN_W), z_p, z_p, xp, mod_p[2], mod_p[3], mod_p[4],
                                           w_out_b, pool_w[l], row(pool_scale[l]), row(post_mix_g[l]),
                                           row(pre_ffn_g[l]), rwt, rb_col,
                                           tm=tm, rows_per_mod=seq_tiles, seq_tiles=seq_tiles, past_len=None)
    counts, src8, dst8, wl = _dispatch_tables(idx_t, w_t, nb, seq, MOE_BM)
    fr_p = _moe_sorted(counts, src8, dst8, wl, h2r, *experts[:3], bm=MOE_BM)
    y_p = _final_grouped(x1_p, fr_p, h2_p, *experts[3:], mod_p[5], row(post_ffn_g[l]),
                         tm=TM_FINAL, seq_tiles=seq // TM_FINAL)

    cos_s, sin_s = _rope_tables(jnp.full((db,), past_len))
    xs = x_sample.reshape(db, D_MODEL)
    k_s, v_s, z_s, qb_s, kb_s = _premix(xs, mod_s[0], mod_s[1], row(pre_mix_g[l]), w_in_b, cos_s, sin_s,
                                        tm=db, rows_per_mod=1, seq_tiles=1, tk=0)
    page_rows = page_size * N_HEADS
    k2 = cache_k[l].reshape(-1, V_DIM)
    v2 = cache_v[l].reshape(-1, V_DIM)
    o_s = _attn_decode(page_table, qb_s.reshape(db, N_HEADS, V_DIM), kb_s.reshape(db, N_HEADS, V_DIM),
                       v_s.reshape(db, N_HEADS, V_DIM), lam4, row(subln_g[l]), k2, v2,
                       page_rows=page_rows, lam_init=lam_init)
    hist_s = jnp.transpose(state_pool[l], (1, 0, 2))
    x1_s, h2_s, wt_s = _postmix(o_s.reshape(db, ATTN_W), z_s, hist_s, xs, mod_s[2], mod_s[3], mod_s[4], w_out_b,
                                pool_w[l], row(pool_scale[l]), row(post_mix_g[l]), row(pre_ffn_g[l]), rwt, rb_col,
                                tm=db, rows_per_mod=1, seq_tiles=1, past_len=past_len)
    f_s = _moe(h2_s, wt_s, *experts, tm=db)
    y_s = _final(x1_s, f_s, mod_s[5], row(post_ffn_g[l]), tm=db, rows_per_mod=1)

    pool_p = z_p.reshape(nb, seq, POOL_W)[:, seq - POOL_BUF:]
    pool_s = jnp.concatenate([state_pool[l][:, 1:], z_s[:, None, :]], axis=1)
    return (y_p.reshape(nb, seq, D_MODEL), y_s.reshape(db, 1, D_MODEL),
            k_p.reshape(1, nb, seq, N_HEADS, V_DIM), v_p.reshape(1, nb, seq, N_HEADS, V_DIM), pool_p[None],
            k_s.reshape(1, db, 1, N_HEADS, V_DIM), v_s.reshape(1, db, 1, N_HEADS, V_DIM), pool_s[None])
```

```python
import functools
import math

import jax
import jax.numpy as jnp
from jax import lax
from jax.experimental import pallas as pl
from jax.experimental.pallas import tpu as pltpu

F32 = jnp.float32
BF16 = jnp.bfloat16

D_MODEL = 1024
ATTN_W = 512
POOL_W = 512
HEAD_DIM = 64
N_HEADS = 4
V_DIM = 2 * HEAD_DIM
IN_COLS = 3 * ATTN_W + POOL_W
POOL_WINDOWS = (2, 4, 8, 16)
POOL_GW = 128
POOL_BUF = 15
POOL_HALO = 16
ROPE_THETA = 10000.0
N_EXPERTS = 64
TOP_K = 8
N_EXPERT_GROUPS = 8
GROUP_SIZE = N_EXPERTS // N_EXPERT_GROUPS
TOPK_GROUPS = 4
EXPERT_DIM = 256
ROUTED_SCALE = 2.5
EPS = 1e-6
LANES = 128
NEG_INF = float("-inf")
VMEM_LIMIT = 56 * 1024 * 1024

TM_TOKEN = 512
TQ_ATTN = 256
TK_ATTN = 256
HEADS_PER_STEP = 4
MOE_BM = 128
SCATTER_BATCH = 8
TM_FINAL = 512
DEC_CHUNK_PAGES = 16


def _silu(x):
    return x * jax.nn.sigmoid(x)


def _rmsnorm(x, g):
    ms = jnp.mean(x * x, axis=-1, keepdims=True)
    return x * lax.rsqrt(ms + EPS) * g


def _dot(a, b):
    return jnp.dot(a, b, preferred_element_type=F32)


def _dot_nt(a, b):
    return lax.dot_general(a, b, (((1,), (1,)), ((), ())), preferred_element_type=F32)


def _ada_kernel(c_ref, w_ref, b_ref, o_ref):
    s = _silu(c_ref[...])
    o_ref[...] = _dot(s.astype(BF16), w_ref[...].astype(BF16)) + b_ref[...]


def _ada(c_pad, w, b):
    rows, n = c_pad.shape[0], w.shape[1]
    tn = 1536
    return pl.pallas_call(
        _ada_kernel,
        grid=(n // tn,),
        in_specs=[pl.BlockSpec((rows, D_MODEL), lambda j: (0, 0)),
                  pl.BlockSpec((D_MODEL, tn), lambda j: (0, j)),
                  pl.BlockSpec((1, tn), lambda j: (0, j))],
        out_specs=pl.BlockSpec((rows, tn), lambda j: (0, j)),
        out_shape=jax.ShapeDtypeStruct((rows, n), F32),
        compiler_params=pltpu.CompilerParams(dimension_semantics=("arbitrary",), vmem_limit_bytes=VMEM_LIMIT),
        name="ada_mod",
    )(c_pad, w, b)


def _rope(x, cos, sin_signed):
    outs = []
    for c in range(x.shape[1] // LANES):
        blk = x[:, c * LANES:(c + 1) * LANES]
        lane = lax.broadcasted_iota(jnp.int32, blk.shape, 1)
        partner = jnp.where((lane % HEAD_DIM) < HEAD_DIM // 2,
                            pltpu.roll(blk, LANES - HEAD_DIM // 2, 1),
                            pltpu.roll(blk, HEAD_DIM // 2, 1))
        outs.append(blk * cos + partner * sin_signed)
    return jnp.concatenate(outs, axis=1)


def _premix_kernel(x_ref, sh_ref, sc_ref, g_ref, w_ref, cos_ref, sin_ref,
                   k_ref, v_ref, z_ref, qb_ref, kb_ref, *rest, n_kv):
    h = _rmsnorm(x_ref[...], g_ref[...]) * (1.0 + sc_ref[0]) + sh_ref[0]
    proj = _dot(h.astype(BF16), w_ref[...])
    cos, sin = cos_ref[...], sin_ref[...]
    q = _rope(proj[:, :ATTN_W], cos, sin)
    k = _rope(proj[:, ATTN_W:2 * ATTN_W], cos, sin)
    v = proj[:, 2 * ATTN_W:3 * ATTN_W]
    for out_ref, val in ((k_ref, k), (v_ref, v)):
        for i in range(val.shape[0] // 8):
            for hh in range(N_HEADS):
                out_ref[pl.ds(i * 8 * N_HEADS + hh, 8, stride=N_HEADS), :] = val[8 * i:8 * i + 8, hh * V_DIM:(hh + 1) * V_DIM]
    z_ref[...] = proj[:, 3 * ATTN_W:]
    qb_ref[...] = (q * (HEAD_DIM ** -0.5)).astype(BF16)
    kb_ref[...] = k.astype(BF16)
    if n_kv:
        vt_ref = rest[0]
        vt = v.T.astype(BF16)
        tk = vt.shape[1] // n_kv
        for hh in range(N_HEADS):
            for c in range(n_kv):
                vt_ref[0, hh, c] = vt[hh * V_DIM:(hh + 1) * V_DIM, c * tk:(c + 1) * tk]


def _premix(x, shift, scale, gain, w_in_b, cos, sin, *, tm, rows_per_mod, seq_tiles, tk):
    t = x.shape[0]
    n_tiles = t // tm
    mod_rows = shift.shape[1]
    mod_idx = (lambda i: (i // rows_per_mod, 0, 0))
    n_kv = tm // tk if tk else 0
    out_shape = [jax.ShapeDtypeStruct((t * N_HEADS, V_DIM), F32), jax.ShapeDtypeStruct((t * N_HEADS, V_DIM), F32),
                 jax.ShapeDtypeStruct((t, POOL_W), F32), jax.ShapeDtypeStruct((t, ATTN_W), BF16),
                 jax.ShapeDtypeStruct((t, ATTN_W), BF16)]
    row_spec = pl.BlockSpec((tm, ATTN_W), lambda i: (i, 0))
    head_rows = pl.BlockSpec((tm * N_HEADS, V_DIM), lambda i: (i, 0))
    out_specs = [head_rows, head_rows, row_spec, row_spec, row_spec]
    if n_kv:
        nb = n_tiles // seq_tiles
        out_shape.append(jax.ShapeDtypeStruct((nb, N_HEADS, seq_tiles * n_kv, V_DIM, tk), BF16))
        out_specs.append(pl.BlockSpec((1, N_HEADS, n_kv, V_DIM, tk),
                                      lambda i: (i // seq_tiles, 0, i % seq_tiles, 0, 0)))
    return pl.pallas_call(
        functools.partial(_premix_kernel, n_kv=n_kv),
        grid=(n_tiles,),
        in_specs=[pl.BlockSpec((tm, D_MODEL), lambda i: (i, 0)),
                  pl.BlockSpec((1, mod_rows, D_MODEL), mod_idx),
                  pl.BlockSpec((1, mod_rows, D_MODEL), mod_idx),
                  pl.BlockSpec((1, D_MODEL), lambda i: (0, 0)),
                  pl.BlockSpec((D_MODEL, IN_COLS), lambda i: (0, 0)),
                  pl.BlockSpec((tm, LANES), lambda i: (i % seq_tiles, 0)),
                  pl.BlockSpec((tm, LANES), lambda i: (i % seq_tiles, 0))],
        out_specs=out_specs,
        out_shape=out_shape,
        compiler_params=pltpu.CompilerParams(dimension_semantics=("parallel",), vmem_limit_bytes=VMEM_LIMIT),
        name="premix",
    )(x, shift, scale, gain, w_in_b, cos, sin)


def _diff_lambda(lam_ref, lam_init):
    lam = lam_ref[...]
    s1 = jnp.sum(lam[0:1] * lam[1:2], axis=1, keepdims=True)
    s2 = jnp.sum(lam[2:3] * lam[3:4], axis=1, keepdims=True)
    return jnp.exp(s1) - jnp.exp(s2) + lam_init


def _split_maps(q):
    lane = lax.broadcasted_iota(jnp.int32, q.shape, 1)
    zero = jnp.zeros_like(q)
    return jnp.concatenate([jnp.where(lane < HEAD_DIM, q, zero), jnp.where(lane >= HEAD_DIM, q, zero)], axis=0)


def _attn_kernel(q_ref, k_ref, vt_ref, lam_ref, g_ref, o_ref, *, tq, tk, hps, lam_init):
    i = pl.program_id(2)
    r = tq // tk
    heads = range(hps)
    cols = [slice(hh * V_DIM, (hh + 1) * V_DIM) for hh in heads]
    qpads = [_split_maps(q_ref[0, :, cols[hh]]) for hh in heads]

    def step(j, carry, masked):
        rows = pl.ds(pl.multiple_of(j * tk, tk), tk)
        ss = [_dot_nt(k_ref[0, rows, cols[hh]], qpads[hh]) for hh in heads]
        ps, stats = [], []
        for hh in heads:
            m, l, _ = carry[hh]
            s = ss[hh]
            if masked:
                kpos = j * tk + lax.broadcasted_iota(jnp.int32, s.shape, 0)
                qpos = i * tq + lax.broadcasted_iota(jnp.int32, s.shape, 1) % tq
                s = jnp.where(kpos <= qpos, s, NEG_INF)
            m_new = jnp.maximum(m, jnp.max(s, axis=0, keepdims=True))
            alpha = jnp.exp(m - m_new)
            p = jnp.exp(s - m_new)
            stats.append((m_new, alpha * l + jnp.sum(p, axis=0, keepdims=True), alpha))
            ps.append(p.astype(BF16))
        out = []
        for hh in heads:
            m_new, l_new, alpha = stats[hh]
            acc = alpha * carry[hh][2] + _dot(vt_ref[0, hh, j], ps[hh])
            out.append((m_new, l_new, acc))
        return tuple(out)

    init = tuple((jnp.full((1, 2 * tq), -1e30, F32), jnp.zeros((1, 2 * tq), F32),
                  jnp.zeros((V_DIM, 2 * tq), F32)) for _ in heads)
    carry = lax.fori_loop(0, i * r, lambda j, c: step(j, c, False), init)
    for jj in range(r):
        carry = step(i * r + jj, carry, True)
    lam = _diff_lambda(lam_ref, lam_init)
    for hh in heads:
        _, l, acc = carry[hh]
        o = acc / l
        od = o[:, :tq] - lam * o[:, tq:]
        ms = jnp.mean(od * od, axis=0, keepdims=True)
        on = od * lax.rsqrt(ms + EPS) * g_ref[...] * (1.0 - lam_init)
        o_ref[0, :, cols[hh]] = on.T.astype(BF16)


def _attn_prompt(qb, kb, vt, lam4, subln_col, *, tq, tk, hps, lam_init):
    b, s, _ = qb.shape
    nq, nkv = s // tq, s // tk
    w = hps * V_DIM
    return pl.pallas_call(
        functools.partial(_attn_kernel, tq=tq, tk=tk, hps=hps, lam_init=lam_init),
        grid=(b, N_HEADS // hps, nq),
        in_specs=[pl.BlockSpec((1, tq, w), lambda bb, h, i: (bb, i, h)),
                  pl.BlockSpec((1, s, w), lambda bb, h, i: (bb, 0, h)),
                  pl.BlockSpec((1, hps, nkv, V_DIM, tk), lambda bb, h, i: (bb, h, 0, 0, 0)),
                  pl.BlockSpec((4, HEAD_DIM), lambda bb, h, i: (0, 0)),
                  pl.BlockSpec((V_DIM, 1), lambda bb, h, i: (0, 0))],
        out_specs=pl.BlockSpec((1, tq, w), lambda bb, h, i: (bb, i, h)),
        out_shape=jax.ShapeDtypeStruct((b, s, ATTN_W), BF16),
        compiler_params=pltpu.CompilerParams(dimension_semantics=("parallel", "parallel", "arbitrary"),
                                             vmem_limit_bytes=VMEM_LIMIT),
        name="attn_prompt",
    )(qb, kb, vt, lam4, subln_col)


def _decode_kernel(pt_ref, q_ref, kn_ref, vn_ref, lam_ref, g_ref, k_hbm, v_hbm, o_ref,
                   kbuf, vbuf, sem, m_sc, l_sc, acc_sc, *, n_chunks, chunk_pages, page_rows, lam_init):
    b = pl.program_id(0)
    c = pl.program_id(1)
    nb = pl.num_programs(0)
    step = b * n_chunks + c
    slot = step % 2

    def chunk_copies(bb, cc, sl):
        copies = []
        for p in range(chunk_pages):
            phys = pt_ref[bb * (n_chunks * chunk_pages) + cc * chunk_pages + p]
            src = pl.ds(pl.multiple_of(phys * page_rows, page_rows), page_rows)
            dst = pl.ds(p * page_rows, page_rows)
            copies.append(pltpu.make_async_copy(k_hbm.at[src, :], kbuf.at[sl, dst, :], sem.at[0, sl]))
            copies.append(pltpu.make_async_copy(v_hbm.at[src, :], vbuf.at[sl, dst, :], sem.at[1, sl]))
        return copies

    @pl.when(step == 0)
    def _():
        for cp in chunk_copies(b, c, slot):
            cp.start()

    @pl.when(step + 1 < nb * n_chunks)
    def _():
        nxt = step + 1
        for cp in chunk_copies(nxt // n_chunks, nxt % n_chunks, 1 - slot):
            cp.start()

    @pl.when(c == 0)
    def _():
        m_sc[...] = jnp.full(m_sc.shape, -1e30, F32)
        l_sc[...] = jnp.zeros(l_sc.shape, F32)
        acc_sc[...] = jnp.zeros(acc_sc.shape, F32)

    qblk = _split_maps(q_ref[0])
    for cp in chunk_copies(b, c, slot):
        cp.wait()
    kc = kbuf[slot].astype(BF16)
    vc = vbuf[slot].astype(BF16)
    s = _dot_nt(qblk, kc)
    row_head = lax.broadcasted_iota(jnp.int32, s.shape, 0) % N_HEADS
    col_head = lax.broadcasted_iota(jnp.int32, s.shape, 1) % N_HEADS
    s = jnp.where(row_head == col_head, s, NEG_INF)
    m_old = m_sc[...]
    m_new = jnp.maximum(m_old, jnp.max(s, axis=1, keepdims=True))
    alpha = jnp.exp(m_old - m_new)
    p = jnp.exp(s - m_new)
    l_new = alpha * l_sc[...] + jnp.sum(p, axis=1, keepdims=True)
    acc_new = alpha * acc_sc[...] + _dot(p.astype(BF16), vc)
    m_sc[...] = m_new
    l_sc[...] = l_new
    acc_sc[...] = acc_new

    @pl.when(c == n_chunks - 1)
    def _():
        kn = jnp.concatenate([kn_ref[0], kn_ref[0]], axis=0).astype(F32)
        vn = jnp.concatenate([vn_ref[0], vn_ref[0]], axis=0).astype(F32)
        s_n = jnp.sum(qblk.astype(F32) * kn, axis=1, keepdims=True)
        m_f = jnp.maximum(m_new, s_n)
        a_f = jnp.exp(m_new - m_f)
        p_n = jnp.exp(s_n - m_f)
        l_f = a_f * l_new + p_n
        o = (a_f * acc_new + p_n * vn) / l_f
        od = o[:N_HEADS] - _diff_lambda(lam_ref, lam_init) * o[N_HEADS:]
        o_ref[0] = (_rmsnorm(od, g_ref[...]) * (1.0 - lam_init)).astype(BF16)


def _attn_decode(page_table, q3, kn3, vn3, lam4, subln_row, k2, v2, *, page_rows, lam_init):
    nb, n_pages = page_table.shape
    chunk_pages = DEC_CHUNK_PAGES
    n_chunks = n_pages // chunk_pages
    rows = chunk_pages * page_rows
    head_spec = pl.BlockSpec((1, N_HEADS, V_DIM), lambda b, c, pt: (b, 0, 0))
    grid_spec = pltpu.PrefetchScalarGridSpec(
        num_scalar_prefetch=1,
        grid=(nb, n_chunks),
        in_specs=[head_spec, head_spec, head_spec,
                  pl.BlockSpec((4, HEAD_DIM), lambda b, c, pt: (0, 0)),
                  pl.BlockSpec((1, V_DIM), lambda b, c, pt: (0, 0)),
                  pl.BlockSpec(memory_space=pl.ANY),
                  pl.BlockSpec(memory_space=pl.ANY)],
        out_specs=head_spec,
        scratch_shapes=[pltpu.VMEM((2, rows, V_DIM), F32), pltpu.VMEM((2, rows, V_DIM), F32),
                        pltpu.SemaphoreType.DMA((2, 2)),
                        pltpu.VMEM((2 * N_HEADS, 1), F32), pltpu.VMEM((2 * N_HEADS, 1), F32),
                        pltpu.VMEM((2 * N_HEADS, V_DIM), F32)])
    return pl.pallas_call(
        functools.partial(_decode_kernel, n_chunks=n_chunks, chunk_pages=chunk_pages,
                          page_rows=page_rows, lam_init=lam_init),
        grid_spec=grid_spec,
        out_shape=jax.ShapeDtypeStruct((nb, N_HEADS, V_DIM), BF16),
        compiler_params=pltpu.CompilerParams(dimension_semantics=("arbitrary", "arbitrary"),
                                             vmem_limit_bytes=VMEM_LIMIT),
        name="attn_decode",
    )(page_table.reshape(-1), q3, kn3, vn3, lam4, subln_row, k2, v2)


def _pool_project(d_groups, pw_ref, ps_ref):
    ys = [_dot(d.astype(BF16), pw_ref[g].astype(BF16)) for g, d in enumerate(d_groups)]
    return jnp.concatenate(ys, axis=1) * ps_ref[...]


def _argmax_first(x, iota, n):
    mx = jnp.max(x, axis=0, keepdims=True)
    ix = jnp.min(jnp.where(x == mx, iota, float(n)), axis=0, keepdims=True)
    return mx, ix


def _row_iota(shape):
    return lax.broadcasted_iota(jnp.int32, shape, 0).astype(F32)


def _route(h2, rwt_ref, rb_ref):
    rw = rwt_ref[...]
    rw_hi = rw.astype(BF16)
    rw_lo = (rw - rw_hi.astype(F32)).astype(BF16)
    h_hi = h2.astype(BF16)
    h_lo = (h2 - h_hi.astype(F32)).astype(BF16)
    logits = _dot_nt(rw_hi, h_hi) + (_dot_nt(rw_hi, h_lo) + _dot_nt(rw_lo, h_hi))
    scores = jax.nn.sigmoid(logits)
    sel = scores + rb_ref[...]
    tm = sel.shape[1]
    iota_g = _row_iota((GROUP_SIZE, tm))
    group_rows = []
    for g in range(N_EXPERT_GROUPS):
        xg = sel[g * GROUP_SIZE:(g + 1) * GROUP_SIZE]
        m1, i1 = _argmax_first(xg, iota_g, GROUP_SIZE)
        m2 = jnp.max(jnp.where(iota_g == i1, NEG_INF, xg), axis=0, keepdims=True)
        group_rows.append(m1 + m2)
    gscore = jnp.concatenate(group_rows, axis=0)
    iota_gr = _row_iota(gscore.shape)
    gmask = jnp.zeros(gscore.shape, F32)
    for _ in range(TOPK_GROUPS):
        _, ig = _argmax_first(gscore, iota_gr, N_EXPERT_GROUPS)
        hit = iota_gr == ig
        gmask = jnp.where(hit, 1.0, gmask)
        gscore = jnp.where(hit, NEG_INF, gscore)
    masked = jnp.concatenate(
        [jnp.where(gmask[g:g + 1] > 0.5, sel[g * GROUP_SIZE:(g + 1) * GROUP_SIZE], NEG_INF)
         for g in range(N_EXPERT_GROUPS)], axis=0)
    iota_e = _row_iota(masked.shape)
    w = jnp.zeros(masked.shape, F32)
    idx_rows, score_rows = [], []
    for _ in range(TOP_K):
        _, ie = _argmax_first(masked, iota_e, N_EXPERTS)
        hit = iota_e == ie
        w = jnp.where(hit, scores, w)
        masked = jnp.where(hit, NEG_INF, masked)
        idx_rows.append(ie)
        score_rows.append(jnp.sum(jnp.where(hit, scores, 0.0), axis=0, keepdims=True))
    wsum = jnp.sum(w, axis=0, keepdims=True)
    dense = w / wsum * ROUTED_SCALE
    idx_t = jnp.concatenate(idx_rows, axis=0).astype(jnp.int32)
    w_t = jnp.concatenate(score_rows, axis=0) / wsum * ROUTED_SCALE
    return dense, idx_t, w_t


def _to_token_rows(x, out_ref):
    n_chunks = x.shape[1] // LANES
    for i in range(x.shape[0] // 8):
        for j in range(n_chunks):
            out_ref[0, pl.ds(i * 8 * n_chunks + j, 8, stride=n_chunks), :] = x[8 * i:8 * i + 8, j * LANES:(j + 1) * LANES]


def _from_token_rows(in_ref, tm):
    n_chunks = D_MODEL // LANES
    return jnp.concatenate([in_ref[0, pl.ds(j, tm, stride=n_chunks), :] for j in range(n_chunks)], axis=1)


def _mix_tail(o_b, pz, x, g1, sh2, sc2, wout_ref, gpost_ref, gpre_ref, rwt_ref, rb_ref, x1_ref, h2_ref, route_refs):
    m = _dot(o_b, wout_ref[:ATTN_W, :]) + _dot(pz.astype(BF16), wout_ref[ATTN_W:, :])
    x1 = x + g1 * _rmsnorm(m, gpost_ref[...])
    h2 = _rmsnorm(x1, gpre_ref[...]) * (1.0 + sc2) + sh2
    x1_ref[...] = x1
    h2_ref[...] = h2.astype(BF16)
    dense, idx_t, w_t = _route(h2, rwt_ref, rb_ref)
    if len(route_refs) == 1:
        route_refs[0][...] = dense.T
    else:
        idx_ref, w_ref, h2r_ref = route_refs
        idx_ref[...] = idx_t
        w_ref[...] = w_t
        _to_token_rows(h2, h2r_ref)


def _postmix_prompt_kernel(o_ref, z_ref, zprev_ref, x_ref, g1_ref, sh2_ref, sc2_ref,
                           wout_ref, pw_ref, ps_ref, gpost_ref, gpre_ref, rwt_ref, rb_ref,
                           x1_ref, h2_ref, idx_ref, w_ref, h2r_ref, zbuf, *, seq_tiles):
    tm = z_ref.shape[0]
    it = pl.program_id(0) % seq_tiles
    zt = z_ref[...]
    zbuf[0:POOL_HALO, :] = jnp.where(it == 0, 0.0, zprev_ref[...])
    zbuf[POOL_HALO:, :] = zt
    pos = it * tm + lax.broadcasted_iota(jnp.int32, (tm, 1), 0)
    d_groups = []
    for g, w in enumerate(POOL_WINDOWS):
        cols = slice(g * POOL_GW, (g + 1) * POOL_GW)
        win = zt[:, cols]
        for s in range(1, w):
            win = win + zbuf[POOL_HALO - s:POOL_HALO - s + tm, cols]
        cnt = jnp.minimum(pos + 1, w).astype(F32)
        d_groups.append(win / cnt - zt[:, cols])
    pz = _pool_project(d_groups, pw_ref, ps_ref)
    _mix_tail(o_ref[...], pz, x_ref[...], g1_ref[0], sh2_ref[0], sc2_ref[0], wout_ref, gpost_ref, gpre_ref,
              rwt_ref, rb_ref, x1_ref, h2_ref, (idx_ref, w_ref, h2r_ref))


def _postmix_sample_kernel(o_ref, z_ref, st_ref, x_ref, g1_ref, sh2_ref, sc2_ref,
                           wout_ref, pw_ref, ps_ref, gpost_ref, gpre_ref, rwt_ref, rb_ref,
                           x1_ref, h2_ref, wt_ref, *, past_len):
    zt = z_ref[...]
    d_groups = []
    for g, w in enumerate(POOL_WINDOWS):
        cols = slice(g * POOL_GW, (g + 1) * POOL_GW)
        win = zt[:, cols]
        for s in range(1, w):
            win = win + st_ref[POOL_BUF - s][:, cols]
        d_groups.append(win / float(min(past_len + 1, w)) - zt[:, cols])
    pz = _pool_project(d_groups, pw_ref, ps_ref)
    _mix_tail(o_ref[...], pz, x_ref[...], g1_ref[0], sh2_ref[0], sc2_ref[0], wout_ref, gpost_ref, gpre_ref,
              rwt_ref, rb_ref, x1_ref, h2_ref, (wt_ref,))


def _postmix(o, z, hist, x, g1, sh2, sc2, wout_b, pool_w, pool_scale, g_post, g_pre, rwt, rb_col,
             *, tm, rows_per_mod, seq_tiles, past_len):
    t = x.shape[0]
    n_tiles = t // tm
    mod_rows = g1.shape[1]
    mod_spec = pl.BlockSpec((1, mod_rows, D_MODEL), lambda i: (i // rows_per_mod, 0, 0))
    full = lambda shape: pl.BlockSpec(shape, lambda i: (0,) * len(shape))
    if past_len is None:
        body = functools.partial(_postmix_prompt_kernel, seq_tiles=seq_tiles)
        halo_blocks = tm // POOL_HALO
        hist_spec = pl.BlockSpec((POOL_HALO, POOL_W), lambda i: (jnp.maximum(i * halo_blocks - 1, 0), 0))
        scratch = [pltpu.VMEM((POOL_HALO + tm, POOL_W), F32)]
        n_seq = n_tiles // seq_tiles
        rows8 = D_MODEL // LANES
        route_specs = [pl.BlockSpec((TOP_K, tm), lambda i: (0, i)), pl.BlockSpec((TOP_K, tm), lambda i: (0, i)),
                       pl.BlockSpec((1, tm * rows8, LANES), lambda i: (i // seq_tiles, i % seq_tiles, 0))]
        route_shapes = [jax.ShapeDtypeStruct((TOP_K, t), jnp.int32), jax.ShapeDtypeStruct((TOP_K, t), F32),
                        jax.ShapeDtypeStruct((n_seq, seq_tiles * tm * rows8, LANES), F32)]
    else:
        body = functools.partial(_postmix_sample_kernel, past_len=past_len)
        hist_spec = full(hist.shape)
        scratch = []
        route_specs = [pl.BlockSpec((tm, N_EXPERTS), lambda i: (i, 0))]
        route_shapes = [jax.ShapeDtypeStruct((t, N_EXPERTS), F32)]
    return pl.pallas_call(
        body,
        grid=(n_tiles,),
        in_specs=[pl.BlockSpec((tm, ATTN_W), lambda i: (i, 0)),
                  pl.BlockSpec((tm, POOL_W), lambda i: (i, 0)),
                  hist_spec,
                  pl.BlockSpec((tm, D_MODEL), lambda i: (i, 0)),
                  mod_spec, mod_spec, mod_spec,
                  full((D_MODEL, D_MODEL)), full(pool_w.shape), full((1, POOL_W)),
                  full((1, D_MODEL)), full((1, D_MODEL)), full((N_EXPERTS, D_MODEL)), full((N_EXPERTS, 1))],
        out_specs=[pl.BlockSpec((tm, D_MODEL), lambda i: (i, 0)),
                   pl.BlockSpec((tm, D_MODEL), lambda i: (i, 0))] + route_specs,
        out_shape=[jax.ShapeDtypeStruct((t, D_MODEL), F32), jax.ShapeDtypeStruct((t, D_MODEL), BF16)] + route_shapes,
        scratch_shapes=scratch,
        compiler_params=pltpu.CompilerParams(dimension_semantics=("parallel",), vmem_limit_bytes=VMEM_LIMIT),
        name="postmix",
    )(o, z, hist, x, g1, sh2, sc2, wout_b, pool_w, pool_scale, g_post, g_pre, rwt, rb_col)


def _swiglu(xb, wg, wu):
    return _silu(_dot(xb, wg.astype(BF16))) * _dot(xb, wu.astype(BF16))


def _moe_kernel(h_ref, wt_ref, wg_ref, wu_ref, wd_ref, sg_ref, su_ref, sd_ref, f_ref):
    e = pl.program_id(1)
    xb = h_ref[...]

    @pl.when(e == 0)
    def _():
        f_ref[...] = _dot(_swiglu(xb, sg_ref[...], su_ref[...]).astype(BF16), sd_ref[...].astype(BF16))

    wt = wt_ref[...]
    lane = lax.broadcasted_iota(jnp.int32, wt.shape, 1)
    wcol = jnp.sum(jnp.where(lane == e, wt, 0.0), axis=1, keepdims=True)
    hh = _swiglu(xb, wg_ref[0], wu_ref[0]) * wcol
    f_ref[...] += _dot(hh.astype(BF16), wd_ref[0].astype(BF16))


def _moe(h2b, wt, wg, wu, wd, sg, su, sd, *, tm):
    t = h2b.shape[0]
    full = lambda shape: pl.BlockSpec(shape, lambda i, e: (0,) * len(shape))
    return pl.pallas_call(
        _moe_kernel,
        grid=(t // tm, N_EXPERTS),
        in_specs=[pl.BlockSpec((tm, D_MODEL), lambda i, e: (i, 0)),
                  pl.BlockSpec((tm, N_EXPERTS), lambda i, e: (i, 0)),
                  pl.BlockSpec((1, D_MODEL, EXPERT_DIM), lambda i, e: (e, 0, 0)),
                  pl.BlockSpec((1, D_MODEL, EXPERT_DIM), lambda i, e: (e, 0, 0)),
                  pl.BlockSpec((1, EXPERT_DIM, D_MODEL), lambda i, e: (e, 0, 0)),
                  full(sg.shape), full(su.shape), full(sd.shape)],
        out_specs=pl.BlockSpec((tm, D_MODEL), lambda i, e: (i, 0)),
        out_shape=jax.ShapeDtypeStruct((t, D_MODEL), F32),
        compiler_params=pltpu.CompilerParams(dimension_semantics=("parallel", "arbitrary"),
                                             vmem_limit_bytes=VMEM_LIMIT),
        name="moe_experts",
    )(h2b, wt, wg, wu, wd, sg, su, sd)


def _dispatch_tables(idx_t, w_t, n_ranges, rtok, bm):
    k, _ = idx_t.shape
    per = k * rtok
    e = idx_t.reshape(k, n_ranges, rtok).transpose(1, 0, 2).reshape(n_ranges, per)
    w = w_t.reshape(k, n_ranges, rtok).transpose(1, 0, 2).reshape(n_ranges, per)
    keys = jnp.sort(e * per + jnp.arange(per, dtype=jnp.int32), axis=1)
    bounds = jnp.arange(N_EXPERTS + 1, dtype=jnp.int32) * per
    below = jnp.sum((keys[:, :, None] < bounds).astype(jnp.int32), axis=1)
    start = below[:, :N_EXPERTS]
    n_sub = (below[:, 1:] - start + bm - 1) // bm
    first_sub = jnp.cumsum(n_sub, axis=1) - n_sub
    tot_sub = per // bm + N_EXPERTS + rtok // bm + 2
    sub = jnp.arange(tot_sub, dtype=jnp.int32)
    begun = sub[None, :, None] >= first_sub[:, None, :]
    sub_start = jnp.max(jnp.where(begun, start[:, None, :], 0), axis=2)
    sub_first = jnp.max(jnp.where(begun, first_sub[:, None, :], 0), axis=2)
    sub_end = jnp.min(jnp.where(begun, per, start[:, None, :]), axis=2)
    src = (sub_start + (sub - sub_first) * bm)[:, :, None] + jnp.arange(bm, dtype=jnp.int32)
    valid = (src < sub_end[:, :, None]).reshape(n_ranges, tot_sub * bm)
    key = jnp.take_along_axis(keys, jnp.minimum(src.reshape(n_ranges, tot_sub * bm), per - 1), axis=1)
    slot = key % per
    tok = slot % rtok
    spare = rtok + jnp.arange(tot_sub * bm, dtype=jnp.int32) % bm
    src8 = jnp.where(valid, tok, 0) * 8
    dst8 = jnp.where(valid, tok, spare) * 8
    wl = jnp.where(valid, jnp.take_along_axis(w, slot, axis=1), 0.0)
    g = n_ranges * N_EXPERTS
    return first_sub.reshape(g), n_sub.reshape(g), src8[:, None, :], dst8[:, None, :], wl[:, None, :]


def _moe_sorted_kernel(first_ref, nsub_ref, src_ref, tok_ref, w_ref, h_ref, wg_ref, wu_ref, wd_ref, acc_ref,
                       wg_b, wu_b, wd_b, xg, ys, *, bm):
    r = pl.program_id(0)
    e = pl.program_id(1)
    n_chunks = D_MODEL // LANES
    n_blocks = nsub_ref[r * N_EXPERTS + e]

    @pl.when(e == 0)
    def _():
        acc_ref[...] = jnp.zeros(acc_ref.shape, F32)

    def gather(q0):
        for rr in range(bm):
            t8 = pl.multiple_of(src_ref[0, 0, q0 + rr], n_chunks)
            xg[rr * n_chunks:(rr + 1) * n_chunks, :] = h_ref[0, pl.ds(t8, n_chunks), :]

    def scatter(q0):
        for b0 in range(0, bm, SCATTER_BATCH):
            rows = range(b0, b0 + SCATTER_BATCH)
            toks = [pl.multiple_of(tok_ref[0, 0, q0 + rr], n_chunks) for rr in rows]
            wts = [w_ref[0, 0, q0 + rr] for rr in rows]
            olds = [acc_ref[0, pl.ds(t8, n_chunks), :] for t8 in toks]
            news = [o + wv * ys[rr * n_chunks:(rr + 1) * n_chunks, :] for o, wv, rr in zip(olds, wts, rows)]
            for t8, nv in zip(toks, news):
                acc_ref[0, pl.ds(t8, n_chunks), :] = nv

    def block(b, scatter_previous):
        x = jnp.concatenate([xg[pl.ds(j, bm, stride=n_chunks), :] for j in range(n_chunks)], axis=1).astype(BF16)
        gather((b + 1) * bm)
        if scatter_previous:
            scatter((b - 1) * bm)
        hh = _silu(_dot(x, wg_b[...])) * _dot(x, wu_b[...])
        y = _dot(hh.astype(BF16), wd_b[...])
        for i in range(bm // 8):
            for j in range(n_chunks):
                ys[pl.ds(i * 8 * n_chunks + j, 8, stride=n_chunks), :] = y[8 * i:8 * i + 8, j * LANES:(j + 1) * LANES]

    @pl.when(n_blocks > 0)
    def _():
        wg_b[...] = wg_ref[0].astype(BF16)
        wu_b[...] = wu_ref[0].astype(BF16)
        wd_b[...] = wd_ref[0].astype(BF16)
        gather(0)
        block(0, False)

        def body(b, carry):
            block(b, True)
            return carry

        lax.fori_loop(1, n_blocks, body, 0)
        scatter((n_blocks - 1) * bm)


def _moe_sorted(first_sub, n_sub, src8, dst8, wl, h2r, wg, wu, wd, *, bm):
    n_ranges, rows, _ = h2r.shape
    n_chunks = D_MODEL // LANES
    rtok = rows // n_chunks
    cap = rtok + 2 * bm
    one = pl.Element(1)
    window = lambda r, e, first, nsub: (r, 0, first[r * N_EXPERTS + e] * bm)
    list_spec = pl.BlockSpec((one, one, pl.Element(cap)), window, memory_space=pltpu.SMEM)
    grid_spec = pltpu.PrefetchScalarGridSpec(
        num_scalar_prefetch=2,
        grid=(n_ranges, N_EXPERTS),
        in_specs=[list_spec, list_spec, list_spec,
                  pl.BlockSpec((1, rows, LANES), lambda r, e, *_: (r, 0, 0), pipeline_mode=pl.Buffered(1)),
                  pl.BlockSpec((1, D_MODEL, EXPERT_DIM), lambda r, e, *_: (e, 0, 0)),
                  pl.BlockSpec((1, D_MODEL, EXPERT_DIM), lambda r, e, *_: (e, 0, 0)),
                  pl.BlockSpec((1, EXPERT_DIM, D_MODEL), lambda r, e, *_: (e, 0, 0))],
        out_specs=pl.BlockSpec((1, (rtok + bm) * n_chunks, LANES), lambda r, e, *_: (r, 0, 0),
                               pipeline_mode=pl.Buffered(1)),
        scratch_shapes=[pltpu.VMEM((D_MODEL, EXPERT_DIM), BF16), pltpu.VMEM((D_MODEL, EXPERT_DIM), BF16),
                        pltpu.VMEM((EXPERT_DIM, D_MODEL), BF16),
                        pltpu.VMEM((bm * n_chunks, LANES), F32), pltpu.VMEM((bm * n_chunks, LANES), F32)])
    return pl.pallas_call(
        functools.partial(_moe_sorted_kernel, bm=bm),
        grid_spec=grid_spec,
        out_shape=jax.ShapeDtypeStruct((n_ranges, (rtok + bm) * n_chunks, LANES), F32),
        compiler_params=pltpu.CompilerParams(dimension_semantics=("arbitrary", "arbitrary"),
                                             vmem_limit_bytes=VMEM_LIMIT),
        name="moe_grouped",
    )(first_sub, n_sub, src8, dst8, wl, h2r, wg, wu, wd)


def _final_grouped_kernel(x1_ref, fr_ref, h_ref, sg_ref, su_ref, sd_ref, g2_ref, gain_ref, y_ref):
    shared = _dot(_swiglu(h_ref[...], sg_ref[...], su_ref[...]).astype(BF16), sd_ref[...].astype(BF16))
    f = _from_token_rows(fr_ref, x1_ref.shape[0]) + shared
    y_ref[...] = x1_ref[...] + g2_ref[0] * _rmsnorm(f, gain_ref[...])


def _final_grouped(x1, fr, h2b, sg, su, sd, g2, gain, *, tm, seq_tiles):
    t = x1.shape[0]
    n_chunks = D_MODEL // LANES
    row = pl.BlockSpec((tm, D_MODEL), lambda i: (i, 0))
    full = lambda shape: pl.BlockSpec(shape, lambda i: (0,) * len(shape))
    return pl.pallas_call(
        _final_grouped_kernel,
        grid=(t // tm,),
        in_specs=[row,
                  pl.BlockSpec((1, tm * n_chunks, LANES), lambda i: (i // seq_tiles, i % seq_tiles, 0)),
                  row, full(sg.shape), full(su.shape), full(sd.shape),
                  pl.BlockSpec((1, 1, D_MODEL), lambda i: (i // seq_tiles, 0, 0)),
                  pl.BlockSpec((1, D_MODEL), lambda i: (0, 0))],
        out_specs=row,
        out_shape=jax.ShapeDtypeStruct((t, D_MODEL), F32),
        compiler_params=pltpu.CompilerParams(dimension_semantics=("parallel",), vmem_limit_bytes=VMEM_LIMIT),
        name="final_grouped",
    )(x1, fr, h2b, sg, su, sd, g2, gain)


def _final_kernel(x1_ref, f_ref, g2_ref, gain_ref, y_ref):
    y_ref[...] = x1_ref[...] + g2_ref[0] * _rmsnorm(f_ref[...], gain_ref[...])


def _final(x1, f, g2, gain, *, tm, rows_per_mod):
    t = x1.shape[0]
    mod_rows = g2.shape[1]
    row = pl.BlockSpec((tm, D_MODEL), lambda i: (i, 0))
    return pl.pallas_call(
        _final_kernel,
        grid=(t // tm,),
        in_specs=[row, row,
                  pl.BlockSpec((1, mod_rows, D_MODEL), lambda i: (i // rows_per_mod, 0, 0)),
                  pl.BlockSpec((1, D_MODEL), lambda i: (0, 0))],
        out_specs=row,
        out_shape=jax.ShapeDtypeStruct((t, D_MODEL), F32),
        compiler_params=pltpu.CompilerParams(dimension_semantics=("parallel",), vmem_limit_bytes=VMEM_LIMIT),
        name="final_residual",
    )(x1, f, g2, gain)


def _rope_tables(pos):
    half = HEAD_DIM // 2
    inv_freq = ROPE_THETA ** (-jnp.arange(half, dtype=F32) / half)
    ang = pos.astype(F32)[:, None] * inv_freq[None, :]
    cos, sin = jnp.cos(ang), jnp.sin(ang)
    reps = LANES // HEAD_DIM
    return jnp.tile(jnp.concatenate([cos, cos], axis=1), (1, reps)), jnp.tile(jnp.concatenate([-sin, sin], axis=1), (1, reps))


def kernel(x_prompt, x_sample, c_prompt, c_sample, cache_k, cache_v, state_pool, page_table, ada_w, ada_b, pre_mix_g, post_mix_g, pre_ffn_g, post_ffn_g, w_in, w_out, lambda_q1, lambda_k1, lambda_q2, lambda_k2, subln_g, pool_w, pool_scale, router_w, router_bias, expert_w_gate, expert_w_up, expert_w_down, shared_w_gate, shared_w_up, shared_w_down):
    depth = ada_w.shape[0]
    assert depth == 1, "single-layer step"
    nb, seq, _ = x_prompt.shape
    db, dec_seq, _ = x_sample.shape
    assert dec_seq == 1
    page_size = cache_k.shape[2]
    n_pages = page_table.shape[1]
    past_len = n_pages * page_size
    lam_init = 0.8 - 0.6 * math.exp(-0.3 * 0)
    l = 0

    c_all = jnp.concatenate([c_prompt, c_sample], axis=0)
    pad = (-c_all.shape[0]) % 16
    mod = _ada(jnp.pad(c_all, ((0, pad), (0, 0))), ada_w[l], ada_b[l][None, :])
    mod_p = [m.reshape(nb, 1, D_MODEL) for m in jnp.split(mod[:nb], 6, axis=-1)]
    mod_s = [m.reshape(1, db, D_MODEL) for m in jnp.split(mod[nb:nb + db], 6, axis=-1)]

    w_in_b = w_in[l].astype(BF16)
    w_out_b = w_out[l].astype(BF16)
    lam4 = jnp.stack([lambda_q1[l], lambda_k1[l], lambda_q2[l], lambda_k2[l]])
    row = lambda v: v.reshape(1, -1)
    rwt = router_w[l].T
    rb_col = router_bias[l].reshape(N_EXPERTS, 1)
    experts = (expert_w_gate[l], expert_w_up[l], expert_w_down[l], shared_w_gate[l], shared_w_up[l], shared_w_down[l])

    t_p = nb * seq
    tm = TM_TOKEN
    seq_tiles = seq // tm
    cos_p, sin_p = _rope_tables(jnp.arange(seq))
    xp = x_prompt.reshape(t_p, D_MODEL)
    k_p, v_p, z_p, qb, kb, vt = _premix(xp, mod_p[0], mod_p[1], row(pre_mix_g[l]), w_in_b, cos_p, sin_p,
                                        tm=tm, rows_per_mod=seq_tiles, seq_tiles=seq_tiles, tk=TK_ATTN)
    o_p = _attn_prompt(qb.reshape(nb, seq, ATTN_W), kb.reshape(nb, seq, ATTN_W), vt, lam4,
                       subln_g[l].reshape(V_DIM, 1), tq=TQ_ATTN, tk=TK_ATTN, hps=HEADS_PER_STEP,
                       lam_init=lam_init)
    x1_p, h2_p, idx_t, w_t, h2r = _postmix(o_p.reshape(t_p, ATTN_W), z_p, z_p, xp, mod_p[2], mod_p[3], mod_p[4],
                                           w_out_b, pool_w[l], row(pool_scale[l]), row(post_mix_g[l]),
                                           row(pre_ffn_g[l]), rwt, rb_col,
                                           tm=tm, rows_per_mod=seq_tiles, seq_tiles=seq_tiles, past_len=None)
    first_sub, n_sub, src8, dst8, wl = _dispatch_tables(idx_t, w_t, nb, seq, MOE_BM)
    fr_p = _moe_sorted(first_sub, n_sub, src8, dst8, wl, h2r, *experts[:3], bm=MOE_BM)
    y_p = _final_grouped(x1_p, fr_p, h2_p, *experts[3:], mod_p[5], row(post_ffn_g[l]),
                         tm=TM_FINAL, seq_tiles=seq // TM_FINAL)

    cos_s, sin_s = _rope_tables(jnp.full((db,), past_len))
    xs = x_sample.reshape(db, D_MODEL)
    k_s, v_s, z_s, qb_s, kb_s = _premix(xs, mod_s[0], mod_s[1], row(pre_mix_g[l]), w_in_b, cos_s, sin_s,
                                        tm=db, rows_per_mod=1, seq_tiles=1, tk=0)
    page_rows = page_size * N_HEADS
    k2 = cache_k[l].reshape(-1, V_DIM)
    v2 = cache_v[l].reshape(-1, V_DIM)
    o_s = _attn_decode(page_table, qb_s.reshape(db, N_HEADS, V_DIM), kb_s.reshape(db, N_HEADS, V_DIM),
                       v_s.reshape(db, N_HEADS, V_DIM), lam4, row(subln_g[l]), k2, v2,
                       page_rows=page_rows, lam_init=lam_init)
    hist_s = jnp.transpose(state_pool[l], (1, 0, 2))
    x1_s, h2_s, wt_s = _postmix(o_s.reshape(db, ATTN_W), z_s, hist_s, xs, mod_s[2], mod_s[3], mod_s[4], w_out_b,
                                pool_w[l], row(pool_scale[l]), row(post_mix_g[l]), row(pre_ffn_g[l]), rwt, rb_col,
                                tm=db, rows_per_mod=1, seq_tiles=1, past_len=past_len)
    f_s = _moe(h2_s, wt_s, *experts, tm=db)
    y_s = _final(x1_s, f_s, mod_s[5], row(post_ffn_g[l]), tm=db, rows_per_mod=1)

    pool_p = z_p.reshape(nb, seq, POOL_W)[:, seq - POOL_BUF:]
    pool_s = jnp.concatenate([state_pool[l][:, 1:], z_s[:, None, :]], axis=1)
    return (y_p.reshape(nb, seq, D_MODEL), y_s.reshape(db, 1, D_MODEL),
            k_p.reshape(1, nb, seq, N_HEADS, V_DIM), v_p.reshape(1, nb, seq, N_HEADS, V_DIM), pool_p[None],
            k_s.reshape(1, db, 1, N_HEADS, V_DIM), v_s.reshape(1, db, 1, N_HEADS, V_DIM), pool_s[None])
```

```python
import functools
import math

import jax
import jax.numpy as jnp
from jax import lax
from jax.experimental import pallas as pl
from jax.experimental.pallas import tpu as pltpu

F32 = jnp.float32
BF16 = jnp.bfloat16

D_MODEL = 1024
ATTN_W = 512
POOL_W = 512
HEAD_DIM = 64
N_HEADS = 4
V_DIM = 2 * HEAD_DIM
IN_COLS = 3 * ATTN_W + POOL_W
POOL_WINDOWS = (2, 4, 8, 16)
POOL_GW = 128
POOL_BUF = 15
POOL_HALO = 16
ROPE_THETA = 10000.0
N_EXPERTS = 64
TOP_K = 8
N_EXPERT_GROUPS = 8
GROUP_SIZE = N_EXPERTS // N_EXPERT_GROUPS
TOPK_GROUPS = 4
EXPERT_DIM = 256
ROUTED_SCALE = 2.5
EPS = 1e-6
LANES = 128
NEG_INF = float("-inf")
Q_SCALE = HEAD_DIM ** -0.5 * math.log2(math.e)
VMEM_LIMIT = 56 * 1024 * 1024

TM_TOKEN = 512
TQ_ATTN = 512
TK_ATTN = 256
HEADS_PER_STEP = 4
MOE_BM = 128
SCATTER_BATCH = 8
TM_FINAL = 512
DEC_CHUNK_PAGES = 16


def _silu(x):
    return x * jax.nn.sigmoid(x)


def _rmsnorm(x, g):
    ms = jnp.mean(x * x, axis=-1, keepdims=True)
    return x * lax.rsqrt(ms + EPS) * g


def _dot(a, b):
    return jnp.dot(a, b, preferred_element_type=F32)


def _dot_nt(a, b):
    return lax.dot_general(a, b, (((1,), (1,)), ((), ())), preferred_element_type=F32)


def _ada_kernel(c_ref, w_ref, b_ref, o_ref):
    s = _silu(c_ref[...])
    o_ref[...] = _dot(s.astype(BF16), w_ref[...].astype(BF16)) + b_ref[...]


def _ada(c_pad, w, b):
    rows, n = c_pad.shape[0], w.shape[1]
    tn = 1536
    return pl.pallas_call(
        _ada_kernel,
        grid=(n // tn,),
        in_specs=[pl.BlockSpec((rows, D_MODEL), lambda j: (0, 0)),
                  pl.BlockSpec((D_MODEL, tn), lambda j: (0, j)),
                  pl.BlockSpec((1, tn), lambda j: (0, j))],
        out_specs=pl.BlockSpec((rows, tn), lambda j: (0, j)),
        out_shape=jax.ShapeDtypeStruct((rows, n), F32),
        compiler_params=pltpu.CompilerParams(dimension_semantics=("arbitrary",), vmem_limit_bytes=VMEM_LIMIT),
        name="ada_mod",
    )(c_pad, w, b)


def _rope(x, cos, sin_signed):
    outs = []
    for c in range(x.shape[1] // LANES):
        blk = x[:, c * LANES:(c + 1) * LANES]
        lane = lax.broadcasted_iota(jnp.int32, blk.shape, 1)
        partner = jnp.where((lane % HEAD_DIM) < HEAD_DIM // 2,
                            pltpu.roll(blk, LANES - HEAD_DIM // 2, 1),
                            pltpu.roll(blk, HEAD_DIM // 2, 1))
        outs.append(blk * cos + partner * sin_signed)
    return jnp.concatenate(outs, axis=1)


def _premix_kernel(x_ref, sh_ref, sc_ref, g_ref, w_ref, cos_ref, sin_ref,
                   k_ref, v_ref, z_ref, qb_ref, kb_ref, *rest, n_kv):
    h = _rmsnorm(x_ref[...], g_ref[...]) * (1.0 + sc_ref[0]) + sh_ref[0]
    proj = _dot(h.astype(BF16), w_ref[...])
    cos, sin = cos_ref[...], sin_ref[...]
    q = _rope(proj[:, :ATTN_W], cos, sin)
    k = _rope(proj[:, ATTN_W:2 * ATTN_W], cos, sin)
    v = proj[:, 2 * ATTN_W:3 * ATTN_W]
    for out_ref, val in ((k_ref, k), (v_ref, v)):
        for i in range(val.shape[0] // 8):
            for hh in range(N_HEADS):
                out_ref[pl.ds(i * 8 * N_HEADS + hh, 8, stride=N_HEADS), :] = val[8 * i:8 * i + 8, hh * V_DIM:(hh + 1) * V_DIM]
    z_ref[...] = proj[:, 3 * ATTN_W:]
    qb_ref[...] = (q * Q_SCALE).astype(BF16)
    kb_ref[...] = k.astype(BF16)
    if n_kv:
        vt_ref = rest[0]
        vt = v.T.astype(BF16)
        tk = vt.shape[1] // n_kv
        for hh in range(N_HEADS):
            for c in range(n_kv):
                vt_ref[0, hh, c] = vt[hh * V_DIM:(hh + 1) * V_DIM, c * tk:(c + 1) * tk]


def _premix(x, shift, scale, gain, w_in_b, cos, sin, *, tm, rows_per_mod, seq_tiles, tk):
    t = x.shape[0]
    n_tiles = t // tm
    mod_rows = shift.shape[1]
    mod_idx = (lambda i: (i // rows_per_mod, 0, 0))
    n_kv = tm // tk if tk else 0
    out_shape = [jax.ShapeDtypeStruct((t * N_HEADS, V_DIM), F32), jax.ShapeDtypeStruct((t * N_HEADS, V_DIM), F32),
                 jax.ShapeDtypeStruct((t, POOL_W), F32), jax.ShapeDtypeStruct((t, ATTN_W), BF16),
                 jax.ShapeDtypeStruct((t, ATTN_W), BF16)]
    row_spec = pl.BlockSpec((tm, ATTN_W), lambda i: (i, 0))
    head_rows = pl.BlockSpec((tm * N_HEADS, V_DIM), lambda i: (i, 0))
    out_specs = [head_rows, head_rows, row_spec, row_spec, row_spec]
    if n_kv:
        nb = n_tiles // seq_tiles
        out_shape.append(jax.ShapeDtypeStruct((nb, N_HEADS, seq_tiles * n_kv, V_DIM, tk), BF16))
        out_specs.append(pl.BlockSpec((1, N_HEADS, n_kv, V_DIM, tk),
                                      lambda i: (i // seq_tiles, 0, i % seq_tiles, 0, 0)))
    return pl.pallas_call(
        functools.partial(_premix_kernel, n_kv=n_kv),
        grid=(n_tiles,),
        in_specs=[pl.BlockSpec((tm, D_MODEL), lambda i: (i, 0)),
                  pl.BlockSpec((1, mod_rows, D_MODEL), mod_idx),
                  pl.BlockSpec((1, mod_rows, D_MODEL), mod_idx),
                  pl.BlockSpec((1, D_MODEL), lambda i: (0, 0)),
                  pl.BlockSpec((D_MODEL, IN_COLS), lambda i: (0, 0)),
                  pl.BlockSpec((tm, LANES), lambda i: (i % seq_tiles, 0)),
                  pl.BlockSpec((tm, LANES), lambda i: (i % seq_tiles, 0))],
        out_specs=out_specs,
        out_shape=out_shape,
        compiler_params=pltpu.CompilerParams(dimension_semantics=("parallel",), vmem_limit_bytes=VMEM_LIMIT),
        name="premix",
    )(x, shift, scale, gain, w_in_b, cos, sin)


def _diff_lambda(lam_ref, lam_init):
    lam = lam_ref[...]
    s1 = jnp.sum(lam[0:1] * lam[1:2], axis=1, keepdims=True)
    s2 = jnp.sum(lam[2:3] * lam[3:4], axis=1, keepdims=True)
    return jnp.exp(s1) - jnp.exp(s2) + lam_init


def _split_maps(q):
    lane = lax.broadcasted_iota(jnp.int32, q.shape, 1)
    zero = jnp.zeros_like(q)
    return jnp.concatenate([jnp.where(lane < HEAD_DIM, q, zero), jnp.where(lane >= HEAD_DIM, q, zero)], axis=0)


def _attn_kernel(q_ref, k_ref, vt_ref, lam_ref, g_ref, o_ref, s_even, s_odd, acc_ref, *, tq, tk, hps, lam_init):
    i = pl.program_id(2)
    r = tq // tk
    heads = range(hps)
    cols = [slice(hh * V_DIM, (hh + 1) * V_DIM) for hh in heads]
    qpads = [_split_maps(q_ref[0, :, cols[hh]]) for hh in heads]

    def logits(j, buf):
        rows = pl.ds(pl.multiple_of(j * tk, tk), tk)
        for hh in heads:
            buf[hh] = _dot_nt(k_ref[0, rows, cols[hh]], qpads[hh])

    def consume(j, buf, carry, masked):
        def read(hh):
            s = buf[hh]
            if masked:
                kpos = j * tk + lax.broadcasted_iota(jnp.int32, s.shape, 0)
                qpos = i * tq + lax.broadcasted_iota(jnp.int32, s.shape, 1) % tq
                s = jnp.where(kpos <= qpos, s, NEG_INF)
            return s

        ps, alphas, out = [], [], []
        for hh in heads:
            m, l = carry[hh]
            m_new = jnp.maximum(m, jnp.max(read(hh), axis=0, keepdims=True))
            alpha = jnp.exp2(m - m_new)
            p = jnp.exp2(read(hh) - m_new)
            out.append((m_new, alpha * l + jnp.sum(p, axis=0, keepdims=True)))
            alphas.append(alpha)
            ps.append(p.astype(BF16))
        for hh in heads:
            acc_ref[hh] = alphas[hh] * acc_ref[hh] + _dot(vt_ref[0, hh, j], ps[hh])
        return tuple(out)

    def pair(p, carry):
        j = 2 * p
        logits(j + 1, s_odd)
        carry = consume(j, s_even, carry, False)
        logits(j + 2, s_even)
        return consume(j + 1, s_odd, carry, False)

    assert r == 2, "the written-out schedule handles two key blocks per query block"
    init = tuple((jnp.full((1, 2 * tq), -1e30, F32), jnp.zeros((1, 2 * tq), F32)) for _ in heads)
    acc_ref[...] = jnp.zeros(acc_ref.shape, F32)
    logits(0, s_even)
    carry = lax.fori_loop(0, i, pair, init)
    logits(2 * i + 1, s_odd)
    carry = consume(2 * i, s_even, carry, True)
    carry = consume(2 * i + 1, s_odd, carry, True)
    lam = _diff_lambda(lam_ref, lam_init)
    for hh in heads:
        o = acc_ref[hh] / carry[hh][1]
        od = o[:, :tq] - lam * o[:, tq:]
        ms = jnp.mean(od * od, axis=0, keepdims=True)
        on = od * lax.rsqrt(ms + EPS) * g_ref[...] * (1.0 - lam_init)
        o_ref[0, :, cols[hh]] = on.T.astype(BF16)


def _attn_prompt(qb, kb, vt, lam4, subln_col, *, tq, tk, hps, lam_init):
    b, s, _ = qb.shape
    nq, nkv = s // tq, s // tk
    w = hps * V_DIM
    return pl.pallas_call(
        functools.partial(_attn_kernel, tq=tq, tk=tk, hps=hps, lam_init=lam_init),
        grid=(b, N_HEADS // hps, nq),
        in_specs=[pl.BlockSpec((1, tq, w), lambda bb, h, i: (bb, i, h)),
                  pl.BlockSpec((1, s, w), lambda bb, h, i: (bb, 0, h)),
                  pl.BlockSpec((1, hps, nkv, V_DIM, tk), lambda bb, h, i: (bb, h, 0, 0, 0)),
                  pl.BlockSpec((4, HEAD_DIM), lambda bb, h, i: (0, 0)),
                  pl.BlockSpec((V_DIM, 1), lambda bb, h, i: (0, 0))],
        out_specs=pl.BlockSpec((1, tq, w), lambda bb, h, i: (bb, i, h)),
        out_shape=jax.ShapeDtypeStruct((b, s, ATTN_W), BF16),
        scratch_shapes=[pltpu.VMEM((hps, tk, 2 * tq), F32), pltpu.VMEM((hps, tk, 2 * tq), F32),
                        pltpu.VMEM((hps, V_DIM, 2 * tq), F32)],
        compiler_params=pltpu.CompilerParams(dimension_semantics=("parallel", "parallel", "arbitrary"),
                                             vmem_limit_bytes=VMEM_LIMIT),
        name="attn_prompt",
    )(qb, kb, vt, lam4, subln_col)


def _decode_kernel(pt_ref, q_ref, kn_ref, vn_ref, lam_ref, g_ref, k_hbm, v_hbm, o_ref,
                   kbuf, vbuf, sem, m_sc, l_sc, acc_sc, *, n_chunks, chunk_pages, page_rows, lam_init):
    b = pl.program_id(0)
    c = pl.program_id(1)
    nb = pl.num_programs(0)
    step = b * n_chunks + c
    slot = step % 2

    def chunk_copies(bb, cc, sl):
        copies = []
        for p in range(chunk_pages):
            phys = pt_ref[bb * (n_chunks * chunk_pages) + cc * chunk_pages + p]
            src = pl.ds(pl.multiple_of(phys * page_rows, page_rows), page_rows)
            dst = pl.ds(p * page_rows, page_rows)
            copies.append(pltpu.make_async_copy(k_hbm.at[src, :], kbuf.at[sl, dst, :], sem.at[0, sl]))
            copies.append(pltpu.make_async_copy(v_hbm.at[src, :], vbuf.at[sl, dst, :], sem.at[1, sl]))
        return copies

    @pl.when(step == 0)
    def _():
        for cp in chunk_copies(b, c, slot):
            cp.start()

    @pl.when(step + 1 < nb * n_chunks)
    def _():
        nxt = step + 1
        for cp in chunk_copies(nxt // n_chunks, nxt % n_chunks, 1 - slot):
            cp.start()

    @pl.when(c == 0)
    def _():
        m_sc[...] = jnp.full(m_sc.shape, -1e30, F32)
        l_sc[...] = jnp.zeros(l_sc.shape, F32)
        acc_sc[...] = jnp.zeros(acc_sc.shape, F32)

    qblk = _split_maps(q_ref[0])
    for cp in chunk_copies(b, c, slot):
        cp.wait()
    kc = kbuf[slot].astype(BF16)
    vc = vbuf[slot].astype(BF16)
    s = _dot_nt(qblk, kc)
    row_head = lax.broadcasted_iota(jnp.int32, s.shape, 0) % N_HEADS
    col_head = lax.broadcasted_iota(jnp.int32, s.shape, 1) % N_HEADS
    s = jnp.where(row_head == col_head, s, NEG_INF)
    m_old = m_sc[...]
    m_new = jnp.maximum(m_old, jnp.max(s, axis=1, keepdims=True))
    alpha = jnp.exp2(m_old - m_new)
    p = jnp.exp2(s - m_new)
    l_new = alpha * l_sc[...] + jnp.sum(p, axis=1, keepdims=True)
    acc_new = alpha * acc_sc[...] + _dot(p.astype(BF16), vc)
    m_sc[...] = m_new
    l_sc[...] = l_new
    acc_sc[...] = acc_new

    @pl.when(c == n_chunks - 1)
    def _():
        kn = jnp.concatenate([kn_ref[0], kn_ref[0]], axis=0).astype(F32)
        vn = jnp.concatenate([vn_ref[0], vn_ref[0]], axis=0).astype(F32)
        s_n = jnp.sum(qblk.astype(F32) * kn, axis=1, keepdims=True)
        m_f = jnp.maximum(m_new, s_n)
        a_f = jnp.exp2(m_new - m_f)
        p_n = jnp.exp2(s_n - m_f)
        l_f = a_f * l_new + p_n
        o = (a_f * acc_new + p_n * vn) / l_f
        od = o[:N_HEADS] - _diff_lambda(lam_ref, lam_init) * o[N_HEADS:]
        o_ref[0] = (_rmsnorm(od, g_ref[...]) * (1.0 - lam_init)).astype(BF16)


def _attn_decode(page_table, q3, kn3, vn3, lam4, subln_row, k2, v2, *, page_rows, lam_init):
    nb, n_pages = page_table.shape
    chunk_pages = DEC_CHUNK_PAGES
    n_chunks = n_pages // chunk_pages
    rows = chunk_pages * page_rows
    head_spec = pl.BlockSpec((1, N_HEADS, V_DIM), lambda b, c, pt: (b, 0, 0))
    grid_spec = pltpu.PrefetchScalarGridSpec(
        num_scalar_prefetch=1,
        grid=(nb, n_chunks),
        in_specs=[head_spec, head_spec, head_spec,
                  pl.BlockSpec((4, HEAD_DIM), lambda b, c, pt: (0, 0)),
                  pl.BlockSpec((1, V_DIM), lambda b, c, pt: (0, 0)),
                  pl.BlockSpec(memory_space=pl.ANY),
                  pl.BlockSpec(memory_space=pl.ANY)],
        out_specs=head_spec,
        scratch_shapes=[pltpu.VMEM((2, rows, V_DIM), F32), pltpu.VMEM((2, rows, V_DIM), F32),
                        pltpu.SemaphoreType.DMA((2, 2)),
                        pltpu.VMEM((2 * N_HEADS, 1), F32), pltpu.VMEM((2 * N_HEADS, 1), F32),
                        pltpu.VMEM((2 * N_HEADS, V_DIM), F32)])
    return pl.pallas_call(
        functools.partial(_decode_kernel, n_chunks=n_chunks, chunk_pages=chunk_pages,
                          page_rows=page_rows, lam_init=lam_init),
        grid_spec=grid_spec,
        out_shape=jax.ShapeDtypeStruct((nb, N_HEADS, V_DIM), BF16),
        compiler_params=pltpu.CompilerParams(dimension_semantics=("arbitrary", "arbitrary"),
                                             vmem_limit_bytes=VMEM_LIMIT),
        name="attn_decode",
    )(page_table.reshape(-1), q3, kn3, vn3, lam4, subln_row, k2, v2)


def _pool_project(d_groups, pw_ref, ps_ref):
    ys = [_dot(d.astype(BF16), pw_ref[g].astype(BF16)) for g, d in enumerate(d_groups)]
    return jnp.concatenate(ys, axis=1) * ps_ref[...]


def _argmax_first(x, iota, n):
    mx = jnp.max(x, axis=0, keepdims=True)
    ix = jnp.min(jnp.where(x == mx, iota, float(n)), axis=0, keepdims=True)
    return mx, ix


def _row_iota(shape):
    return lax.broadcasted_iota(jnp.int32, shape, 0).astype(F32)


def _route(h2, rwt_ref, rb_ref):
    rw = rwt_ref[...]
    rw_hi = rw.astype(BF16)
    rw_lo = (rw - rw_hi.astype(F32)).astype(BF16)
    h_hi = h2.astype(BF16)
    h_lo = (h2 - h_hi.astype(F32)).astype(BF16)
    logits = _dot_nt(rw_hi, h_hi) + (_dot_nt(rw_hi, h_lo) + _dot_nt(rw_lo, h_hi))
    scores = jax.nn.sigmoid(logits)
    sel = scores + rb_ref[...]
    tm = sel.shape[1]
    iota_g = _row_iota((GROUP_SIZE, tm))
    group_rows = []
    for g in range(N_EXPERT_GROUPS):
        xg = sel[g * GROUP_SIZE:(g + 1) * GROUP_SIZE]
        m1, i1 = _argmax_first(xg, iota_g, GROUP_SIZE)
        m2 = jnp.max(jnp.where(iota_g == i1, NEG_INF, xg), axis=0, keepdims=True)
        group_rows.append(m1 + m2)
    gscore = jnp.concatenate(group_rows, axis=0)
    iota_gr = _row_iota(gscore.shape)
    gmask = jnp.zeros(gscore.shape, F32)
    for _ in range(TOPK_GROUPS):
        _, ig = _argmax_first(gscore, iota_gr, N_EXPERT_GROUPS)
        hit = iota_gr == ig
        gmask = jnp.where(hit, 1.0, gmask)
        gscore = jnp.where(hit, NEG_INF, gscore)
    masked = jnp.concatenate(
        [jnp.where(gmask[g:g + 1] > 0.5, sel[g * GROUP_SIZE:(g + 1) * GROUP_SIZE], NEG_INF)
         for g in range(N_EXPERT_GROUPS)], axis=0)
    iota_e = _row_iota(masked.shape)
    w = jnp.zeros(masked.shape, F32)
    idx_rows, score_rows = [], []
    for _ in range(TOP_K):
        _, ie = _argmax_first(masked, iota_e, N_EXPERTS)
        hit = iota_e == ie
        w = jnp.where(hit, scores, w)
        masked = jnp.where(hit, NEG_INF, masked)
        idx_rows.append(ie)
        score_rows.append(jnp.sum(jnp.where(hit, scores, 0.0), axis=0, keepdims=True))
    wsum = jnp.sum(w, axis=0, keepdims=True)
    dense = w / wsum * ROUTED_SCALE
    idx_t = jnp.concatenate(idx_rows, axis=0).astype(jnp.int32)
    w_t = jnp.concatenate(score_rows, axis=0) / wsum * ROUTED_SCALE
    return dense, idx_t, w_t


def _to_token_rows(x, out_ref):
    n_chunks = x.shape[1] // LANES
    for i in range(x.shape[0] // 8):
        for j in range(n_chunks):
            out_ref[0, pl.ds(i * 8 * n_chunks + j, 8, stride=n_chunks), :] = x[8 * i:8 * i + 8, j * LANES:(j + 1) * LANES]


def _from_token_rows(in_ref, tm):
    n_chunks = D_MODEL // LANES
    return jnp.concatenate([in_ref[0, pl.ds(j, tm, stride=n_chunks), :] for j in range(n_chunks)], axis=1)


def _mix_tail(o_b, pz, x, g1, sh2, sc2, wout_ref, gpost_ref, gpre_ref, rwt_ref, rb_ref, x1_ref, h2_ref, route_refs):
    m = _dot(o_b, wout_ref[:ATTN_W, :]) + _dot(pz.astype(BF16), wout_ref[ATTN_W:, :])
    x1 = x + g1 * _rmsnorm(m, gpost_ref[...])
    h2 = _rmsnorm(x1, gpre_ref[...]) * (1.0 + sc2) + sh2
    x1_ref[...] = x1
    h2_ref[...] = h2.astype(BF16)
    dense, idx_t, w_t = _route(h2, rwt_ref, rb_ref)
    if len(route_refs) == 1:
        route_refs[0][...] = dense.T
    else:
        idx_ref, w_ref, h2r_ref = route_refs
        idx_ref[...] = idx_t
        w_ref[...] = w_t
        _to_token_rows(h2, h2r_ref)


def _postmix_prompt_kernel(o_ref, z_ref, zprev_ref, x_ref, g1_ref, sh2_ref, sc2_ref,
                           wout_ref, pw_ref, ps_ref, gpost_ref, gpre_ref, rwt_ref, rb_ref,
                           x1_ref, h2_ref, idx_ref, w_ref, h2r_ref, zbuf, *, seq_tiles):
    tm = z_ref.shape[0]
    it = pl.program_id(0) % seq_tiles
    zt = z_ref[...]
    zbuf[0:POOL_HALO, :] = jnp.where(it == 0, 0.0, zprev_ref[...])
    zbuf[POOL_HALO:, :] = zt
    pos = it * tm + lax.broadcasted_iota(jnp.int32, (tm, 1), 0)
    d_groups = []
    for g, w in enumerate(POOL_WINDOWS):
        cols = slice(g * POOL_GW, (g + 1) * POOL_GW)
        win = zt[:, cols]
        for s in range(1, w):
            win = win + zbuf[POOL_HALO - s:POOL_HALO - s + tm, cols]
        cnt = jnp.minimum(pos + 1, w).astype(F32)
        d_groups.append(win / cnt - zt[:, cols])
    pz = _pool_project(d_groups, pw_ref, ps_ref)
    _mix_tail(o_ref[...], pz, x_ref[...], g1_ref[0], sh2_ref[0], sc2_ref[0], wout_ref, gpost_ref, gpre_ref,
              rwt_ref, rb_ref, x1_ref, h2_ref, (idx_ref, w_ref, h2r_ref))


def _postmix_sample_kernel(o_ref, z_ref, st_ref, x_ref, g1_ref, sh2_ref, sc2_ref,
                           wout_ref, pw_ref, ps_ref, gpost_ref, gpre_ref, rwt_ref, rb_ref,
                           x1_ref, h2_ref, wt_ref, *, past_len):
    zt = z_ref[...]
    d_groups = []
    for g, w in enumerate(POOL_WINDOWS):
        cols = slice(g * POOL_GW, (g + 1) * POOL_GW)
        win = zt[:, cols]
        for s in range(1, w):
            win = win + st_ref[POOL_BUF - s][:, cols]
        d_groups.append(win / float(min(past_len + 1, w)) - zt[:, cols])
    pz = _pool_project(d_groups, pw_ref, ps_ref)
    _mix_tail(o_ref[...], pz, x_ref[...], g1_ref[0], sh2_ref[0], sc2_ref[0], wout_ref, gpost_ref, gpre_ref,
              rwt_ref, rb_ref, x1_ref, h2_ref, (wt_ref,))


def _postmix(o, z, hist, x, g1, sh2, sc2, wout_b, pool_w, pool_scale, g_post, g_pre, rwt, rb_col,
             *, tm, rows_per_mod, seq_tiles, past_len):
    t = x.shape[0]
    n_tiles = t // tm
    mod_rows = g1.shape[1]
    mod_spec = pl.BlockSpec((1, mod_rows, D_MODEL), lambda i: (i // rows_per_mod, 0, 0))
    full = lambda shape: pl.BlockSpec(shape, lambda i: (0,) * len(shape))
    if past_len is None:
        body = functools.partial(_postmix_prompt_kernel, seq_tiles=seq_tiles)
        halo_blocks = tm // POOL_HALO
        hist_spec = pl.BlockSpec((POOL_HALO, POOL_W), lambda i: (jnp.maximum(i * halo_blocks - 1, 0), 0))
        scratch = [pltpu.VMEM((POOL_HALO + tm, POOL_W), F32)]
        n_seq = n_tiles // seq_tiles
        rows8 = D_MODEL // LANES
        route_specs = [pl.BlockSpec((TOP_K, tm), lambda i: (0, i)), pl.BlockSpec((TOP_K, tm), lambda i: (0, i)),
                       pl.BlockSpec((1, tm * rows8, LANES), lambda i: (i // seq_tiles, i % seq_tiles, 0))]
        route_shapes = [jax.ShapeDtypeStruct((TOP_K, t), jnp.int32), jax.ShapeDtypeStruct((TOP_K, t), F32),
                        jax.ShapeDtypeStruct((n_seq, seq_tiles * tm * rows8, LANES), F32)]
    else:
        body = functools.partial(_postmix_sample_kernel, past_len=past_len)
        hist_spec = full(hist.shape)
        scratch = []
        route_specs = [pl.BlockSpec((tm, N_EXPERTS), lambda i: (i, 0))]
        route_shapes = [jax.ShapeDtypeStruct((t, N_EXPERTS), F32)]
    return pl.pallas_call(
        body,
        grid=(n_tiles,),
        in_specs=[pl.BlockSpec((tm, ATTN_W), lambda i: (i, 0)),
                  pl.BlockSpec((tm, POOL_W), lambda i: (i, 0)),
                  hist_spec,
                  pl.BlockSpec((tm, D_MODEL), lambda i: (i, 0)),
                  mod_spec, mod_spec, mod_spec,
                  full((D_MODEL, D_MODEL)), full(pool_w.shape), full((1, POOL_W)),
                  full((1, D_MODEL)), full((1, D_MODEL)), full((N_EXPERTS, D_MODEL)), full((N_EXPERTS, 1))],
        out_specs=[pl.BlockSpec((tm, D_MODEL), lambda i: (i, 0)),
                   pl.BlockSpec((tm, D_MODEL), lambda i: (i, 0))] + route_specs,
        out_shape=[jax.ShapeDtypeStruct((t, D_MODEL), F32), jax.ShapeDtypeStruct((t, D_MODEL), BF16)] + route_shapes,
        scratch_shapes=scratch,
        compiler_params=pltpu.CompilerParams(dimension_semantics=("parallel",), vmem_limit_bytes=VMEM_LIMIT),
        name="postmix",
    )(o, z, hist, x, g1, sh2, sc2, wout_b, pool_w, pool_scale, g_post, g_pre, rwt, rb_col)


def _swiglu(xb, wg, wu):
    return _silu(_dot(xb, wg.astype(BF16))) * _dot(xb, wu.astype(BF16))


def _moe_kernel(h_ref, wt_ref, wg_ref, wu_ref, wd_ref, sg_ref, su_ref, sd_ref, f_ref):
    e = pl.program_id(1)
    xb = h_ref[...]

    @pl.when(e == 0)
    def _():
        f_ref[...] = _dot(_swiglu(xb, sg_ref[...], su_ref[...]).astype(BF16), sd_ref[...].astype(BF16))

    wt = wt_ref[...]
    lane = lax.broadcasted_iota(jnp.int32, wt.shape, 1)
    wcol = jnp.sum(jnp.where(lane == e, wt, 0.0), axis=1, keepdims=True)
    hh = _swiglu(xb, wg_ref[0], wu_ref[0]) * wcol
    f_ref[...] += _dot(hh.astype(BF16), wd_ref[0].astype(BF16))


def _moe(h2b, wt, wg, wu, wd, sg, su, sd, *, tm):
    t = h2b.shape[0]
    full = lambda shape: pl.BlockSpec(shape, lambda i, e: (0,) * len(shape))
    return pl.pallas_call(
        _moe_kernel,
        grid=(t // tm, N_EXPERTS),
        in_specs=[pl.BlockSpec((tm, D_MODEL), lambda i, e: (i, 0)),
                  pl.BlockSpec((tm, N_EXPERTS), lambda i, e: (i, 0)),
                  pl.BlockSpec((1, D_MODEL, EXPERT_DIM), lambda i, e: (e, 0, 0)),
                  pl.BlockSpec((1, D_MODEL, EXPERT_DIM), lambda i, e: (e, 0, 0)),
                  pl.BlockSpec((1, EXPERT_DIM, D_MODEL), lambda i, e: (e, 0, 0)),
                  full(sg.shape), full(su.shape), full(sd.shape)],
        out_specs=pl.BlockSpec((tm, D_MODEL), lambda i, e: (i, 0)),
        out_shape=jax.ShapeDtypeStruct((t, D_MODEL), F32),
        compiler_params=pltpu.CompilerParams(dimension_semantics=("parallel", "arbitrary"),
                                             vmem_limit_bytes=VMEM_LIMIT),
        name="moe_experts",
    )(h2b, wt, wg, wu, wd, sg, su, sd)


def _dispatch_tables(idx_t, w_t, n_ranges, rtok, bm):
    k, _ = idx_t.shape
    per = k * rtok
    e = idx_t.reshape(k, n_ranges, rtok).transpose(1, 0, 2).reshape(n_ranges, per)
    w = w_t.reshape(k, n_ranges, rtok).transpose(1, 0, 2).reshape(n_ranges, per)
    keys = jnp.sort(e * per + jnp.arange(per, dtype=jnp.int32), axis=1)
    bounds = jnp.arange(N_EXPERTS + 1, dtype=jnp.int32) * per
    below = jnp.sum((keys[:, :, None] < bounds).astype(jnp.int32), axis=1)
    start = below[:, :N_EXPERTS]
    n_sub = (below[:, 1:] - start + bm - 1) // bm
    first_sub = jnp.cumsum(n_sub, axis=1) - n_sub
    tot_sub = per // bm + N_EXPERTS + rtok // bm + 2
    sub = jnp.arange(tot_sub, dtype=jnp.int32)
    begun = sub[None, :, None] >= first_sub[:, None, :]
    sub_start = jnp.max(jnp.where(begun, start[:, None, :], 0), axis=2)
    sub_first = jnp.max(jnp.where(begun, first_sub[:, None, :], 0), axis=2)
    sub_end = jnp.min(jnp.where(begun, per, start[:, None, :]), axis=2)
    src = (sub_start + (sub - sub_first) * bm)[:, :, None] + jnp.arange(bm, dtype=jnp.int32)
    valid = (src < sub_end[:, :, None]).reshape(n_ranges, tot_sub * bm)
    key = jnp.take_along_axis(keys, jnp.minimum(src.reshape(n_ranges, tot_sub * bm), per - 1), axis=1)
    slot = key % per
    tok = slot % rtok
    spare = rtok + jnp.arange(tot_sub * bm, dtype=jnp.int32) % bm
    src8 = jnp.where(valid, tok, 0) * 8
    dst8 = jnp.where(valid, tok, spare) * 8
    wl = jnp.where(valid, jnp.take_along_axis(w, slot, axis=1), 0.0)
    g = n_ranges * N_EXPERTS
    return first_sub.reshape(g), n_sub.reshape(g), src8[:, None, :], dst8[:, None, :], wl[:, None, :]


def _moe_sorted_kernel(first_ref, nsub_ref, src_ref, tok_ref, w_ref, h_ref, wg_ref, wu_ref, wd_ref, acc_ref,
                       wg_b, wu_b, wd_b, xg, ys, *, bm):
    r = pl.program_id(0)
    e = pl.program_id(1)
    n_chunks = D_MODEL // LANES
    n_blocks = nsub_ref[r * N_EXPERTS + e]

    @pl.when(e == 0)
    def _():
        acc_ref[...] = jnp.zeros(acc_ref.shape, F32)

    def gather(q0):
        for rr in range(bm):
            t8 = pl.multiple_of(src_ref[0, 0, q0 + rr], n_chunks)
            xg[rr * n_chunks:(rr + 1) * n_chunks, :] = h_ref[0, pl.ds(t8, n_chunks), :]

    def scatter(q0):
        for b0 in range(0, bm, SCATTER_BATCH):
            rows = range(b0, b0 + SCATTER_BATCH)
            toks = [pl.multiple_of(tok_ref[0, 0, q0 + rr], n_chunks) for rr in rows]
            wts = [w_ref[0, 0, q0 + rr] for rr in rows]
            olds = [acc_ref[0, pl.ds(t8, n_chunks), :] for t8 in toks]
            news = [o + wv * ys[rr * n_chunks:(rr + 1) * n_chunks, :] for o, wv, rr in zip(olds, wts, rows)]
            for t8, nv in zip(toks, news):
                acc_ref[0, pl.ds(t8, n_chunks), :] = nv

    def block(b, scatter_previous):
        x = jnp.concatenate([xg[pl.ds(j, bm, stride=n_chunks), :] for j in range(n_chunks)], axis=1).astype(BF16)
        gather((b + 1) * bm)
        if scatter_previous:
            scatter((b - 1) * bm)
        hh = _silu(_dot(x, wg_b[...])) * _dot(x, wu_b[...])
        y = _dot(hh.astype(BF16), wd_b[...])
        for i in range(bm // 8):
            for j in range(n_chunks):
                ys[pl.ds(i * 8 * n_chunks + j, 8, stride=n_chunks), :] = y[8 * i:8 * i + 8, j * LANES:(j + 1) * LANES]

    @pl.when(n_blocks > 0)
    def _():
        wg_b[...] = wg_ref[0].astype(BF16)
        wu_b[...] = wu_ref[0].astype(BF16)
        wd_b[...] = wd_ref[0].astype(BF16)
        gather(0)
        block(0, False)

        def body(b, carry):
            block(b, True)
            return carry

        lax.fori_loop(1, n_blocks, body, 0)
        scatter((n_blocks - 1) * bm)


def _moe_sorted(first_sub, n_sub, src8, dst8, wl, h2r, wg, wu, wd, *, bm):
    n_ranges, rows, _ = h2r.shape
    n_chunks = D_MODEL // LANES
    rtok = rows // n_chunks
    cap = rtok + 2 * bm
    one = pl.Element(1)
    window = lambda r, e, first, nsub: (r, 0, first[r * N_EXPERTS + e] * bm)
    list_spec = pl.BlockSpec((one, one, pl.Element(cap)), window, memory_space=pltpu.SMEM)
    grid_spec = pltpu.PrefetchScalarGridSpec(
        num_scalar_prefetch=2,
        grid=(n_ranges, N_EXPERTS),
        in_specs=[list_spec, list_spec, list_spec,
                  pl.BlockSpec((1, rows, LANES), lambda r, e, *_: (r, 0, 0), pipeline_mode=pl.Buffered(1)),
                  pl.BlockSpec((1, D_MODEL, EXPERT_DIM), lambda r, e, *_: (e, 0, 0)),
                  pl.BlockSpec((1, D_MODEL, EXPERT_DIM), lambda r, e, *_: (e, 0, 0)),
                  pl.BlockSpec((1, EXPERT_DIM, D_MODEL), lambda r, e, *_: (e, 0, 0))],
        out_specs=pl.BlockSpec((1, (rtok + bm) * n_chunks, LANES), lambda r, e, *_: (r, 0, 0),
                               pipeline_mode=pl.Buffered(1)),
        scratch_shapes=[pltpu.VMEM((D_MODEL, EXPERT_DIM), BF16), pltpu.VMEM((D_MODEL, EXPERT_DIM), BF16),
                        pltpu.VMEM((EXPERT_DIM, D_MODEL), BF16),
                        pltpu.VMEM((bm * n_chunks, LANES), F32), pltpu.VMEM((bm * n_chunks, LANES), F32)])
    return pl.pallas_call(
        functools.partial(_moe_sorted_kernel, bm=bm),
        grid_spec=grid_spec,
        out_shape=jax.ShapeDtypeStruct((n_ranges, (rtok + bm) * n_chunks, LANES), F32),
        compiler_params=pltpu.CompilerParams(dimension_semantics=("arbitrary", "arbitrary"),
                                             vmem_limit_bytes=VMEM_LIMIT),
        name="moe_grouped",
    )(first_sub, n_sub, src8, dst8, wl, h2r, wg, wu, wd)


def _final_grouped_kernel(x1_ref, fr_ref, h_ref, sg_ref, su_ref, sd_ref, g2_ref, gain_ref, y_ref):
    shared = _dot(_swiglu(h_ref[...], sg_ref[...], su_ref[...]).astype(BF16), sd_ref[...].astype(BF16))
    f = _from_token_rows(fr_ref, x1_ref.shape[0]) + shared
    y_ref[...] = x1_ref[...] + g2_ref[0] * _rmsnorm(f, gain_ref[...])


def _final_grouped(x1, fr, h2b, sg, su, sd, g2, gain, *, tm, seq_tiles):
    t = x1.shape[0]
    n_chunks = D_MODEL // LANES
    row = pl.BlockSpec((tm, D_MODEL), lambda i: (i, 0))
    full = lambda shape: pl.BlockSpec(shape, lambda i: (0,) * len(shape))
    return pl.pallas_call(
        _final_grouped_kernel,
        grid=(t // tm,),
        in_specs=[row,
                  pl.BlockSpec((1, tm * n_chunks, LANES), lambda i: (i // seq_tiles, i % seq_tiles, 0)),
                  row, full(sg.shape), full(su.shape), full(sd.shape),
                  pl.BlockSpec((1, 1, D_MODEL), lambda i: (i // seq_tiles, 0, 0)),
                  pl.BlockSpec((1, D_MODEL), lambda i: (0, 0))],
        out_specs=row,
        out_shape=jax.ShapeDtypeStruct((t, D_MODEL), F32),
        compiler_params=pltpu.CompilerParams(dimension_semantics=("parallel",), vmem_limit_bytes=VMEM_LIMIT),
        name="final_grouped",
    )(x1, fr, h2b, sg, su, sd, g2, gain)


def _final_kernel(x1_ref, f_ref, g2_ref, gain_ref, y_ref):
    y_ref[...] = x1_ref[...] + g2_ref[0] * _rmsnorm(f_ref[...], gain_ref[...])


def _final(x1, f, g2, gain, *, tm, rows_per_mod):
    t = x1.shape[0]
    mod_rows = g2.shape[1]
    row = pl.BlockSpec((tm, D_MODEL), lambda i: (i, 0))
    return pl.pallas_call(
        _final_kernel,
        grid=(t // tm,),
        in_specs=[row, row,
                  pl.BlockSpec((1, mod_rows, D_MODEL), lambda i: (i // rows_per_mod, 0, 0)),
                  pl.BlockSpec((1, D_MODEL), lambda i: (0, 0))],
        out_specs=row,
        out_shape=jax.ShapeDtypeStruct((t, D_MODEL), F32),
        compiler_params=pltpu.CompilerParams(dimension_semantics=("parallel",), vmem_limit_bytes=VMEM_LIMIT),
        name="final_residual",
    )(x1, f, g2, gain)


def _rope_tables(pos):
    half = HEAD_DIM // 2
    inv_freq = ROPE_THETA ** (-jnp.arange(half, dtype=F32) / half)
    ang = pos.astype(F32)[:, None] * inv_freq[None, :]
    cos, sin = jnp.cos(ang), jnp.sin(ang)
    reps = LANES // HEAD_DIM
    return jnp.tile(jnp.concatenate([cos, cos], axis=1), (1, reps)), jnp.tile(jnp.concatenate([-sin, sin], axis=1), (1, reps))


def kernel(x_prompt, x_sample, c_prompt, c_sample, cache_k, cache_v, state_pool, page_table, ada_w, ada_b, pre_mix_g, post_mix_g, pre_ffn_g, post_ffn_g, w_in, w_out, lambda_q1, lambda_k1, lambda_q2, lambda_k2, subln_g, pool_w, pool_scale, router_w, router_bias, expert_w_gate, expert_w_up, expert_w_down, shared_w_gate, shared_w_up, shared_w_down):
    depth = ada_w.shape[0]
    assert depth == 1, "single-layer step"
    nb, seq, _ = x_prompt.shape
    db, dec_seq, _ = x_sample.shape
    assert dec_seq == 1
    page_size = cache_k.shape[2]
    n_pages = page_table.shape[1]
    past_len = n_pages * page_size
    lam_init = 0.8 - 0.6 * math.exp(-0.3 * 0)
    l = 0

    c_all = jnp.concatenate([c_prompt, c_sample], axis=0)
    pad = (-c_all.shape[0]) % 16
    mod = _ada(jnp.pad(c_all, ((0, pad), (0, 0))), ada_w[l], ada_b[l][None, :])
    mod_p = [m.reshape(nb, 1, D_MODEL) for m in jnp.split(mod[:nb], 6, axis=-1)]
    mod_s = [m.reshape(1, db, D_MODEL) for m in jnp.split(mod[nb:nb + db], 6, axis=-1)]

    w_in_b = w_in[l].astype(BF16)
    w_out_b = w_out[l].astype(BF16)
    lam4 = jnp.stack([lambda_q1[l], lambda_k1[l], lambda_q2[l], lambda_k2[l]])
    row = lambda v: v.reshape(1, -1)
    rwt = router_w[l].T
    rb_col = router_bias[l].reshape(N_EXPERTS, 1)
    experts = (expert_w_gate[l], expert_w_up[l], expert_w_down[l], shared_w_gate[l], shared_w_up[l], shared_w_down[l])

    t_p = nb * seq
    tm = TM_TOKEN
    seq_tiles = seq // tm
    cos_p, sin_p = _rope_tables(jnp.arange(seq))
    xp = x_prompt.reshape(t_p, D_MODEL)
    k_p, v_p, z_p, qb, kb, vt = _premix(xp, mod_p[0], mod_p[1], row(pre_mix_g[l]), w_in_b, cos_p, sin_p,
                                        tm=tm, rows_per_mod=seq_tiles, seq_tiles=seq_tiles, tk=TK_ATTN)
    o_p = _attn_prompt(qb.reshape(nb, seq, ATTN_W), kb.reshape(nb, seq, ATTN_W), vt, lam4,
                       subln_g[l].reshape(V_DIM, 1), tq=TQ_ATTN, tk=TK_ATTN, hps=HEADS_PER_STEP,
                       lam_init=lam_init)
    x1_p, h2_p, idx_t, w_t, h2r = _postmix(o_p.reshape(t_p, ATTN_W), z_p, z_p, xp, mod_p[2], mod_p[3], mod_p[4],
                                           w_out_b, pool_w[l], row(pool_scale[l]), row(post_mix_g[l]),
                                           row(pre_ffn_g[l]), rwt, rb_col,
                                           tm=tm, rows_per_mod=seq_tiles, seq_tiles=seq_tiles, past_len=None)
    first_sub, n_sub, src8, dst8, wl = _dispatch_tables(idx_t, w_t, nb, seq, MOE_BM)
    fr_p = _moe_sorted(first_sub, n_sub, src8, dst8, wl, h2r, *experts[:3], bm=MOE_BM)
    y_p = _final_grouped(x1_p, fr_p, h2_p, *experts[3:], mod_p[5], row(post_ffn_g[l]),
                         tm=TM_FINAL, seq_tiles=seq // TM_FINAL)

    cos_s, sin_s = _rope_tables(jnp.full((db,), past_len))
    xs = x_sample.reshape(db, D_MODEL)
    k_s, v_s, z_s, qb_s, kb_s = _premix(xs, mod_s[0], mod_s[1], row(pre_mix_g[l]), w_in_b, cos_s, sin_s,
                                        tm=db, rows_per_mod=1, seq_tiles=1, tk=0)
    page_rows = page_size * N_HEADS
    k2 = cache_k[l].reshape(-1, V_DIM)
    v2 = cache_v[l].reshape(-1, V_DIM)
    o_s = _attn_decode(page_table, qb_s.reshape(db, N_HEADS, V_DIM), kb_s.reshape(db, N_HEADS, V_DIM),
                       v_s.reshape(db, N_HEADS, V_DIM), lam4, row(subln_g[l]), k2, v2,
                       page_rows=page_rows, lam_init=lam_init)
    hist_s = jnp.transpose(state_pool[l], (1, 0, 2))
    x1_s, h2_s, wt_s = _postmix(o_s.reshape(db, ATTN_W), z_s, hist_s, xs, mod_s[2], mod_s[3], mod_s[4], w_out_b,
                                pool_w[l], row(pool_scale[l]), row(post_mix_g[l]), row(pre_ffn_g[l]), rwt, rb_col,
                                tm=db, rows_per_mod=1, seq_tiles=1, past_len=past_len)
    f_s = _moe(h2_s, wt_s, *experts, tm=db)
    y_s = _final(x1_s, f_s, mod_s[5], row(post_ffn_g[l]), tm=db, rows_per_mod=1)

    pool_p = z_p.reshape(nb, seq, POOL_W)[:, seq - POOL_BUF:]
    pool_s = jnp.concatenate([state_pool[l][:, 1:], z_s[:, None, :]], axis=1)
    return (y_p.reshape(nb, seq, D_MODEL), y_s.reshape(db, 1, D_MODEL),
            k_p.reshape(1, nb, seq, N_HEADS, V_DIM), v_p.reshape(1, nb, seq, N_HEADS, V_DIM), pool_p[None],
            k_s.reshape(1, db, 1, N_HEADS, V_DIM), v_s.reshape(1, db, 1, N_HEADS, V_DIM), pool_s[None])
```

```python
import functools
import math

import jax
import jax.numpy as jnp
from jax import lax
from jax.experimental import pallas as pl
from jax.experimental.pallas import tpu as pltpu

F32 = jnp.float32
BF16 = jnp.bfloat16

D_MODEL = 1024
ATTN_W = 512
POOL_W = 512
HEAD_DIM = 64
N_HEADS = 4
V_DIM = 2 * HEAD_DIM
IN_COLS = 3 * ATTN_W + POOL_W
POOL_WINDOWS = (2, 4, 8, 16)
POOL_GW = 128
POOL_BUF = 15
POOL_HALO = 16
ROPE_THETA = 10000.0
N_EXPERTS = 64
TOP_K = 8
N_EXPERT_GROUPS = 8
GROUP_SIZE = N_EXPERTS // N_EXPERT_GROUPS
TOPK_GROUPS = 4
EXPERT_DIM = 256
ROUTED_SCALE = 2.5
EPS = 1e-6
LANES = 128
NEG_INF = float("-inf")
Q_SCALE = HEAD_DIM ** -0.5 * math.log2(math.e)
VMEM_LIMIT = 56 * 1024 * 1024

TM_TOKEN = 512
TQ_ATTN = 512
TK_ATTN = 256
HEADS_PER_STEP = 4
MOE_BM = 128
SCATTER_BATCH = 8
TM_FINAL = 512
DEC_CHUNK_PAGES = 16


def _silu(x):
    return x * jax.nn.sigmoid(x)


def _rmsnorm(x, g):
    ms = jnp.mean(x * x, axis=-1, keepdims=True)
    return x * lax.rsqrt(ms + EPS) * g


def _dot(a, b):
    return jnp.dot(a, b, preferred_element_type=F32)


def _dot_nt(a, b):
    return lax.dot_general(a, b, (((1,), (1,)), ((), ())), preferred_element_type=F32)


def _ada_kernel(c_ref, w_ref, b_ref, o_ref):
    s = _silu(c_ref[...])
    o_ref[...] = _dot(s.astype(BF16), w_ref[...].astype(BF16)) + b_ref[...]


def _ada(c_pad, w, b):
    rows, n = c_pad.shape[0], w.shape[1]
    tn = 1536
    return pl.pallas_call(
        _ada_kernel,
        grid=(n // tn,),
        in_specs=[pl.BlockSpec((rows, D_MODEL), lambda j: (0, 0)),
                  pl.BlockSpec((D_MODEL, tn), lambda j: (0, j)),
                  pl.BlockSpec((1, tn), lambda j: (0, j))],
        out_specs=pl.BlockSpec((rows, tn), lambda j: (0, j)),
        out_shape=jax.ShapeDtypeStruct((rows, n), F32),
        compiler_params=pltpu.CompilerParams(dimension_semantics=("arbitrary",), vmem_limit_bytes=VMEM_LIMIT),
        name="ada_mod",
    )(c_pad, w, b)


def _rope(x, cos, sin_signed):
    outs = []
    for c in range(x.shape[1] // LANES):
        blk = x[:, c * LANES:(c + 1) * LANES]
        lane = lax.broadcasted_iota(jnp.int32, blk.shape, 1)
        partner = jnp.where((lane % HEAD_DIM) < HEAD_DIM // 2,
                            pltpu.roll(blk, LANES - HEAD_DIM // 2, 1),
                            pltpu.roll(blk, HEAD_DIM // 2, 1))
        outs.append(blk * cos + partner * sin_signed)
    return jnp.concatenate(outs, axis=1)


def _premix_kernel(x_ref, sh_ref, sc_ref, g_ref, w_ref, cos_ref, sin_ref,
                   k_ref, v_ref, z_ref, qb_ref, kb_ref, *rest, n_kv):
    h = _rmsnorm(x_ref[...], g_ref[...]) * (1.0 + sc_ref[0]) + sh_ref[0]
    proj = _dot(h.astype(BF16), w_ref[...])
    cos, sin = cos_ref[...], sin_ref[...]
    q = _rope(proj[:, :ATTN_W], cos, sin)
    k = _rope(proj[:, ATTN_W:2 * ATTN_W], cos, sin)
    v = proj[:, 2 * ATTN_W:3 * ATTN_W]
    for out_ref, val in ((k_ref, k), (v_ref, v)):
        for i in range(val.shape[0] // 8):
            for hh in range(N_HEADS):
                out_ref[pl.ds(i * 8 * N_HEADS + hh, 8, stride=N_HEADS), :] = val[8 * i:8 * i + 8, hh * V_DIM:(hh + 1) * V_DIM]
    z_ref[...] = proj[:, 3 * ATTN_W:]
    qb_ref[...] = (q * Q_SCALE).astype(BF16)
    kb_ref[...] = k.astype(BF16)
    if n_kv:
        vt_ref = rest[0]
        vt = v.T.astype(BF16)
        tk = vt.shape[1] // n_kv
        for hh in range(N_HEADS):
            for c in range(n_kv):
                vt_ref[0, hh, c] = vt[hh * V_DIM:(hh + 1) * V_DIM, c * tk:(c + 1) * tk]


def _premix(x, shift, scale, gain, w_in_b, cos, sin, *, tm, rows_per_mod, seq_tiles, tk):
    t = x.shape[0]
    n_tiles = t // tm
    mod_rows = shift.shape[1]
    mod_idx = (lambda i: (i // rows_per_mod, 0, 0))
    n_kv = tm // tk if tk else 0
    out_shape = [jax.ShapeDtypeStruct((t * N_HEADS, V_DIM), F32), jax.ShapeDtypeStruct((t * N_HEADS, V_DIM), F32),
                 jax.ShapeDtypeStruct((t, POOL_W), F32), jax.ShapeDtypeStruct((t, ATTN_W), BF16),
                 jax.ShapeDtypeStruct((t, ATTN_W), BF16)]
    row_spec = pl.BlockSpec((tm, ATTN_W), lambda i: (i, 0))
    head_rows = pl.BlockSpec((tm * N_HEADS, V_DIM), lambda i: (i, 0))
    out_specs = [head_rows, head_rows, row_spec, row_spec, row_spec]
    if n_kv:
        nb = n_tiles // seq_tiles
        out_shape.append(jax.ShapeDtypeStruct((nb, N_HEADS, seq_tiles * n_kv, V_DIM, tk), BF16))
        out_specs.append(pl.BlockSpec((1, N_HEADS, n_kv, V_DIM, tk),
                                      lambda i: (i // seq_tiles, 0, i % seq_tiles, 0, 0)))
    return pl.pallas_call(
        functools.partial(_premix_kernel, n_kv=n_kv),
        grid=(n_tiles,),
        in_specs=[pl.BlockSpec((tm, D_MODEL), lambda i: (i, 0)),
                  pl.BlockSpec((1, mod_rows, D_MODEL), mod_idx),
                  pl.BlockSpec((1, mod_rows, D_MODEL), mod_idx),
                  pl.BlockSpec((1, D_MODEL), lambda i: (0, 0)),
                  pl.BlockSpec((D_MODEL, IN_COLS), lambda i: (0, 0)),
                  pl.BlockSpec((tm, LANES), lambda i: (i % seq_tiles, 0)),
                  pl.BlockSpec((tm, LANES), lambda i: (i % seq_tiles, 0))],
        out_specs=out_specs,
        out_shape=out_shape,
        compiler_params=pltpu.CompilerParams(dimension_semantics=("parallel",), vmem_limit_bytes=VMEM_LIMIT),
        name="premix",
    )(x, shift, scale, gain, w_in_b, cos, sin)


def _diff_lambda(lam_ref, lam_init):
    lam = lam_ref[...]
    s1 = jnp.sum(lam[0:1] * lam[1:2], axis=1, keepdims=True)
    s2 = jnp.sum(lam[2:3] * lam[3:4], axis=1, keepdims=True)
    return jnp.exp(s1) - jnp.exp(s2) + lam_init


def _split_maps(q):
    lane = lax.broadcasted_iota(jnp.int32, q.shape, 1)
    zero = jnp.zeros_like(q)
    return jnp.concatenate([jnp.where(lane < HEAD_DIM, q, zero), jnp.where(lane >= HEAD_DIM, q, zero)], axis=0)


def _attn_kernel(q_ref, k_ref, vt_ref, lam_ref, g_ref, o_ref, s_even, s_odd, acc_ref, *, tq, tk, hps, lam_init):
    i = pl.program_id(2)
    r = tq // tk
    heads = range(hps)
    cols = [slice(hh * V_DIM, (hh + 1) * V_DIM) for hh in heads]
    qpads = [_split_maps(q_ref[0, :, cols[hh]]) for hh in heads]

    def logits(j, buf):
        rows = pl.ds(pl.multiple_of(j * tk, tk), tk)
        for hh in heads:
            buf[hh] = _dot_nt(k_ref[0, rows, cols[hh]], qpads[hh])

    def consume(j, buf, carry, masked):
        def read(hh):
            s = buf[hh]
            if masked:
                kpos = j * tk + lax.broadcasted_iota(jnp.int32, s.shape, 0)
                qpos = i * tq + lax.broadcasted_iota(jnp.int32, s.shape, 1) % tq
                s = jnp.where(kpos <= qpos, s, NEG_INF)
            return s

        ps, alphas, out = [], [], []
        for hh in heads:
            m, l = carry[hh]
            m_new = jnp.maximum(m, jnp.max(read(hh), axis=0, keepdims=True))
            alpha = jnp.exp2(m - m_new)
            p = jnp.exp2(read(hh) - m_new)
            out.append((m_new, alpha * l + jnp.sum(p, axis=0, keepdims=True)))
            alphas.append(alpha)
            ps.append(p.astype(BF16))
        for hh in heads:
            acc_ref[hh] = alphas[hh] * acc_ref[hh] + _dot(vt_ref[0, hh, j], ps[hh])
        return tuple(out)

    def pair(p, carry):
        j = 2 * p
        logits(j + 1, s_odd)
        carry = consume(j, s_even, carry, False)
        logits(j + 2, s_even)
        return consume(j + 1, s_odd, carry, False)

    assert r == 2, "the written-out schedule handles two key blocks per query block"
    init = tuple((jnp.full((1, 2 * tq), -1e30, F32), jnp.zeros((1, 2 * tq), F32)) for _ in heads)
    acc_ref[...] = jnp.zeros(acc_ref.shape, F32)
    logits(0, s_even)
    carry = lax.fori_loop(0, i, pair, init)
    logits(2 * i + 1, s_odd)
    carry = consume(2 * i, s_even, carry, True)
    carry = consume(2 * i + 1, s_odd, carry, True)
    lam = _diff_lambda(lam_ref, lam_init)
    for hh in heads:
        o = acc_ref[hh] / carry[hh][1]
        od = o[:, :tq] - lam * o[:, tq:]
        ms = jnp.mean(od * od, axis=0, keepdims=True)
        on = od * lax.rsqrt(ms + EPS) * g_ref[...] * (1.0 - lam_init)
        o_ref[0, :, cols[hh]] = on.T.astype(BF16)


def _attn_prompt(qb, kb, vt, lam4, subln_col, *, tq, tk, hps, lam_init):
    b, s, _ = qb.shape
    nq, nkv = s // tq, s // tk
    w = hps * V_DIM
    return pl.pallas_call(
        functools.partial(_attn_kernel, tq=tq, tk=tk, hps=hps, lam_init=lam_init),
        grid=(b, N_HEADS // hps, nq),
        in_specs=[pl.BlockSpec((1, tq, w), lambda bb, h, i: (bb, i, h)),
                  pl.BlockSpec((1, s, w), lambda bb, h, i: (bb, 0, h)),
                  pl.BlockSpec((1, hps, nkv, V_DIM, tk), lambda bb, h, i: (bb, h, 0, 0, 0)),
                  pl.BlockSpec((4, HEAD_DIM), lambda bb, h, i: (0, 0)),
                  pl.BlockSpec((V_DIM, 1), lambda bb, h, i: (0, 0))],
        out_specs=pl.BlockSpec((1, tq, w), lambda bb, h, i: (bb, i, h)),
        out_shape=jax.ShapeDtypeStruct((b, s, ATTN_W), BF16),
        scratch_shapes=[pltpu.VMEM((hps, tk, 2 * tq), F32), pltpu.VMEM((hps, tk, 2 * tq), F32),
                        pltpu.VMEM((hps, V_DIM, 2 * tq), F32)],
        compiler_params=pltpu.CompilerParams(dimension_semantics=("parallel", "parallel", "arbitrary"),
                                             vmem_limit_bytes=VMEM_LIMIT),
        name="attn_prompt",
    )(qb, kb, vt, lam4, subln_col)


def _decode_kernel(pt_ref, q_ref, kn_ref, vn_ref, lam_ref, g_ref, k_hbm, v_hbm, o_ref,
                   kbuf, vbuf, sem, m_sc, l_sc, acc_sc, *, n_chunks, chunk_pages, page_rows, lam_init):
    b = pl.program_id(0)
    c = pl.program_id(1)
    nb = pl.num_programs(0)
    step = b * n_chunks + c
    slot = step % 2

    def chunk_copies(bb, cc, sl):
        copies = []
        for p in range(chunk_pages):
            phys = pt_ref[bb * (n_chunks * chunk_pages) + cc * chunk_pages + p]
            src = pl.ds(pl.multiple_of(phys * page_rows, page_rows), page_rows)
            dst = pl.ds(p * page_rows, page_rows)
            copies.append(pltpu.make_async_copy(k_hbm.at[src, :], kbuf.at[sl, dst, :], sem.at[0, sl]))
            copies.append(pltpu.make_async_copy(v_hbm.at[src, :], vbuf.at[sl, dst, :], sem.at[1, sl]))
        return copies

    @pl.when(step == 0)
    def _():
        for cp in chunk_copies(b, c, slot):
            cp.start()

    @pl.when(step + 1 < nb * n_chunks)
    def _():
        nxt = step + 1
        for cp in chunk_copies(nxt // n_chunks, nxt % n_chunks, 1 - slot):
            cp.start()

    @pl.when(c == 0)
    def _():
        m_sc[...] = jnp.full(m_sc.shape, -1e30, F32)
        l_sc[...] = jnp.zeros(l_sc.shape, F32)
        acc_sc[...] = jnp.zeros(acc_sc.shape, F32)

    qblk = _split_maps(q_ref[0])
    for cp in chunk_copies(b, c, slot):
        cp.wait()
    kc = kbuf[slot].astype(BF16)
    vc = vbuf[slot].astype(BF16)
    s = _dot_nt(qblk, kc)
    row_head = lax.broadcasted_iota(jnp.int32, s.shape, 0) % N_HEADS
    col_head = lax.broadcasted_iota(jnp.int32, s.shape, 1) % N_HEADS
    s = jnp.where(row_head == col_head, s, NEG_INF)
    m_old = m_sc[...]
    m_new = jnp.maximum(m_old, jnp.max(s, axis=1, keepdims=True))
    alpha = jnp.exp2(m_old - m_new)
    p = jnp.exp2(s - m_new)
    l_new = alpha * l_sc[...] + jnp.sum(p, axis=1, keepdims=True)
    acc_new = alpha * acc_sc[...] + _dot(p.astype(BF16), vc)
    m_sc[...] = m_new
    l_sc[...] = l_new
    acc_sc[...] = acc_new

    @pl.when(c == n_chunks - 1)
    def _():
        kn = jnp.concatenate([kn_ref[0], kn_ref[0]], axis=0).astype(F32)
        vn = jnp.concatenate([vn_ref[0], vn_ref[0]], axis=0).astype(F32)
        s_n = jnp.sum(qblk.astype(F32) * kn, axis=1, keepdims=True)
        m_f = jnp.maximum(m_new, s_n)
        a_f = jnp.exp2(m_new - m_f)
        p_n = jnp.exp2(s_n - m_f)
        l_f = a_f * l_new + p_n
        o = (a_f * acc_new + p_n * vn) / l_f
        od = o[:N_HEADS] - _diff_lambda(lam_ref, lam_init) * o[N_HEADS:]
        o_ref[0] = (_rmsnorm(od, g_ref[...]) * (1.0 - lam_init)).astype(BF16)


def _attn_decode(page_table, q3, kn3, vn3, lam4, subln_row, k2, v2, *, page_rows, lam_init):
    nb, n_pages = page_table.shape
    chunk_pages = DEC_CHUNK_PAGES
    n_chunks = n_pages // chunk_pages
    rows = chunk_pages * page_rows
    head_spec = pl.BlockSpec((1, N_HEADS, V_DIM), lambda b, c, pt: (b, 0, 0))
    grid_spec = pltpu.PrefetchScalarGridSpec(
        num_scalar_prefetch=1,
        grid=(nb, n_chunks),
        in_specs=[head_spec, head_spec, head_spec,
                  pl.BlockSpec((4, HEAD_DIM), lambda b, c, pt: (0, 0)),
                  pl.BlockSpec((1, V_DIM), lambda b, c, pt: (0, 0)),
                  pl.BlockSpec(memory_space=pl.ANY),
                  pl.BlockSpec(memory_space=pl.ANY)],
        out_specs=head_spec,
        scratch_shapes=[pltpu.VMEM((2, rows, V_DIM), F32), pltpu.VMEM((2, rows, V_DIM), F32),
                        pltpu.SemaphoreType.DMA((2, 2)),
                        pltpu.VMEM((2 * N_HEADS, 1), F32), pltpu.VMEM((2 * N_HEADS, 1), F32),
                        pltpu.VMEM((2 * N_HEADS, V_DIM), F32)])
    return pl.pallas_call(
        functools.partial(_decode_kernel, n_chunks=n_chunks, chunk_pages=chunk_pages,
                          page_rows=page_rows, lam_init=lam_init),
        grid_spec=grid_spec,
        out_shape=jax.ShapeDtypeStruct((nb, N_HEADS, V_DIM), BF16),
        compiler_params=pltpu.CompilerParams(dimension_semantics=("arbitrary", "arbitrary"),
                                             vmem_limit_bytes=VMEM_LIMIT),
        name="attn_decode",
    )(page_table.reshape(-1), q3, kn3, vn3, lam4, subln_row, k2, v2)


def _pool_project(d_groups, pw_ref, ps_ref):
    ys = [_dot(d.astype(BF16), pw_ref[g].astype(BF16)) for g, d in enumerate(d_groups)]
    return jnp.concatenate(ys, axis=1) * ps_ref[...]


def _argmax_first(x, iota, n):
    mx = jnp.max(x, axis=0, keepdims=True)
    ix = jnp.min(jnp.where(x == mx, iota, float(n)), axis=0, keepdims=True)
    return mx, ix


def _row_iota(shape):
    return lax.broadcasted_iota(jnp.int32, shape, 0).astype(F32)


def _route(h2, rwt_ref, rb_ref):
    rw = rwt_ref[...]
    rw_hi = rw.astype(BF16)
    rw_lo = (rw - rw_hi.astype(F32)).astype(BF16)
    h_hi = h2.astype(BF16)
    h_lo = (h2 - h_hi.astype(F32)).astype(BF16)
    logits = _dot_nt(rw_hi, h_hi) + (_dot_nt(rw_hi, h_lo) + _dot_nt(rw_lo, h_hi))
    scores = jax.nn.sigmoid(logits)
    sel = scores + rb_ref[...]
    tm = sel.shape[1]
    iota_g = _row_iota((GROUP_SIZE, tm))
    group_rows = []
    for g in range(N_EXPERT_GROUPS):
        xg = sel[g * GROUP_SIZE:(g + 1) * GROUP_SIZE]
        m1, i1 = _argmax_first(xg, iota_g, GROUP_SIZE)
        m2 = jnp.max(jnp.where(iota_g == i1, NEG_INF, xg), axis=0, keepdims=True)
        group_rows.append(m1 + m2)
    gscore = jnp.concatenate(group_rows, axis=0)
    iota_gr = _row_iota(gscore.shape)
    gmask = jnp.zeros(gscore.shape, F32)
    for _ in range(TOPK_GROUPS):
        _, ig = _argmax_first(gscore, iota_gr, N_EXPERT_GROUPS)
        hit = iota_gr == ig
        gmask = jnp.where(hit, 1.0, gmask)
        gscore = jnp.where(hit, NEG_INF, gscore)
    masked = jnp.concatenate(
        [jnp.where(gmask[g:g + 1] > 0.5, sel[g * GROUP_SIZE:(g + 1) * GROUP_SIZE], NEG_INF)
         for g in range(N_EXPERT_GROUPS)], axis=0)
    iota_e = _row_iota(masked.shape)
    w = jnp.zeros(masked.shape, F32)
    idx_rows, score_rows = [], []
    for _ in range(TOP_K):
        _, ie = _argmax_first(masked, iota_e, N_EXPERTS)
        hit = iota_e == ie
        w = jnp.where(hit, scores, w)
        masked = jnp.where(hit, NEG_INF, masked)
        idx_rows.append(ie)
        score_rows.append(jnp.sum(jnp.where(hit, scores, 0.0), axis=0, keepdims=True))
    wsum = jnp.sum(w, axis=0, keepdims=True)
    dense = w / wsum * ROUTED_SCALE
    idx_t = jnp.concatenate(idx_rows, axis=0).astype(jnp.int32)
    w_t = jnp.concatenate(score_rows, axis=0) / wsum * ROUTED_SCALE
    return dense, idx_t, w_t


def _to_token_rows(x, out_ref):
    n_chunks = x.shape[1] // LANES
    for i in range(x.shape[0] // 8):
        for j in range(n_chunks):
            out_ref[0, pl.ds(i * 8 * n_chunks + j, 8, stride=n_chunks), :] = x[8 * i:8 * i + 8, j * LANES:(j + 1) * LANES]


def _from_token_rows(in_ref, tm):
    n_chunks = D_MODEL // LANES
    return jnp.concatenate([in_ref[0, pl.ds(j, tm, stride=n_chunks), :] for j in range(n_chunks)], axis=1)


def _mix_tail(o_b, pz, x, g1, sh2, sc2, wout_ref, gpost_ref, gpre_ref, rwt_ref, rb_ref, x1_ref, h2_ref, route_refs):
    m = _dot(o_b, wout_ref[:ATTN_W, :]) + _dot(pz.astype(BF16), wout_ref[ATTN_W:, :])
    x1 = x + g1 * _rmsnorm(m, gpost_ref[...])
    h2 = _rmsnorm(x1, gpre_ref[...]) * (1.0 + sc2) + sh2
    x1_ref[...] = x1
    h2_ref[...] = h2.astype(BF16)
    dense, idx_t, w_t = _route(h2, rwt_ref, rb_ref)
    if len(route_refs) == 1:
        route_refs[0][...] = dense.T
    else:
        idx_ref, w_ref, h2r_ref = route_refs
        idx_ref[...] = idx_t
        w_ref[...] = w_t
        _to_token_rows(h2, h2r_ref)


def _postmix_prompt_kernel(o_ref, z_ref, zprev_ref, x_ref, g1_ref, sh2_ref, sc2_ref,
                           wout_ref, pw_ref, ps_ref, gpost_ref, gpre_ref, rwt_ref, rb_ref,
                           x1_ref, h2_ref, idx_ref, w_ref, h2r_ref, zbuf, *, seq_tiles):
    tm = z_ref.shape[0]
    it = pl.program_id(0) % seq_tiles
    zt = z_ref[...]
    zbuf[0:POOL_HALO, :] = jnp.where(it == 0, 0.0, zprev_ref[...])
    zbuf[POOL_HALO:, :] = zt
    pos = it * tm + lax.broadcasted_iota(jnp.int32, (tm, 1), 0)
    d_groups = []
    for g, w in enumerate(POOL_WINDOWS):
        cols = slice(g * POOL_GW, (g + 1) * POOL_GW)
        win = zt[:, cols]
        for s in range(1, w):
            win = win + zbuf[POOL_HALO - s:POOL_HALO - s + tm, cols]
        cnt = jnp.minimum(pos + 1, w).astype(F32)
        d_groups.append(win / cnt - zt[:, cols])
    pz = _pool_project(d_groups, pw_ref, ps_ref)
    _mix_tail(o_ref[...], pz, x_ref[...], g1_ref[0], sh2_ref[0], sc2_ref[0], wout_ref, gpost_ref, gpre_ref,
              rwt_ref, rb_ref, x1_ref, h2_ref, (idx_ref, w_ref, h2r_ref))


def _postmix_sample_kernel(o_ref, z_ref, st_ref, x_ref, g1_ref, sh2_ref, sc2_ref,
                           wout_ref, pw_ref, ps_ref, gpost_ref, gpre_ref, rwt_ref, rb_ref,
                           x1_ref, h2_ref, wt_ref, *, past_len):
    zt = z_ref[...]
    d_groups = []
    for g, w in enumerate(POOL_WINDOWS):
        cols = slice(g * POOL_GW, (g + 1) * POOL_GW)
        win = zt[:, cols]
        for s in range(1, w):
            win = win + st_ref[POOL_BUF - s][:, cols]
        d_groups.append(win / float(min(past_len + 1, w)) - zt[:, cols])
    pz = _pool_project(d_groups, pw_ref, ps_ref)
    _mix_tail(o_ref[...], pz, x_ref[...], g1_ref[0], sh2_ref[0], sc2_ref[0], wout_ref, gpost_ref, gpre_ref,
              rwt_ref, rb_ref, x1_ref, h2_ref, (wt_ref,))


def _postmix(o, z, hist, x, g1, sh2, sc2, wout_b, pool_w, pool_scale, g_post, g_pre, rwt, rb_col,
             *, tm, rows_per_mod, seq_tiles, past_len):
    t = x.shape[0]
    n_tiles = t // tm
    mod_rows = g1.shape[1]
    mod_spec = pl.BlockSpec((1, mod_rows, D_MODEL), lambda i: (i // rows_per_mod, 0, 0))
    full = lambda shape: pl.BlockSpec(shape, lambda i: (0,) * len(shape))
    if past_len is None:
        body = functools.partial(_postmix_prompt_kernel, seq_tiles=seq_tiles)
        halo_blocks = tm // POOL_HALO
        hist_spec = pl.BlockSpec((POOL_HALO, POOL_W), lambda i: (jnp.maximum(i * halo_blocks - 1, 0), 0))
        scratch = [pltpu.VMEM((POOL_HALO + tm, POOL_W), F32)]
        n_seq = n_tiles // seq_tiles
        rows8 = D_MODEL // LANES
        route_specs = [pl.BlockSpec((TOP_K, tm), lambda i: (0, i)), pl.BlockSpec((TOP_K, tm), lambda i: (0, i)),
                       pl.BlockSpec((1, tm * rows8, LANES), lambda i: (i // seq_tiles, i % seq_tiles, 0))]
        route_shapes = [jax.ShapeDtypeStruct((TOP_K, t), jnp.int32), jax.ShapeDtypeStruct((TOP_K, t), F32),
                        jax.ShapeDtypeStruct((n_seq, seq_tiles * tm * rows8, LANES), F32)]
    else:
        body = functools.partial(_postmix_sample_kernel, past_len=past_len)
        hist_spec = full(hist.shape)
        scratch = []
        route_specs = [pl.BlockSpec((tm, N_EXPERTS), lambda i: (i, 0))]
        route_shapes = [jax.ShapeDtypeStruct((t, N_EXPERTS), F32)]
    return pl.pallas_call(
        body,
        grid=(n_tiles,),
        in_specs=[pl.BlockSpec((tm, ATTN_W), lambda i: (i, 0)),
                  pl.BlockSpec((tm, POOL_W), lambda i: (i, 0)),
                  hist_spec,
                  pl.BlockSpec((tm, D_MODEL), lambda i: (i, 0)),
                  mod_spec, mod_spec, mod_spec,
                  full((D_MODEL, D_MODEL)), full(pool_w.shape), full((1, POOL_W)),
                  full((1, D_MODEL)), full((1, D_MODEL)), full((N_EXPERTS, D_MODEL)), full((N_EXPERTS, 1))],
        out_specs=[pl.BlockSpec((tm, D_MODEL), lambda i: (i, 0)),
                   pl.BlockSpec((tm, D_MODEL), lambda i: (i, 0))] + route_specs,
        out_shape=[jax.ShapeDtypeStruct((t, D_MODEL), F32), jax.ShapeDtypeStruct((t, D_MODEL), BF16)] + route_shapes,
        scratch_shapes=scratch,
        compiler_params=pltpu.CompilerParams(dimension_semantics=("parallel",), vmem_limit_bytes=VMEM_LIMIT),
        name="postmix",
    )(o, z, hist, x, g1, sh2, sc2, wout_b, pool_w, pool_scale, g_post, g_pre, rwt, rb_col)


def _swiglu(xb, wg, wu):
    return _silu(_dot(xb, wg.astype(BF16))) * _dot(xb, wu.astype(BF16))


def _moe_kernel(h_ref, wt_ref, wg_ref, wu_ref, wd_ref, sg_ref, su_ref, sd_ref, f_ref):
    e = pl.program_id(1)
    xb = h_ref[...]

    @pl.when(e == 0)
    def _():
        f_ref[...] = _dot(_swiglu(xb, sg_ref[...], su_ref[...]).astype(BF16), sd_ref[...].astype(BF16))

    wt = wt_ref[...]
    lane = lax.broadcasted_iota(jnp.int32, wt.shape, 1)
    wcol = jnp.sum(jnp.where(lane == e, wt, 0.0), axis=1, keepdims=True)
    hh = _swiglu(xb, wg_ref[0], wu_ref[0]) * wcol
    f_ref[...] += _dot(hh.astype(BF16), wd_ref[0].astype(BF16))


def _moe(h2b, wt, wg, wu, wd, sg, su, sd, *, tm):
    t = h2b.shape[0]
    full = lambda shape: pl.BlockSpec(shape, lambda i, e: (0,) * len(shape))
    return pl.pallas_call(
        _moe_kernel,
        grid=(t // tm, N_EXPERTS),
        in_specs=[pl.BlockSpec((tm, D_MODEL), lambda i, e: (i, 0)),
                  pl.BlockSpec((tm, N_EXPERTS), lambda i, e: (i, 0)),
                  pl.BlockSpec((1, D_MODEL, EXPERT_DIM), lambda i, e: (e, 0, 0)),
                  pl.BlockSpec((1, D_MODEL, EXPERT_DIM), lambda i, e: (e, 0, 0)),
                  pl.BlockSpec((1, EXPERT_DIM, D_MODEL), lambda i, e: (e, 0, 0)),
                  full(sg.shape), full(su.shape), full(sd.shape)],
        out_specs=pl.BlockSpec((tm, D_MODEL), lambda i, e: (i, 0)),
        out_shape=jax.ShapeDtypeStruct((t, D_MODEL), F32),
        compiler_params=pltpu.CompilerParams(dimension_semantics=("parallel", "arbitrary"),
                                             vmem_limit_bytes=VMEM_LIMIT),
        name="moe_experts",
    )(h2b, wt, wg, wu, wd, sg, su, sd)


def _dispatch_tables(idx_t, w_t, n_ranges, rtok, bm):
    k, _ = idx_t.shape
    per = k * rtok
    e = idx_t.reshape(k, n_ranges, rtok).transpose(1, 0, 2).reshape(n_ranges, per)
    w = w_t.reshape(k, n_ranges, rtok).transpose(1, 0, 2).reshape(n_ranges, per)
    keys = jnp.sort(e * per + jnp.arange(per, dtype=jnp.int32), axis=1)
    bounds = jnp.arange(N_EXPERTS + 1, dtype=jnp.int32) * per
    below = jnp.sum((keys[:, :, None] < bounds).astype(jnp.int32), axis=1)
    start = below[:, :N_EXPERTS]
    n_sub = (below[:, 1:] - start + bm - 1) // bm
    first_sub = jnp.cumsum(n_sub, axis=1) - n_sub
    tot_sub = 1 + per // bm + N_EXPERTS + rtok // bm + 3
    sub = jnp.arange(tot_sub, dtype=jnp.int32) - 1
    begun = sub[None, :, None] >= first_sub[:, None, :]
    sub_start = jnp.max(jnp.where(begun, start[:, None, :], 0), axis=2)
    sub_first = jnp.max(jnp.where(begun, first_sub[:, None, :], 0), axis=2)
    sub_end = jnp.min(jnp.where(begun, per, start[:, None, :]), axis=2)
    src = (sub_start + (sub - sub_first) * bm)[:, :, None] + jnp.arange(bm, dtype=jnp.int32)
    valid = jnp.logical_and(src < sub_end[:, :, None], (sub >= 0)[None, :, None]).reshape(n_ranges, tot_sub * bm)
    key = jnp.take_along_axis(keys, jnp.clip(src.reshape(n_ranges, tot_sub * bm), 0, per - 1), axis=1)
    slot = key % per
    tok = slot % rtok
    spare = rtok + jnp.arange(tot_sub * bm, dtype=jnp.int32) % bm
    src8 = jnp.where(valid, tok, 0) * 8
    dst8 = jnp.where(valid, tok, spare) * 8
    wl = jnp.where(valid, jnp.take_along_axis(w, slot, axis=1), 0.0)
    g = n_ranges * N_EXPERTS
    return first_sub.reshape(g), n_sub.reshape(g), src8[:, None, :], dst8[:, None, :], wl[:, None, :]


def _moe_sorted_kernel(first_ref, nsub_ref, src_ref, tok_ref, w_ref, h_ref, wg_ref, wu_ref, wd_ref, acc_ref,
                       wgu_b, wd_b, xg, ys, *, bm):
    r = pl.program_id(0)
    e = pl.program_id(1)
    n_chunks = D_MODEL // LANES
    n_blocks = nsub_ref[r * N_EXPERTS + e]
    tile = lambda i, j: pl.ds((i * n_chunks + j) * 8, 8)
    token = lambda rr: pl.ds((rr // 8) * 8 * n_chunks + rr % 8, n_chunks, stride=8)

    def gather(q0):
        for rr in range(bm):
            t8 = pl.multiple_of(src_ref[0, 0, q0 + rr], n_chunks)
            xg[token(rr), :] = h_ref[0, pl.ds(t8, n_chunks), :]

    def scatter(q0):
        for b0 in range(0, bm, SCATTER_BATCH):
            rows = range(b0, b0 + SCATTER_BATCH)
            toks = [pl.multiple_of(tok_ref[0, 0, q0 + rr], n_chunks) for rr in rows]
            wts = [w_ref[0, 0, q0 + rr] for rr in rows]
            olds = [acc_ref[0, pl.ds(t8, n_chunks), :] for t8 in toks]
            news = [o + wv * ys[token(rr), :] for o, wv, rr in zip(olds, wts, rows)]
            for t8, nv in zip(toks, news):
                acc_ref[0, pl.ds(t8, n_chunks), :] = nv

    def block(b, carry):
        x = jnp.concatenate([jnp.concatenate([xg[tile(i, j), :] for i in range(bm // 8)], axis=0)
                             for j in range(n_chunks)], axis=1).astype(BF16)
        gather((b + 2) * bm)
        scatter(b * bm)
        gu = _dot(x, wgu_b[...])
        hh = _silu(gu[:, :EXPERT_DIM]) * gu[:, EXPERT_DIM:]
        y = _dot(hh.astype(BF16), wd_b[...])
        for i in range(bm // 8):
            for j in range(n_chunks):
                ys[tile(i, j), :] = y[8 * i:8 * i + 8, j * LANES:(j + 1) * LANES]
        return carry

    @pl.when(e == 0)
    def _():
        acc_ref[...] = jnp.zeros(acc_ref.shape, F32)
        ys[...] = jnp.zeros(ys.shape, F32)
        gather(bm)

    @pl.when(n_blocks > 0)
    def _():
        wgu_b[:, :EXPERT_DIM] = wg_ref[0].astype(BF16)
        wgu_b[:, EXPERT_DIM:] = wu_ref[0].astype(BF16)
        wd_b[...] = wd_ref[0].astype(BF16)
        lax.fori_loop(0, n_blocks, block, 0)

    @pl.when(e == pl.num_programs(1) - 1)
    def _():
        scatter(n_blocks * bm)


def _moe_sorted(first_sub, n_sub, src8, dst8, wl, h2r, wg, wu, wd, *, bm):
    n_ranges, rows, _ = h2r.shape
    n_chunks = D_MODEL // LANES
    rtok = rows // n_chunks
    cap = rtok + 3 * bm
    one = pl.Element(1)
    window = lambda r, e, first, nsub: (r, 0, first[r * N_EXPERTS + e] * bm)
    list_spec = pl.BlockSpec((one, one, pl.Element(cap)), window, memory_space=pltpu.SMEM)
    grid_spec = pltpu.PrefetchScalarGridSpec(
        num_scalar_prefetch=2,
        grid=(n_ranges, N_EXPERTS),
        in_specs=[list_spec, list_spec, list_spec,
                  pl.BlockSpec((1, rows, LANES), lambda r, e, *_: (r, 0, 0), pipeline_mode=pl.Buffered(1)),
                  pl.BlockSpec((1, D_MODEL, EXPERT_DIM), lambda r, e, *_: (e, 0, 0)),
                  pl.BlockSpec((1, D_MODEL, EXPERT_DIM), lambda r, e, *_: (e, 0, 0)),
                  pl.BlockSpec((1, EXPERT_DIM, D_MODEL), lambda r, e, *_: (e, 0, 0))],
        out_specs=pl.BlockSpec((1, (rtok + bm) * n_chunks, LANES), lambda r, e, *_: (r, 0, 0),
                               pipeline_mode=pl.Buffered(1)),
        scratch_shapes=[pltpu.VMEM((D_MODEL, 2 * EXPERT_DIM), BF16), pltpu.VMEM((EXPERT_DIM, D_MODEL), BF16),
                        pltpu.VMEM((bm * n_chunks, LANES), F32), pltpu.VMEM((bm * n_chunks, LANES), F32)])
    return pl.pallas_call(
        functools.partial(_moe_sorted_kernel, bm=bm),
        grid_spec=grid_spec,
        out_shape=jax.ShapeDtypeStruct((n_ranges, (rtok + bm) * n_chunks, LANES), F32),
        compiler_params=pltpu.CompilerParams(dimension_semantics=("arbitrary", "arbitrary"),
                                             vmem_limit_bytes=VMEM_LIMIT),
        name="moe_grouped",
    )(first_sub, n_sub, src8, dst8, wl, h2r, wg, wu, wd)


def _final_grouped_kernel(x1_ref, fr_ref, h_ref, sg_ref, su_ref, sd_ref, g2_ref, gain_ref, y_ref):
    shared = _dot(_swiglu(h_ref[...], sg_ref[...], su_ref[...]).astype(BF16), sd_ref[...].astype(BF16))
    f = _from_token_rows(fr_ref, x1_ref.shape[0]) + shared
    y_ref[...] = x1_ref[...] + g2_ref[0] * _rmsnorm(f, gain_ref[...])


def _final_grouped(x1, fr, h2b, sg, su, sd, g2, gain, *, tm, seq_tiles):
    t = x1.shape[0]
    n_chunks = D_MODEL // LANES
    row = pl.BlockSpec((tm, D_MODEL), lambda i: (i, 0))
    full = lambda shape: pl.BlockSpec(shape, lambda i: (0,) * len(shape))
    return pl.pallas_call(
        _final_grouped_kernel,
        grid=(t // tm,),
        in_specs=[row,
                  pl.BlockSpec((1, tm * n_chunks, LANES), lambda i: (i // seq_tiles, i % seq_tiles, 0)),
                  row, full(sg.shape), full(su.shape), full(sd.shape),
                  pl.BlockSpec((1, 1, D_MODEL), lambda i: (i // seq_tiles, 0, 0)),
                  pl.BlockSpec((1, D_MODEL), lambda i: (0, 0))],
        out_specs=row,
        out_shape=jax.ShapeDtypeStruct((t, D_MODEL), F32),
        compiler_params=pltpu.CompilerParams(dimension_semantics=("parallel",), vmem_limit_bytes=VMEM_LIMIT),
        name="final_grouped",
    )(x1, fr, h2b, sg, su, sd, g2, gain)


def _final_kernel(x1_ref, f_ref, g2_ref, gain_ref, y_ref):
    y_ref[...] = x1_ref[...] + g2_ref[0] * _rmsnorm(f_ref[...], gain_ref[...])


def _final(x1, f, g2, gain, *, tm, rows_per_mod):
    t = x1.shape[0]
    mod_rows = g2.shape[1]
    row = pl.BlockSpec((tm, D_MODEL), lambda i: (i, 0))
    return pl.pallas_call(
        _final_kernel,
        grid=(t // tm,),
        in_specs=[row, row,
                  pl.BlockSpec((1, mod_rows, D_MODEL), lambda i: (i // rows_per_mod, 0, 0)),
                  pl.BlockSpec((1, D_MODEL), lambda i: (0, 0))],
        out_specs=row,
        out_shape=jax.ShapeDtypeStruct((t, D_MODEL), F32),
        compiler_params=pltpu.CompilerParams(dimension_semantics=("parallel",), vmem_limit_bytes=VMEM_LIMIT),
        name="final_residual",
    )(x1, f, g2, gain)


def _rope_tables(pos):
    half = HEAD_DIM // 2
    inv_freq = ROPE_THETA ** (-jnp.arange(half, dtype=F32) / half)
    ang = pos.astype(F32)[:, None] * inv_freq[None, :]
    cos, sin = jnp.cos(ang), jnp.sin(ang)
    reps = LANES // HEAD_DIM
    return jnp.tile(jnp.concatenate([cos, cos], axis=1), (1, reps)), jnp.tile(jnp.concatenate([-sin, sin], axis=1), (1, reps))


def kernel(x_prompt, x_sample, c_prompt, c_sample, cache_k, cache_v, state_pool, page_table, ada_w, ada_b, pre_mix_g, post_mix_g, pre_ffn_g, post_ffn_g, w_in, w_out, lambda_q1, lambda_k1, lambda_q2, lambda_k2, subln_g, pool_w, pool_scale, router_w, router_bias, expert_w_gate, expert_w_up, expert_w_down, shared_w_gate, shared_w_up, shared_w_down):
    depth = ada_w.shape[0]
    assert depth == 1, "single-layer step"
    nb, seq, _ = x_prompt.shape
    db, dec_seq, _ = x_sample.shape
    assert dec_seq == 1
    page_size = cache_k.shape[2]
    n_pages = page_table.shape[1]
    past_len = n_pages * page_size
    lam_init = 0.8 - 0.6 * math.exp(-0.3 * 0)
    l = 0

    c_all = jnp.concatenate([c_prompt, c_sample], axis=0)
    pad = (-c_all.shape[0]) % 16
    mod = _ada(jnp.pad(c_all, ((0, pad), (0, 0))), ada_w[l], ada_b[l][None, :])
    mod_p = [m.reshape(nb, 1, D_MODEL) for m in jnp.split(mod[:nb], 6, axis=-1)]
    mod_s = [m.reshape(1, db, D_MODEL) for m in jnp.split(mod[nb:nb + db], 6, axis=-1)]

    w_in_b = w_in[l].astype(BF16)
    w_out_b = w_out[l].astype(BF16)
    lam4 = jnp.stack([lambda_q1[l], lambda_k1[l], lambda_q2[l], lambda_k2[l]])
    row = lambda v: v.reshape(1, -1)
    rwt = router_w[l].T
    rb_col = router_bias[l].reshape(N_EXPERTS, 1)
    experts = (expert_w_gate[l], expert_w_up[l], expert_w_down[l], shared_w_gate[l], shared_w_up[l], shared_w_down[l])

    t_p = nb * seq
    tm = TM_TOKEN
    seq_tiles = seq // tm
    cos_p, sin_p = _rope_tables(jnp.arange(seq))
    xp = x_prompt.reshape(t_p, D_MODEL)
    k_p, v_p, z_p, qb, kb, vt = _premix(xp, mod_p[0], mod_p[1], row(pre_mix_g[l]), w_in_b, cos_p, sin_p,
                                        tm=tm, rows_per_mod=seq_tiles, seq_tiles=seq_tiles, tk=TK_ATTN)
    o_p = _attn_prompt(qb.reshape(nb, seq, ATTN_W), kb.reshape(nb, seq, ATTN_W), vt, lam4,
                       subln_g[l].reshape(V_DIM, 1), tq=TQ_ATTN, tk=TK_ATTN, hps=HEADS_PER_STEP,
                       lam_init=lam_init)
    x1_p, h2_p, idx_t, w_t, h2r = _postmix(o_p.reshape(t_p, ATTN_W), z_p, z_p, xp, mod_p[2], mod_p[3], mod_p[4],
                                           w_out_b, pool_w[l], row(pool_scale[l]), row(post_mix_g[l]),
                                           row(pre_ffn_g[l]), rwt, rb_col,
                                           tm=tm, rows_per_mod=seq_tiles, seq_tiles=seq_tiles, past_len=None)
    first_sub, n_sub, src8, dst8, wl = _dispatch_tables(idx_t, w_t, nb, seq, MOE_BM)
    fr_p = _moe_sorted(first_sub, n_sub, src8, dst8, wl, h2r, *experts[:3], bm=MOE_BM)
    y_p = _final_grouped(x1_p, fr_p, h2_p, *experts[3:], mod_p[5], row(post_ffn_g[l]),
                         tm=TM_FINAL, seq_tiles=seq // TM_FINAL)

    cos_s, sin_s = _rope_tables(jnp.full((db,), past_len))
    xs = x_sample.reshape(db, D_MODEL)
    k_s, v_s, z_s, qb_s, kb_s = _premix(xs, mod_s[0], mod_s[1], row(pre_mix_g[l]), w_in_b, cos_s, sin_s,
                                        tm=db, rows_per_mod=1, seq_tiles=1, tk=0)
    page_rows = page_size * N_HEADS
    k2 = cache_k[l].reshape(-1, V_DIM)
    v2 = cache_v[l].reshape(-1, V_DIM)
    o_s = _attn_decode(page_table, qb_s.reshape(db, N_HEADS, V_DIM), kb_s.reshape(db, N_HEADS, V_DIM),
                       v_s.reshape(db, N_HEADS, V_DIM), lam4, row(subln_g[l]), k2, v2,
                       page_rows=page_rows, lam_init=lam_init)
    hist_s = jnp.transpose(state_pool[l], (1, 0, 2))
    x1_s, h2_s, wt_s = _postmix(o_s.reshape(db, ATTN_W), z_s, hist_s, xs, mod_s[2], mod_s[3], mod_s[4], w_out_b,
                                pool_w[l], row(pool_scale[l]), row(post_mix_g[l]), row(pre_ffn_g[l]), rwt, rb_col,
                                tm=db, rows_per_mod=1, seq_tiles=1, past_len=past_len)
    f_s = _moe(h2_s, wt_s, *experts, tm=db)
    y_s = _final(x1_s, f_s, mod_s[5], row(post_ffn_g[l]), tm=db, rows_per_mod=1)

    pool_p = z_p.reshape(nb, seq, POOL_W)[:, seq - POOL_BUF:]
    pool_s = jnp.concatenate([state_pool[l][:, 1:], z_s[:, None, :]], axis=1)
    return (y_p.reshape(nb, seq, D_MODEL), y_s.reshape(db, 1, D_MODEL),
            k_p.reshape(1, nb, seq, N_HEADS, V_DIM), v_p.reshape(1, nb, seq, N_HEADS, V_DIM), pool_p[None],
            k_s.reshape(1, db, 1, N_HEADS, V_DIM), v_s.reshape(1, db, 1, N_HEADS, V_DIM), pool_s[None])
```

```python
import functools
import math

import jax
import jax.numpy as jnp
from jax import lax
from jax.experimental import pallas as pl
from jax.experimental.pallas import tpu as pltpu

F32 = jnp.float32
BF16 = jnp.bfloat16

D_MODEL = 1024
ATTN_W = 512
POOL_W = 512
HEAD_DIM = 64
N_HEADS = 4
V_DIM = 2 * HEAD_DIM
IN_COLS = 3 * ATTN_W + POOL_W
POOL_WINDOWS = (2, 4, 8, 16)
POOL_GW = 128
POOL_BUF = 15
POOL_HALO = 16
ROPE_THETA = 10000.0
N_EXPERTS = 64
TOP_K = 8
N_EXPERT_GROUPS = 8
GROUP_SIZE = N_EXPERTS // N_EXPERT_GROUPS
TOPK_GROUPS = 4
EXPERT_DIM = 256
ROUTED_SCALE = 2.5
EPS = 1e-6
LANES = 128
NEG_INF = float("-inf")
Q_SCALE = HEAD_DIM ** -0.5 * math.log2(math.e)
VMEM_LIMIT = 56 * 1024 * 1024

TM_TOKEN = 512
TQ_ATTN = 512
TK_ATTN = 256
HEADS_PER_STEP = 4
MOE_BM = 128
SCATTER_BATCH = 8
TM_FINAL = 512
DEC_CHUNK_PAGES = 16


def _silu(x):
    return x * jax.nn.sigmoid(x)


def _rmsnorm(x, g):
    ms = jnp.mean(x * x, axis=-1, keepdims=True)
    return x * lax.rsqrt(ms + EPS) * g


def _dot(a, b):
    return jnp.dot(a, b, preferred_element_type=F32)


def _dot_nt(a, b):
    return lax.dot_general(a, b, (((1,), (1,)), ((), ())), preferred_element_type=F32)


def _ada_kernel(c_ref, w_ref, b_ref, o_ref):
    s = _silu(c_ref[...])
    o_ref[...] = _dot(s.astype(BF16), w_ref[...].astype(BF16)) + b_ref[...]


def _ada(c_pad, w, b):
    rows, n = c_pad.shape[0], w.shape[1]
    tn = 1536
    return pl.pallas_call(
        _ada_kernel,
        grid=(n // tn,),
        in_specs=[pl.BlockSpec((rows, D_MODEL), lambda j: (0, 0)),
                  pl.BlockSpec((D_MODEL, tn), lambda j: (0, j)),
                  pl.BlockSpec((1, tn), lambda j: (0, j))],
        out_specs=pl.BlockSpec((rows, tn), lambda j: (0, j)),
        out_shape=jax.ShapeDtypeStruct((rows, n), F32),
        compiler_params=pltpu.CompilerParams(dimension_semantics=("arbitrary",), vmem_limit_bytes=VMEM_LIMIT),
        name="ada_mod",
    )(c_pad, w, b)


def _rope(x, cos, sin_signed):
    outs = []
    for c in range(x.shape[1] // LANES):
        blk = x[:, c * LANES:(c + 1) * LANES]
        lane = lax.broadcasted_iota(jnp.int32, blk.shape, 1)
        partner = jnp.where((lane % HEAD_DIM) < HEAD_DIM // 2,
                            pltpu.roll(blk, LANES - HEAD_DIM // 2, 1),
                            pltpu.roll(blk, HEAD_DIM // 2, 1))
        outs.append(blk * cos + partner * sin_signed)
    return jnp.concatenate(outs, axis=1)


def _premix_kernel(x_ref, sh_ref, sc_ref, g_ref, w_ref, cos_ref, sin_ref,
                   k_ref, v_ref, z_ref, qb_ref, kb_ref, *rest, n_kv):
    h = _rmsnorm(x_ref[...], g_ref[...]) * (1.0 + sc_ref[0]) + sh_ref[0]
    proj = _dot(h.astype(BF16), w_ref[...])
    cos, sin = cos_ref[...], sin_ref[...]
    q = _rope(proj[:, :ATTN_W], cos, sin)
    k = _rope(proj[:, ATTN_W:2 * ATTN_W], cos, sin)
    v = proj[:, 2 * ATTN_W:3 * ATTN_W]
    for out_ref, val in ((k_ref, k), (v_ref, v)):
        for i in range(val.shape[0] // 8):
            for hh in range(N_HEADS):
                out_ref[pl.ds(i * 8 * N_HEADS + hh, 8, stride=N_HEADS), :] = val[8 * i:8 * i + 8, hh * V_DIM:(hh + 1) * V_DIM]
    z_ref[...] = proj[:, 3 * ATTN_W:]
    qb_ref[...] = (q * Q_SCALE).astype(BF16)
    kb_ref[...] = k.astype(BF16)
    if n_kv:
        vt_ref = rest[0]
        vt = v.T.astype(BF16)
        tk = vt.shape[1] // n_kv
        for hh in range(N_HEADS):
            for c in range(n_kv):
                vt_ref[0, hh, c] = vt[hh * V_DIM:(hh + 1) * V_DIM, c * tk:(c + 1) * tk]


def _premix(x, shift, scale, gain, w_in_b, cos, sin, *, tm, rows_per_mod, seq_tiles, tk):
    t = x.shape[0]
    n_tiles = t // tm
    mod_rows = shift.shape[1]
    mod_idx = (lambda i: (i // rows_per_mod, 0, 0))
    n_kv = tm // tk if tk else 0
    out_shape = [jax.ShapeDtypeStruct((t * N_HEADS, V_DIM), F32), jax.ShapeDtypeStruct((t * N_HEADS, V_DIM), F32),
                 jax.ShapeDtypeStruct((t, POOL_W), F32), jax.ShapeDtypeStruct((t, ATTN_W), BF16),
                 jax.ShapeDtypeStruct((t, ATTN_W), BF16)]
    row_spec = pl.BlockSpec((tm, ATTN_W), lambda i: (i, 0))
    head_rows = pl.BlockSpec((tm * N_HEADS, V_DIM), lambda i: (i, 0))
    out_specs = [head_rows, head_rows, row_spec, row_spec, row_spec]
    if n_kv:
        nb = n_tiles // seq_tiles
        out_shape.append(jax.ShapeDtypeStruct((nb, N_HEADS, seq_tiles * n_kv, V_DIM, tk), BF16))
        out_specs.append(pl.BlockSpec((1, N_HEADS, n_kv, V_DIM, tk),
                                      lambda i: (i // seq_tiles, 0, i % seq_tiles, 0, 0)))
    return pl.pallas_call(
        functools.partial(_premix_kernel, n_kv=n_kv),
        grid=(n_tiles,),
        in_specs=[pl.BlockSpec((tm, D_MODEL), lambda i: (i, 0)),
                  pl.BlockSpec((1, mod_rows, D_MODEL), mod_idx),
                  pl.BlockSpec((1, mod_rows, D_MODEL), mod_idx),
                  pl.BlockSpec((1, D_MODEL), lambda i: (0, 0)),
                  pl.BlockSpec((D_MODEL, IN_COLS), lambda i: (0, 0)),
                  pl.BlockSpec((tm, LANES), lambda i: (i % seq_tiles, 0)),
                  pl.BlockSpec((tm, LANES), lambda i: (i % seq_tiles, 0))],
        out_specs=out_specs,
        out_shape=out_shape,
        compiler_params=pltpu.CompilerParams(dimension_semantics=("parallel",), vmem_limit_bytes=VMEM_LIMIT),
        name="premix",
    )(x, shift, scale, gain, w_in_b, cos, sin)


def _diff_lambda(lam_ref, lam_init):
    lam = lam_ref[...]
    s1 = jnp.sum(lam[0:1] * lam[1:2], axis=1, keepdims=True)
    s2 = jnp.sum(lam[2:3] * lam[3:4], axis=1, keepdims=True)
    return jnp.exp(s1) - jnp.exp(s2) + lam_init


def _split_maps(q):
    lane = lax.broadcasted_iota(jnp.int32, q.shape, 1)
    zero = jnp.zeros_like(q)
    return jnp.concatenate([jnp.where(lane < HEAD_DIM, q, zero), jnp.where(lane >= HEAD_DIM, q, zero)], axis=0)


def _attn_kernel(q_ref, k_ref, vt_ref, lam_ref, g_ref, o_ref, s_even, s_odd, acc_ref, *, tq, tk, hps, lam_init):
    i = pl.program_id(2)
    r = tq // tk
    heads = range(hps)
    cols = [slice(hh * V_DIM, (hh + 1) * V_DIM) for hh in heads]
    qpads = [_split_maps(q_ref[0, :, cols[hh]]) for hh in heads]

    def logits(j, buf):
        rows = pl.ds(pl.multiple_of(j * tk, tk), tk)
        for hh in heads:
            buf[hh] = _dot_nt(k_ref[0, rows, cols[hh]], qpads[hh])

    def consume(j, buf, carry, masked):
        def read(hh):
            s = buf[hh]
            if masked:
                kpos = j * tk + lax.broadcasted_iota(jnp.int32, s.shape, 0)
                qpos = i * tq + lax.broadcasted_iota(jnp.int32, s.shape, 1) % tq
                s = jnp.where(kpos <= qpos, s, NEG_INF)
            return s

        ps, alphas, out = [], [], []
        for hh in heads:
            m, l = carry[hh]
            m_new = jnp.maximum(m, jnp.max(read(hh), axis=0, keepdims=True))
            alpha = jnp.exp2(m - m_new)
            p = jnp.exp2(read(hh) - m_new)
            out.append((m_new, alpha * l + jnp.sum(p, axis=0, keepdims=True)))
            alphas.append(alpha)
            ps.append(p.astype(BF16))
        for hh in heads:
            acc_ref[hh] = alphas[hh] * acc_ref[hh] + _dot(vt_ref[0, hh, j], ps[hh])
        return tuple(out)

    def pair(p, carry):
        j = 2 * p
        logits(j + 1, s_odd)
        carry = consume(j, s_even, carry, False)
        logits(j + 2, s_even)
        return consume(j + 1, s_odd, carry, False)

    assert r == 2, "the written-out schedule handles two key blocks per query block"
    init = tuple((jnp.full((1, 2 * tq), -1e30, F32), jnp.zeros((1, 2 * tq), F32)) for _ in heads)
    acc_ref[...] = jnp.zeros(acc_ref.shape, F32)
    logits(0, s_even)
    carry = lax.fori_loop(0, i, pair, init)
    logits(2 * i + 1, s_odd)
    carry = consume(2 * i, s_even, carry, True)
    carry = consume(2 * i + 1, s_odd, carry, True)
    lam = _diff_lambda(lam_ref, lam_init)
    for hh in heads:
        o = acc_ref[hh] / carry[hh][1]
        od = o[:, :tq] - lam * o[:, tq:]
        ms = jnp.mean(od * od, axis=0, keepdims=True)
        on = od * lax.rsqrt(ms + EPS) * g_ref[...] * (1.0 - lam_init)
        o_ref[0, :, cols[hh]] = on.T.astype(BF16)


def _attn_prompt(qb, kb, vt, lam4, subln_col, *, tq, tk, hps, lam_init):
    b, s, _ = qb.shape
    nq, nkv = s // tq, s // tk
    w = hps * V_DIM
    return pl.pallas_call(
        functools.partial(_attn_kernel, tq=tq, tk=tk, hps=hps, lam_init=lam_init),
        grid=(b, N_HEADS // hps, nq),
        in_specs=[pl.BlockSpec((1, tq, w), lambda bb, h, i: (bb, i, h)),
                  pl.BlockSpec((1, s, w), lambda bb, h, i: (bb, 0, h)),
                  pl.BlockSpec((1, hps, nkv, V_DIM, tk), lambda bb, h, i: (bb, h, 0, 0, 0)),
                  pl.BlockSpec((4, HEAD_DIM), lambda bb, h, i: (0, 0)),
                  pl.BlockSpec((V_DIM, 1), lambda bb, h, i: (0, 0))],
        out_specs=pl.BlockSpec((1, tq, w), lambda bb, h, i: (bb, i, h)),
        out_shape=jax.ShapeDtypeStruct((b, s, ATTN_W), BF16),
        scratch_shapes=[pltpu.VMEM((hps, tk, 2 * tq), F32), pltpu.VMEM((hps, tk, 2 * tq), F32),
                        pltpu.VMEM((hps, V_DIM, 2 * tq), F32)],
        compiler_params=pltpu.CompilerParams(dimension_semantics=("parallel", "parallel", "arbitrary"),
                                             vmem_limit_bytes=VMEM_LIMIT),
        name="attn_prompt",
    )(qb, kb, vt, lam4, subln_col)


def _decode_kernel(pt_ref, q_ref, kn_ref, vn_ref, lam_ref, g_ref, k_hbm, v_hbm, o_ref,
                   kbuf, vbuf, sem, m_sc, l_sc, acc_sc, *, n_chunks, chunk_pages, page_rows, lam_init):
    b = pl.program_id(0)
    c = pl.program_id(1)
    nb = pl.num_programs(0)
    step = b * n_chunks + c
    slot = step % 2

    def chunk_copies(bb, cc, sl):
        copies = []
        for p in range(chunk_pages):
            phys = pt_ref[bb * (n_chunks * chunk_pages) + cc * chunk_pages + p]
            src = pl.ds(pl.multiple_of(phys * page_rows, page_rows), page_rows)
            dst = pl.ds(p * page_rows, page_rows)
            copies.append(pltpu.make_async_copy(k_hbm.at[src, :], kbuf.at[sl, dst, :], sem.at[0, sl]))
            copies.append(pltpu.make_async_copy(v_hbm.at[src, :], vbuf.at[sl, dst, :], sem.at[1, sl]))
        return copies

    @pl.when(step == 0)
    def _():
        for cp in chunk_copies(b, c, slot):
            cp.start()

    @pl.when(step + 1 < nb * n_chunks)
    def _():
        nxt = step + 1
        for cp in chunk_copies(nxt // n_chunks, nxt % n_chunks, 1 - slot):
            cp.start()

    @pl.when(c == 0)
    def _():
        m_sc[...] = jnp.full(m_sc.shape, -1e30, F32)
        l_sc[...] = jnp.zeros(l_sc.shape, F32)
        acc_sc[...] = jnp.zeros(acc_sc.shape, F32)

    qblk = _split_maps(q_ref[0])
    for cp in chunk_copies(b, c, slot):
        cp.wait()
    kc = kbuf[slot].astype(BF16)
    vc = vbuf[slot].astype(BF16)
    s = _dot_nt(qblk, kc)
    row_head = lax.broadcasted_iota(jnp.int32, s.shape, 0) % N_HEADS
    col_head = lax.broadcasted_iota(jnp.int32, s.shape, 1) % N_HEADS
    s = jnp.where(row_head == col_head, s, NEG_INF)
    m_old = m_sc[...]
    m_new = jnp.maximum(m_old, jnp.max(s, axis=1, keepdims=True))
    alpha = jnp.exp2(m_old - m_new)
    p = jnp.exp2(s - m_new)
    l_new = alpha * l_sc[...] + jnp.sum(p, axis=1, keepdims=True)
    acc_new = alpha * acc_sc[...] + _dot(p.astype(BF16), vc)
    m_sc[...] = m_new
    l_sc[...] = l_new
    acc_sc[...] = acc_new

    @pl.when(c == n_chunks - 1)
    def _():
        kn = jnp.concatenate([kn_ref[0], kn_ref[0]], axis=0).astype(F32)
        vn = jnp.concatenate([vn_ref[0], vn_ref[0]], axis=0).astype(F32)
        s_n = jnp.sum(qblk.astype(F32) * kn, axis=1, keepdims=True)
        m_f = jnp.maximum(m_new, s_n)
        a_f = jnp.exp2(m_new - m_f)
        p_n = jnp.exp2(s_n - m_f)
        l_f = a_f * l_new + p_n
        o = (a_f * acc_new + p_n * vn) / l_f
        od = o[:N_HEADS] - _diff_lambda(lam_ref, lam_init) * o[N_HEADS:]
        o_ref[0] = (_rmsnorm(od, g_ref[...]) * (1.0 - lam_init)).astype(BF16)


def _attn_decode(page_table, q3, kn3, vn3, lam4, subln_row, k2, v2, *, page_rows, lam_init):
    nb, n_pages = page_table.shape
    chunk_pages = DEC_CHUNK_PAGES
    n_chunks = n_pages // chunk_pages
    rows = chunk_pages * page_rows
    head_spec = pl.BlockSpec((1, N_HEADS, V_DIM), lambda b, c, pt: (b, 0, 0))
    grid_spec = pltpu.PrefetchScalarGridSpec(
        num_scalar_prefetch=1,
        grid=(nb, n_chunks),
        in_specs=[head_spec, head_spec, head_spec,
                  pl.BlockSpec((4, HEAD_DIM), lambda b, c, pt: (0, 0)),
                  pl.BlockSpec((1, V_DIM), lambda b, c, pt: (0, 0)),
                  pl.BlockSpec(memory_space=pl.ANY),
                  pl.BlockSpec(memory_space=pl.ANY)],
        out_specs=head_spec,
        scratch_shapes=[pltpu.VMEM((2, rows, V_DIM), F32), pltpu.VMEM((2, rows, V_DIM), F32),
                        pltpu.SemaphoreType.DMA((2, 2)),
                        pltpu.VMEM((2 * N_HEADS, 1), F32), pltpu.VMEM((2 * N_HEADS, 1), F32),
                        pltpu.VMEM((2 * N_HEADS, V_DIM), F32)])
    return pl.pallas_call(
        functools.partial(_decode_kernel, n_chunks=n_chunks, chunk_pages=chunk_pages,
                          page_rows=page_rows, lam_init=lam_init),
        grid_spec=grid_spec,
        out_shape=jax.ShapeDtypeStruct((nb, N_HEADS, V_DIM), BF16),
        compiler_params=pltpu.CompilerParams(dimension_semantics=("arbitrary", "arbitrary"),
                                             vmem_limit_bytes=VMEM_LIMIT),
        name="attn_decode",
    )(page_table.reshape(-1), q3, kn3, vn3, lam4, subln_row, k2, v2)


def _pool_project(d_groups, pw_ref, ps_ref):
    ys = [_dot(d.astype(BF16), pw_ref[g].astype(BF16)) for g, d in enumerate(d_groups)]
    return jnp.concatenate(ys, axis=1) * ps_ref[...]


def _argmax_first(x, iota, n):
    mx = jnp.max(x, axis=0, keepdims=True)
    ix = jnp.min(jnp.where(x == mx, iota, float(n)), axis=0, keepdims=True)
    return mx, ix


def _row_iota(shape):
    return lax.broadcasted_iota(jnp.int32, shape, 0).astype(F32)


def _route(h2, rwt_ref, rb_ref):
    rw = rwt_ref[...]
    rw_hi = rw.astype(BF16)
    rw_lo = (rw - rw_hi.astype(F32)).astype(BF16)
    h_hi = h2.astype(BF16)
    h_lo = (h2 - h_hi.astype(F32)).astype(BF16)
    logits = _dot_nt(rw_hi, h_hi) + (_dot_nt(rw_hi, h_lo) + _dot_nt(rw_lo, h_hi))
    scores = jax.nn.sigmoid(logits)
    sel = scores + rb_ref[...]
    tm = sel.shape[1]
    iota_g = _row_iota((GROUP_SIZE, tm))
    group_rows = []
    for g in range(N_EXPERT_GROUPS):
        xg = sel[g * GROUP_SIZE:(g + 1) * GROUP_SIZE]
        m1, i1 = _argmax_first(xg, iota_g, GROUP_SIZE)
        m2 = jnp.max(jnp.where(iota_g == i1, NEG_INF, xg), axis=0, keepdims=True)
        group_rows.append(m1 + m2)
    gscore = jnp.concatenate(group_rows, axis=0)
    iota_gr = _row_iota(gscore.shape)
    gmask = jnp.zeros(gscore.shape, F32)
    for _ in range(TOPK_GROUPS):
        _, ig = _argmax_first(gscore, iota_gr, N_EXPERT_GROUPS)
        hit = iota_gr == ig
        gmask = jnp.where(hit, 1.0, gmask)
        gscore = jnp.where(hit, NEG_INF, gscore)
    masked = jnp.concatenate(
        [jnp.where(gmask[g:g + 1] > 0.5, sel[g * GROUP_SIZE:(g + 1) * GROUP_SIZE], NEG_INF)
         for g in range(N_EXPERT_GROUPS)], axis=0)
    iota_e = _row_iota(masked.shape)
    w = jnp.zeros(masked.shape, F32)
    idx_rows, score_rows = [], []
    for _ in range(TOP_K):
        _, ie = _argmax_first(masked, iota_e, N_EXPERTS)
        hit = iota_e == ie
        w = jnp.where(hit, scores, w)
        masked = jnp.where(hit, NEG_INF, masked)
        idx_rows.append(ie)
        score_rows.append(jnp.sum(jnp.where(hit, scores, 0.0), axis=0, keepdims=True))
    wsum = jnp.sum(w, axis=0, keepdims=True)
    idx_t = jnp.concatenate(idx_rows, axis=0).astype(jnp.int32)
    w_t = jnp.concatenate(score_rows, axis=0) / wsum * ROUTED_SCALE
    return idx_t, w_t


def _to_token_rows(x, out_ref):
    n_chunks = x.shape[1] // LANES
    for i in range(x.shape[0] // 8):
        for j in range(n_chunks):
            out_ref[0, pl.ds(i * 8 * n_chunks + j, 8, stride=n_chunks), :] = x[8 * i:8 * i + 8, j * LANES:(j + 1) * LANES]


def _from_token_rows(in_ref, tm):
    n_chunks = D_MODEL // LANES
    return jnp.concatenate([in_ref[0, pl.ds(j, tm, stride=n_chunks), :] for j in range(n_chunks)], axis=1)


def _mix_tail(o_b, pz, x, g1, sh2, sc2, wout_ref, gpost_ref, gpre_ref, rwt_ref, rb_ref,
              x1_ref, h2_ref, idx_ref, w_ref, h2r_ref):
    m = _dot(o_b, wout_ref[:ATTN_W, :]) + _dot(pz.astype(BF16), wout_ref[ATTN_W:, :])
    x1 = x + g1 * _rmsnorm(m, gpost_ref[...])
    h2 = _rmsnorm(x1, gpre_ref[...]) * (1.0 + sc2) + sh2
    x1_ref[...] = x1
    h2_ref[...] = h2.astype(BF16)
    idx_ref[...], w_ref[...] = _route(h2, rwt_ref, rb_ref)
    _to_token_rows(h2, h2r_ref)


def _postmix_prompt_kernel(o_ref, z_ref, zprev_ref, x_ref, g1_ref, sh2_ref, sc2_ref,
                           wout_ref, pw_ref, ps_ref, gpost_ref, gpre_ref, rwt_ref, rb_ref,
                           x1_ref, h2_ref, idx_ref, w_ref, h2r_ref, zbuf, *, seq_tiles):
    tm = z_ref.shape[0]
    it = pl.program_id(0) % seq_tiles
    zt = z_ref[...]
    zbuf[0:POOL_HALO, :] = jnp.where(it == 0, 0.0, zprev_ref[...])
    zbuf[POOL_HALO:, :] = zt
    pos = it * tm + lax.broadcasted_iota(jnp.int32, (tm, 1), 0)
    d_groups = []
    for g, w in enumerate(POOL_WINDOWS):
        cols = slice(g * POOL_GW, (g + 1) * POOL_GW)
        win = zt[:, cols]
        for s in range(1, w):
            win = win + zbuf[POOL_HALO - s:POOL_HALO - s + tm, cols]
        cnt = jnp.minimum(pos + 1, w).astype(F32)
        d_groups.append(win / cnt - zt[:, cols])
    pz = _pool_project(d_groups, pw_ref, ps_ref)
    _mix_tail(o_ref[...], pz, x_ref[...], g1_ref[0], sh2_ref[0], sc2_ref[0], wout_ref, gpost_ref, gpre_ref,
              rwt_ref, rb_ref, x1_ref, h2_ref, idx_ref, w_ref, h2r_ref)


def _postmix_sample_kernel(o_ref, z_ref, st_ref, x_ref, g1_ref, sh2_ref, sc2_ref,
                           wout_ref, pw_ref, ps_ref, gpost_ref, gpre_ref, rwt_ref, rb_ref,
                           x1_ref, h2_ref, idx_ref, w_ref, h2r_ref, *, past_len):
    zt = z_ref[...]
    d_groups = []
    for g, w in enumerate(POOL_WINDOWS):
        cols = slice(g * POOL_GW, (g + 1) * POOL_GW)
        win = zt[:, cols]
        for s in range(1, w):
            win = win + st_ref[POOL_BUF - s][:, cols]
        d_groups.append(win / float(min(past_len + 1, w)) - zt[:, cols])
    pz = _pool_project(d_groups, pw_ref, ps_ref)
    _mix_tail(o_ref[...], pz, x_ref[...], g1_ref[0], sh2_ref[0], sc2_ref[0], wout_ref, gpost_ref, gpre_ref,
              rwt_ref, rb_ref, x1_ref, h2_ref, idx_ref, w_ref, h2r_ref)


def _postmix(o, z, hist, x, g1, sh2, sc2, wout_b, pool_w, pool_scale, g_post, g_pre, rwt, rb_col,
             *, tm, rows_per_mod, seq_tiles, past_len):
    t = x.shape[0]
    n_tiles = t // tm
    mod_rows = g1.shape[1]
    rows8 = D_MODEL // LANES
    mod_spec = pl.BlockSpec((1, mod_rows, D_MODEL), lambda i: (i // rows_per_mod, 0, 0))
    full = lambda shape: pl.BlockSpec(shape, lambda i: (0,) * len(shape))
    if past_len is None:
        body = functools.partial(_postmix_prompt_kernel, seq_tiles=seq_tiles)
        halo_blocks = tm // POOL_HALO
        hist_spec = pl.BlockSpec((POOL_HALO, POOL_W), lambda i: (jnp.maximum(i * halo_blocks - 1, 0), 0))
        scratch = [pltpu.VMEM((POOL_HALO + tm, POOL_W), F32)]
    else:
        body = functools.partial(_postmix_sample_kernel, past_len=past_len)
        hist_spec = full(hist.shape)
        scratch = []
    return pl.pallas_call(
        body,
        grid=(n_tiles,),
        in_specs=[pl.BlockSpec((tm, ATTN_W), lambda i: (i, 0)),
                  pl.BlockSpec((tm, POOL_W), lambda i: (i, 0)),
                  hist_spec,
                  pl.BlockSpec((tm, D_MODEL), lambda i: (i, 0)),
                  mod_spec, mod_spec, mod_spec,
                  full((D_MODEL, D_MODEL)), full(pool_w.shape), full((1, POOL_W)),
                  full((1, D_MODEL)), full((1, D_MODEL)), full((N_EXPERTS, D_MODEL)), full((N_EXPERTS, 1))],
        out_specs=[pl.BlockSpec((tm, D_MODEL), lambda i: (i, 0)),
                   pl.BlockSpec((tm, D_MODEL), lambda i: (i, 0)),
                   pl.BlockSpec((TOP_K, tm), lambda i: (0, i)), pl.BlockSpec((TOP_K, tm), lambda i: (0, i)),
                   pl.BlockSpec((1, tm * rows8, LANES), lambda i: (i // seq_tiles, i % seq_tiles, 0))],
        out_shape=[jax.ShapeDtypeStruct((t, D_MODEL), F32), jax.ShapeDtypeStruct((t, D_MODEL), BF16),
                   jax.ShapeDtypeStruct((TOP_K, t), jnp.int32), jax.ShapeDtypeStruct((TOP_K, t), F32),
                   jax.ShapeDtypeStruct((n_tiles // seq_tiles, seq_tiles * tm * rows8, LANES), F32)],
        scratch_shapes=scratch,
        compiler_params=pltpu.CompilerParams(dimension_semantics=("parallel",), vmem_limit_bytes=VMEM_LIMIT),
        name="postmix",
    )(o, z, hist, x, g1, sh2, sc2, wout_b, pool_w, pool_scale, g_post, g_pre, rwt, rb_col)


def _swiglu(xb, wg, wu):
    return _silu(_dot(xb, wg.astype(BF16))) * _dot(xb, wu.astype(BF16))


def _list_window(range_slots, bm):
    return (-(-range_slots // bm) + 2) * bm


def _dispatch_tables(idx_p, w_p, idx_x, w_x, n_ranges, rtok, bm):
    k, extra = idx_x.shape
    slots = rtok + extra
    per = k * slots
    last = (jnp.arange(n_ranges) == n_ranges - 1)[None, :, None]
    e = jnp.concatenate([idx_p.reshape(k, n_ranges, rtok),
                         jnp.where(last, idx_x[:, None, :], N_EXPERTS)], axis=2)
    w = jnp.concatenate([w_p.reshape(k, n_ranges, rtok), jnp.where(last, w_x[:, None, :], 0.0)], axis=2)
    e = e.transpose(1, 0, 2).reshape(n_ranges, per)
    w = w.transpose(1, 0, 2).reshape(n_ranges, per)
    keys = jnp.sort(e * per + jnp.arange(per, dtype=jnp.int32), axis=1)
    bounds = jnp.arange(N_EXPERTS + 1, dtype=jnp.int32) * per
    below = jnp.sum((keys[:, :, None] < bounds).astype(jnp.int32), axis=1)
    start = below[:, :N_EXPERTS]
    total = below[:, N_EXPERTS:]
    n_sub = (below[:, 1:] - start + bm - 1) // bm
    first_sub = jnp.cumsum(n_sub, axis=1) - n_sub
    tot_sub = -(-per // bm) + N_EXPERTS + _list_window(slots, bm) // bm
    sub = jnp.arange(tot_sub, dtype=jnp.int32)
    begun = sub[None, :, None] >= first_sub[:, None, :]
    sub_start = jnp.max(jnp.where(begun, start[:, None, :], 0), axis=2)
    sub_first = jnp.max(jnp.where(begun, first_sub[:, None, :], 0), axis=2)
    sub_end = jnp.min(jnp.where(begun, total[:, :, None], start[:, None, :]), axis=2)
    src = (sub_start + (sub - sub_first) * bm)[:, :, None] + jnp.arange(bm, dtype=jnp.int32)
    valid = (src < sub_end[:, :, None]).reshape(n_ranges, tot_sub * bm)
    key = jnp.take_along_axis(keys, jnp.minimum(src.reshape(n_ranges, tot_sub * bm), per - 1), axis=1)
    slot = key % per
    tok = slot % slots
    spare = slots + jnp.arange(tot_sub * bm, dtype=jnp.int32) % bm
    src8 = jnp.where(valid, tok, 0) * 8
    dst8 = jnp.where(valid, tok, spare) * 8
    wl = jnp.where(valid, jnp.take_along_axis(w, slot, axis=1), 0.0)
    g = n_ranges * N_EXPERTS
    return first_sub.reshape(g), n_sub.reshape(g), src8[:, None, :], dst8[:, None, :], wl[:, None, :]


def _moe_sorted_kernel(first_ref, nsub_ref, src_ref, tok_ref, w_ref, h_hbm, hx_hbm, wg_ref, wu_ref, wd_ref,
                       acc_ref, hbuf, sem, wg_b, wu_b, wd_b, xg, ys, *, bm):
    r = pl.program_id(0)
    e = pl.program_id(1)
    n_chunks = D_MODEL // LANES
    n_blocks = nsub_ref[r * N_EXPERTS + e]
    range_rows = h_hbm.shape[1]

    @pl.when(e == 0)
    def _():
        main = pltpu.make_async_copy(h_hbm.at[r], hbuf.at[pl.ds(0, range_rows), :], sem.at[0])
        main.start()

        @pl.when(r == pl.num_programs(0) - 1)
        def _():
            extra = pltpu.make_async_copy(hx_hbm.at[0], hbuf.at[pl.ds(range_rows, hx_hbm.shape[1]), :], sem.at[1])
            extra.start()
            extra.wait()

        acc_ref[...] = jnp.zeros(acc_ref.shape, F32)
        main.wait()

    def gather(q0):
        for rr in range(bm):
            t8 = pl.multiple_of(src_ref[0, 0, q0 + rr], n_chunks)
            xg[rr * n_chunks:(rr + 1) * n_chunks, :] = hbuf[pl.ds(t8, n_chunks), :]

    def scatter(q0):
        for b0 in range(0, bm, SCATTER_BATCH):
            rows = range(b0, b0 + SCATTER_BATCH)
            toks = [pl.multiple_of(tok_ref[0, 0, q0 + rr], n_chunks) for rr in rows]
            wts = [w_ref[0, 0, q0 + rr] for rr in rows]
            olds = [acc_ref[0, pl.ds(t8, n_chunks), :] for t8 in toks]
            news = [o + wv * ys[rr * n_chunks:(rr + 1) * n_chunks, :] for o, wv, rr in zip(olds, wts, rows)]
            for t8, nv in zip(toks, news):
                acc_ref[0, pl.ds(t8, n_chunks), :] = nv

    def block(b, scatter_previous):
        x = jnp.concatenate([xg[pl.ds(j, bm, stride=n_chunks), :] for j in range(n_chunks)], axis=1).astype(BF16)
        gather((b + 1) * bm)
        if scatter_previous:
            scatter((b - 1) * bm)
        hh = _silu(_dot(x, wg_b[...])) * _dot(x, wu_b[...])
        y = _dot(hh.astype(BF16), wd_b[...])
        for i in range(bm // 8):
            for j in range(n_chunks):
                ys[pl.ds(i * 8 * n_chunks + j, 8, stride=n_chunks), :] = y[8 * i:8 * i + 8, j * LANES:(j + 1) * LANES]

    @pl.when(n_blocks > 0)
    def _():
        wg_b[...] = wg_ref[0].astype(BF16)
        wu_b[...] = wu_ref[0].astype(BF16)
        wd_b[...] = wd_ref[0].astype(BF16)
        gather(0)
        block(0, False)

        def body(b, carry):
            block(b, True)
            return carry

        lax.fori_loop(1, n_blocks, body, 0)
        scatter((n_blocks - 1) * bm)


def _moe_sorted(first_sub, n_sub, src8, dst8, wl, h2r, h2r_x, wg, wu, wd, *, bm):
    n_ranges, rows, _ = h2r.shape
    n_chunks = D_MODEL // LANES
    rtok = (rows + h2r_x.shape[1]) // n_chunks
    cap = _list_window(rtok, bm)
    one = pl.Element(1)
    window = lambda r, e, first, nsub: (r, 0, first[r * N_EXPERTS + e] * bm)
    list_spec = pl.BlockSpec((one, one, pl.Element(cap)), window, memory_space=pltpu.SMEM)
    grid_spec = pltpu.PrefetchScalarGridSpec(
        num_scalar_prefetch=2,
        grid=(n_ranges, N_EXPERTS),
        in_specs=[list_spec, list_spec, list_spec,
                  pl.BlockSpec(memory_space=pl.ANY), pl.BlockSpec(memory_space=pl.ANY),
                  pl.BlockSpec((1, D_MODEL, EXPERT_DIM), lambda r, e, *_: (e, 0, 0)),
                  pl.BlockSpec((1, D_MODEL, EXPERT_DIM), lambda r, e, *_: (e, 0, 0)),
                  pl.BlockSpec((1, EXPERT_DIM, D_MODEL), lambda r, e, *_: (e, 0, 0))],
        out_specs=pl.BlockSpec((1, (rtok + bm) * n_chunks, LANES), lambda r, e, *_: (r, 0, 0),
                               pipeline_mode=pl.Buffered(1)),
        scratch_shapes=[pltpu.VMEM((rtok * n_chunks, LANES), F32), pltpu.SemaphoreType.DMA((2,)),
                        pltpu.VMEM((D_MODEL, EXPERT_DIM), BF16), pltpu.VMEM((D_MODEL, EXPERT_DIM), BF16),
                        pltpu.VMEM((EXPERT_DIM, D_MODEL), BF16),
                        pltpu.VMEM((bm * n_chunks, LANES), F32), pltpu.VMEM((bm * n_chunks, LANES), F32)])
    return pl.pallas_call(
        functools.partial(_moe_sorted_kernel, bm=bm),
        grid_spec=grid_spec,
        out_shape=jax.ShapeDtypeStruct((n_ranges, (rtok + bm) * n_chunks, LANES), F32),
        compiler_params=pltpu.CompilerParams(dimension_semantics=("arbitrary", "arbitrary"),
                                             vmem_limit_bytes=VMEM_LIMIT),
        name="moe_grouped",
    )(first_sub, n_sub, src8, dst8, wl, h2r, h2r_x, wg, wu, wd)


def _final_grouped_kernel(x1_ref, fr_ref, h_ref, sg_ref, su_ref, sd_ref, g2_ref, gain_ref, y_ref):
    shared = _dot(_swiglu(h_ref[...], sg_ref[...], su_ref[...]).astype(BF16), sd_ref[...].astype(BF16))
    f = _from_token_rows(fr_ref, x1_ref.shape[0]) + shared
    y_ref[...] = x1_ref[...] + g2_ref[0] * _rmsnorm(f, gain_ref[...])


def _final_grouped(x1, fr, h2b, sg, su, sd, g2, gain, *, tm, fr_index, mod_index):
    t = x1.shape[0]
    n_chunks = D_MODEL // LANES
    row = pl.BlockSpec((tm, D_MODEL), lambda i: (i, 0))
    full = lambda shape: pl.BlockSpec(shape, lambda i: (0,) * len(shape))
    return pl.pallas_call(
        _final_grouped_kernel,
        grid=(t // tm,),
        in_specs=[row,
                  pl.BlockSpec((1, tm * n_chunks, LANES), lambda i: (*fr_index(i), 0)),
                  row, full(sg.shape), full(su.shape), full(sd.shape),
                  pl.BlockSpec((1, g2.shape[1], D_MODEL), lambda i: (mod_index(i), 0, 0)),
                  pl.BlockSpec((1, D_MODEL), lambda i: (0, 0))],
        out_specs=row,
        out_shape=jax.ShapeDtypeStruct((t, D_MODEL), F32),
        compiler_params=pltpu.CompilerParams(dimension_semantics=("parallel",), vmem_limit_bytes=VMEM_LIMIT),
        name="final_grouped",
    )(x1, fr, h2b, sg, su, sd, g2, gain)


def _rope_tables(pos):
    half = HEAD_DIM // 2
    inv_freq = ROPE_THETA ** (-jnp.arange(half, dtype=F32) / half)
    ang = pos.astype(F32)[:, None] * inv_freq[None, :]
    cos, sin = jnp.cos(ang), jnp.sin(ang)
    reps = LANES // HEAD_DIM
    return jnp.tile(jnp.concatenate([cos, cos], axis=1), (1, reps)), jnp.tile(jnp.concatenate([-sin, sin], axis=1), (1, reps))


def kernel(x_prompt, x_sample, c_prompt, c_sample, cache_k, cache_v, state_pool, page_table, ada_w, ada_b, pre_mix_g, post_mix_g, pre_ffn_g, post_ffn_g, w_in, w_out, lambda_q1, lambda_k1, lambda_q2, lambda_k2, subln_g, pool_w, pool_scale, router_w, router_bias, expert_w_gate, expert_w_up, expert_w_down, shared_w_gate, shared_w_up, shared_w_down):
    depth = ada_w.shape[0]
    assert depth == 1, "single-layer step"
    nb, seq, _ = x_prompt.shape
    db, dec_seq, _ = x_sample.shape
    assert dec_seq == 1
    page_size = cache_k.shape[2]
    n_pages = page_table.shape[1]
    past_len = n_pages * page_size
    lam_init = 0.8 - 0.6 * math.exp(-0.3 * 0)
    l = 0

    c_all = jnp.concatenate([c_prompt, c_sample], axis=0)
    pad = (-c_all.shape[0]) % 16
    mod = _ada(jnp.pad(c_all, ((0, pad), (0, 0))), ada_w[l], ada_b[l][None, :])
    mod_p = [m.reshape(nb, 1, D_MODEL) for m in jnp.split(mod[:nb], 6, axis=-1)]
    mod_s = [m.reshape(1, db, D_MODEL) for m in jnp.split(mod[nb:nb + db], 6, axis=-1)]

    w_in_b = w_in[l].astype(BF16)
    w_out_b = w_out[l].astype(BF16)
    lam4 = jnp.stack([lambda_q1[l], lambda_k1[l], lambda_q2[l], lambda_k2[l]])
    row = lambda v: v.reshape(1, -1)
    rwt = router_w[l].T
    rb_col = router_bias[l].reshape(N_EXPERTS, 1)
    experts = (expert_w_gate[l], expert_w_up[l], expert_w_down[l], shared_w_gate[l], shared_w_up[l], shared_w_down[l])

    t_p = nb * seq
    tm = TM_TOKEN
    seq_tiles = seq // tm
    cos_p, sin_p = _rope_tables(jnp.arange(seq))
    xp = x_prompt.reshape(t_p, D_MODEL)
    k_p, v_p, z_p, qb, kb, vt = _premix(xp, mod_p[0], mod_p[1], row(pre_mix_g[l]), w_in_b, cos_p, sin_p,
                                        tm=tm, rows_per_mod=seq_tiles, seq_tiles=seq_tiles, tk=TK_ATTN)
    o_p = _attn_prompt(qb.reshape(nb, seq, ATTN_W), kb.reshape(nb, seq, ATTN_W), vt, lam4,
                       subln_g[l].reshape(V_DIM, 1), tq=TQ_ATTN, tk=TK_ATTN, hps=HEADS_PER_STEP,
                       lam_init=lam_init)
    x1_p, h2_p, idx_p, w_p, h2r = _postmix(o_p.reshape(t_p, ATTN_W), z_p, z_p, xp, mod_p[2], mod_p[3], mod_p[4],
                                           w_out_b, pool_w[l], row(pool_scale[l]), row(post_mix_g[l]),
                                           row(pre_ffn_g[l]), rwt, rb_col,
                                           tm=tm, rows_per_mod=seq_tiles, seq_tiles=seq_tiles, past_len=None)

    cos_s, sin_s = _rope_tables(jnp.full((db,), past_len))
    xs = x_sample.reshape(db, D_MODEL)
    k_s, v_s, z_s, qb_s, kb_s = _premix(xs, mod_s[0], mod_s[1], row(pre_mix_g[l]), w_in_b, cos_s, sin_s,
                                        tm=db, rows_per_mod=1, seq_tiles=1, tk=0)
    page_rows = page_size * N_HEADS
    k2 = cache_k[l].reshape(-1, V_DIM)
    v2 = cache_v[l].reshape(-1, V_DIM)
    o_s = _attn_decode(page_table, qb_s.reshape(db, N_HEADS, V_DIM), kb_s.reshape(db, N_HEADS, V_DIM),
                       v_s.reshape(db, N_HEADS, V_DIM), lam4, row(subln_g[l]), k2, v2,
                       page_rows=page_rows, lam_init=lam_init)
    hist_s = jnp.transpose(state_pool[l], (1, 0, 2))
    x1_s, h2_s, idx_s, w_s, h2r_s = _postmix(o_s.reshape(db, ATTN_W), z_s, hist_s, xs, mod_s[2], mod_s[3], mod_s[4],
                                             w_out_b, pool_w[l], row(pool_scale[l]), row(post_mix_g[l]),
                                             row(pre_ffn_g[l]), rwt, rb_col,
                                             tm=db, rows_per_mod=1, seq_tiles=1, past_len=past_len)

    first_sub, n_sub, src8, dst8, wl = _dispatch_tables(idx_p, w_p, idx_s, w_s, nb, seq, MOE_BM)
    fr = _moe_sorted(first_sub, n_sub, src8, dst8, wl, h2r, h2r_s, *experts[:3], bm=MOE_BM)
    final_tiles = seq // TM_FINAL
    y_p = _final_grouped(x1_p, fr, h2_p, *experts[3:], mod_p[5], row(post_ffn_g[l]), tm=TM_FINAL,
                         fr_index=lambda i: (i // final_tiles, i % final_tiles), mod_index=lambda i: i // final_tiles)
    y_s = _final_grouped(x1_s, fr, h2_s, *experts[3:], mod_s[5], row(post_ffn_g[l]), tm=db,
                         fr_index=lambda i: (nb - 1, seq // db), mod_index=lambda i: 0)

    pool_p = z_p.reshape(nb, seq, POOL_W)[:, seq - POOL_BUF:]
    pool_s = jnp.concatenate([state_pool[l][:, 1:], z_s[:, None, :]], axis=1)
    return (y_p.reshape(nb, seq, D_MODEL), y_s.reshape(db, 1, D_MODEL),
            k_p.reshape(1, nb, seq, N_HEADS, V_DIM), v_p.reshape(1, nb, seq, N_HEADS, V_DIM), pool_p[None],
            k_s.reshape(1, db, 1, N_HEADS, V_DIM), v_s.reshape(1, db, 1, N_HEADS, V_DIM), pool_s[None])
```

```python
import functools
import math

import jax
import jax.numpy as jnp
from jax import lax
from jax.experimental import pallas as pl
from jax.experimental.pallas import tpu as pltpu

F32 = jnp.float32
BF16 = jnp.bfloat16

D_MODEL = 1024
ATTN_W = 512
POOL_W = 512
HEAD_DIM = 64
N_HEADS = 4
V_DIM = 2 * HEAD_DIM
IN_COLS = 3 * ATTN_W + POOL_W
POOL_WINDOWS = (2, 4, 8, 16)
POOL_GW = 128
POOL_BUF = 15
POOL_HALO = 16
ROPE_THETA = 10000.0
N_EXPERTS = 64
TOP_K = 8
N_EXPERT_GROUPS = 8
GROUP_SIZE = N_EXPERTS // N_EXPERT_GROUPS
TOPK_GROUPS = 4
EXPERT_DIM = 256
ROUTED_SCALE = 2.5
EPS = 1e-6
LANES = 128
NEG_INF = float("-inf")
Q_SCALE = HEAD_DIM ** -0.5 * math.log2(math.e)
VMEM_LIMIT = 56 * 1024 * 1024

TM_TOKEN = 512
TQ_ATTN = 512
TK_ATTN = 256
HEADS_PER_STEP = 4
MOE_BM = 128
SCATTER_BATCH = 8
TM_FINAL = 512
DEC_CHUNK_PAGES = 32


def _silu(x):
    return x * jax.nn.sigmoid(x)


def _rmsnorm(x, g):
    ms = jnp.mean(x * x, axis=-1, keepdims=True)
    return x * lax.rsqrt(ms + EPS) * g


def _dot(a, b):
    return jnp.dot(a, b, preferred_element_type=F32)


def _dot_nt(a, b):
    return lax.dot_general(a, b, (((1,), (1,)), ((), ())), preferred_element_type=F32)


def _ada_kernel(c_ref, w_ref, b_ref, o_ref):
    s = _silu(c_ref[...])
    o_ref[...] = _dot(s.astype(BF16), w_ref[...].astype(BF16)) + b_ref[...]


def _ada(c_pad, w, b):
    rows, n = c_pad.shape[0], w.shape[1]
    tn = 1536
    return pl.pallas_call(
        _ada_kernel,
        grid=(n // tn,),
        in_specs=[pl.BlockSpec((rows, D_MODEL), lambda j: (0, 0)),
                  pl.BlockSpec((D_MODEL, tn), lambda j: (0, j)),
                  pl.BlockSpec((1, tn), lambda j: (0, j))],
        out_specs=pl.BlockSpec((rows, tn), lambda j: (0, j)),
        out_shape=jax.ShapeDtypeStruct((rows, n), F32),
        compiler_params=pltpu.CompilerParams(dimension_semantics=("arbitrary",), vmem_limit_bytes=VMEM_LIMIT),
        name="ada_mod",
    )(c_pad, w, b)


def _rope(x, cos, sin_signed):
    outs = []
    for c in range(x.shape[1] // LANES):
        blk = x[:, c * LANES:(c + 1) * LANES]
        lane = lax.broadcasted_iota(jnp.int32, blk.shape, 1)
        partner = jnp.where((lane % HEAD_DIM) < HEAD_DIM // 2,
                            pltpu.roll(blk, LANES - HEAD_DIM // 2, 1),
                            pltpu.roll(blk, HEAD_DIM // 2, 1))
        outs.append(blk * cos + partner * sin_signed)
    return jnp.concatenate(outs, axis=1)


def _premix_kernel(x_ref, sh_ref, sc_ref, g_ref, w_ref, cos_ref, sin_ref,
                   k_ref, v_ref, z_ref, qb_ref, kb_ref, *rest, n_kv):
    h = _rmsnorm(x_ref[...], g_ref[...]) * (1.0 + sc_ref[0]) + sh_ref[0]
    proj = _dot(h.astype(BF16), w_ref[...])
    cos, sin = cos_ref[...], sin_ref[...]
    q = _rope(proj[:, :ATTN_W], cos, sin)
    k = _rope(proj[:, ATTN_W:2 * ATTN_W], cos, sin)
    v = proj[:, 2 * ATTN_W:3 * ATTN_W]
    for out_ref, val in ((k_ref, k), (v_ref, v)):
        for i in range(val.shape[0] // 8):
            for hh in range(N_HEADS):
                out_ref[pl.ds(i * 8 * N_HEADS + hh, 8, stride=N_HEADS), :] = val[8 * i:8 * i + 8, hh * V_DIM:(hh + 1) * V_DIM]
    z_ref[...] = proj[:, 3 * ATTN_W:]
    qb_ref[...] = (q * Q_SCALE).astype(BF16)
    kb_ref[...] = k.astype(BF16)
    if n_kv:
        vt_ref = rest[0]
        vt = v.T.astype(BF16)
        tk = vt.shape[1] // n_kv
        for hh in range(N_HEADS):
            for c in range(n_kv):
                vt_ref[0, hh, c] = vt[hh * V_DIM:(hh + 1) * V_DIM, c * tk:(c + 1) * tk]


def _premix(x, shift, scale, gain, w_in_b, cos, sin, *, tm, rows_per_mod, seq_tiles, tk):
    t = x.shape[0]
    n_tiles = t // tm
    mod_rows = shift.shape[1]
    mod_idx = (lambda i: (i // rows_per_mod, 0, 0))
    n_kv = tm // tk if tk else 0
    out_shape = [jax.ShapeDtypeStruct((t * N_HEADS, V_DIM), F32), jax.ShapeDtypeStruct((t * N_HEADS, V_DIM), F32),
                 jax.ShapeDtypeStruct((t, POOL_W), F32), jax.ShapeDtypeStruct((t, ATTN_W), BF16),
                 jax.ShapeDtypeStruct((t, ATTN_W), BF16)]
    row_spec = pl.BlockSpec((tm, ATTN_W), lambda i: (i, 0))
    head_rows = pl.BlockSpec((tm * N_HEADS, V_DIM), lambda i: (i, 0))
    out_specs = [head_rows, head_rows, row_spec, row_spec, row_spec]
    if n_kv:
        nb = n_tiles // seq_tiles
        out_shape.append(jax.ShapeDtypeStruct((nb, N_HEADS, seq_tiles * n_kv, V_DIM, tk), BF16))
        out_specs.append(pl.BlockSpec((1, N_HEADS, n_kv, V_DIM, tk),
                                      lambda i: (i // seq_tiles, 0, i % seq_tiles, 0, 0)))
    return pl.pallas_call(
        functools.partial(_premix_kernel, n_kv=n_kv),
        grid=(n_tiles,),
        in_specs=[pl.BlockSpec((tm, D_MODEL), lambda i: (i, 0)),
                  pl.BlockSpec((1, mod_rows, D_MODEL), mod_idx),
                  pl.BlockSpec((1, mod_rows, D_MODEL), mod_idx),
                  pl.BlockSpec((1, D_MODEL), lambda i: (0, 0)),
                  pl.BlockSpec((D_MODEL, IN_COLS), lambda i: (0, 0)),
                  pl.BlockSpec((tm, LANES), lambda i: (i % seq_tiles, 0)),
                  pl.BlockSpec((tm, LANES), lambda i: (i % seq_tiles, 0))],
        out_specs=out_specs,
        out_shape=out_shape,
        compiler_params=pltpu.CompilerParams(dimension_semantics=("parallel",), vmem_limit_bytes=VMEM_LIMIT),
        name="premix",
    )(x, shift, scale, gain, w_in_b, cos, sin)


def _diff_lambda(lam_ref, lam_init):
    lam = lam_ref[...]
    s1 = jnp.sum(lam[0:1] * lam[1:2], axis=1, keepdims=True)
    s2 = jnp.sum(lam[2:3] * lam[3:4], axis=1, keepdims=True)
    return jnp.exp(s1) - jnp.exp(s2) + lam_init


def _split_maps(q):
    lane = lax.broadcasted_iota(jnp.int32, q.shape, 1)
    zero = jnp.zeros_like(q)
    return jnp.concatenate([jnp.where(lane < HEAD_DIM, q, zero), jnp.where(lane >= HEAD_DIM, q, zero)], axis=0)


def _attn_kernel(q_ref, k_ref, vt_ref, lam_ref, g_ref, o_ref, s_even, s_odd, acc_ref, *, tq, tk, hps, lam_init):
    i = pl.program_id(2)
    r = tq // tk
    heads = range(hps)
    cols = [slice(hh * V_DIM, (hh + 1) * V_DIM) for hh in heads]
    qpads = [_split_maps(q_ref[0, :, cols[hh]]) for hh in heads]

    def logits(j, buf):
        rows = pl.ds(pl.multiple_of(j * tk, tk), tk)
        for hh in heads:
            buf[hh] = _dot_nt(k_ref[0, rows, cols[hh]], qpads[hh])

    def consume(j, buf, carry, masked):
        def read(hh):
            s = buf[hh]
            if masked:
                kpos = j * tk + lax.broadcasted_iota(jnp.int32, s.shape, 0)
                qpos = i * tq + lax.broadcasted_iota(jnp.int32, s.shape, 1) % tq
                s = jnp.where(kpos <= qpos, s, NEG_INF)
            return s

        ps, alphas, out = [], [], []
        for hh in heads:
            m, l = carry[hh]
            m_new = jnp.maximum(m, jnp.max(read(hh), axis=0, keepdims=True))
            alpha = jnp.exp2(m - m_new)
            p = jnp.exp2(read(hh) - m_new)
            out.append((m_new, alpha * l + jnp.sum(p, axis=0, keepdims=True)))
            alphas.append(alpha)
            ps.append(p.astype(BF16))
        for hh in heads:
            acc_ref[hh] = alphas[hh] * acc_ref[hh] + _dot(vt_ref[0, hh, j], ps[hh])
        return tuple(out)

    def pair(p, carry):
        j = 2 * p
        logits(j + 1, s_odd)
        carry = consume(j, s_even, carry, False)
        logits(j + 2, s_even)
        return consume(j + 1, s_odd, carry, False)

    assert r == 2, "the written-out schedule handles two key blocks per query block"
    init = tuple((jnp.full((1, 2 * tq), -1e30, F32), jnp.zeros((1, 2 * tq), F32)) for _ in heads)
    acc_ref[...] = jnp.zeros(acc_ref.shape, F32)
    logits(0, s_even)
    carry = lax.fori_loop(0, i, pair, init)
    logits(2 * i + 1, s_odd)
    carry = consume(2 * i, s_even, carry, True)
    carry = consume(2 * i + 1, s_odd, carry, True)
    lam = _diff_lambda(lam_ref, lam_init)
    for hh in heads:
        o = acc_ref[hh] / carry[hh][1]
        od = o[:, :tq] - lam * o[:, tq:]
        ms = jnp.mean(od * od, axis=0, keepdims=True)
        on = od * lax.rsqrt(ms + EPS) * g_ref[...] * (1.0 - lam_init)
        o_ref[0, :, cols[hh]] = on.T.astype(BF16)


def _attn_prompt(qb, kb, vt, lam4, subln_col, *, tq, tk, hps, lam_init):
    b, s, _ = qb.shape
    nq, nkv = s // tq, s // tk
    w = hps * V_DIM
    return pl.pallas_call(
        functools.partial(_attn_kernel, tq=tq, tk=tk, hps=hps, lam_init=lam_init),
        grid=(b, N_HEADS // hps, nq),
        in_specs=[pl.BlockSpec((1, tq, w), lambda bb, h, i: (bb, i, h)),
                  pl.BlockSpec((1, s, w), lambda bb, h, i: (bb, 0, h)),
                  pl.BlockSpec((1, hps, nkv, V_DIM, tk), lambda bb, h, i: (bb, h, 0, 0, 0)),
                  pl.BlockSpec((4, HEAD_DIM), lambda bb, h, i: (0, 0)),
                  pl.BlockSpec((V_DIM, 1), lambda bb, h, i: (0, 0))],
        out_specs=pl.BlockSpec((1, tq, w), lambda bb, h, i: (bb, i, h)),
        out_shape=jax.ShapeDtypeStruct((b, s, ATTN_W), BF16),
        scratch_shapes=[pltpu.VMEM((hps, tk, 2 * tq), F32), pltpu.VMEM((hps, tk, 2 * tq), F32),
                        pltpu.VMEM((hps, V_DIM, 2 * tq), F32)],
        compiler_params=pltpu.CompilerParams(dimension_semantics=("parallel", "parallel", "arbitrary"),
                                             vmem_limit_bytes=VMEM_LIMIT),
        name="attn_prompt",
    )(qb, kb, vt, lam4, subln_col)


def _decode_kernel(pt_ref, q_ref, kn_ref, vn_ref, lam_ref, g_ref, k_hbm, v_hbm, o_ref,
                   kbuf, vbuf, sem, m_sc, l_sc, acc_sc, *, n_chunks, chunk_pages, page_rows, lam_init):
    b = pl.program_id(0)
    c = pl.program_id(1)
    nb = pl.num_programs(0)
    step = b * n_chunks + c
    slot = step % 2

    def chunk_copies(bb, cc, sl):
        copies = []
        for p in range(chunk_pages):
            phys = pt_ref[bb * (n_chunks * chunk_pages) + cc * chunk_pages + p]
            src = pl.ds(pl.multiple_of(phys * page_rows, page_rows), page_rows)
            dst = pl.ds(p * page_rows, page_rows)
            copies.append(pltpu.make_async_copy(k_hbm.at[src, :], kbuf.at[sl, dst, :], sem.at[0, sl]))
            copies.append(pltpu.make_async_copy(v_hbm.at[src, :], vbuf.at[sl, dst, :], sem.at[1, sl]))
        return copies

    @pl.when(step == 0)
    def _():
        for cp in chunk_copies(b, c, slot):
            cp.start()

    @pl.when(step + 1 < nb * n_chunks)
    def _():
        nxt = step + 1
        for cp in chunk_copies(nxt // n_chunks, nxt % n_chunks, 1 - slot):
            cp.start()

    @pl.when(c == 0)
    def _():
        m_sc[...] = jnp.full(m_sc.shape, -1e30, F32)
        l_sc[...] = jnp.zeros(l_sc.shape, F32)
        acc_sc[...] = jnp.zeros(acc_sc.shape, F32)

    qblk = _split_maps(q_ref[0])
    for cp in chunk_copies(b, c, slot):
        cp.wait()
    kc = kbuf[slot].astype(BF16)
    vc = vbuf[slot].astype(BF16)
    s = _dot_nt(qblk, kc)
    row_head = lax.broadcasted_iota(jnp.int32, s.shape, 0) % N_HEADS
    col_head = lax.broadcasted_iota(jnp.int32, s.shape, 1) % N_HEADS
    s = jnp.where(row_head == col_head, s, NEG_INF)
    m_old = m_sc[...]
    m_new = jnp.maximum(m_old, jnp.max(s, axis=1, keepdims=True))
    alpha = jnp.exp2(m_old - m_new)
    p = jnp.exp2(s - m_new)
    l_new = alpha * l_sc[...] + jnp.sum(p, axis=1, keepdims=True)
    acc_new = alpha * acc_sc[...] + _dot(p.astype(BF16), vc)
    m_sc[...] = m_new
    l_sc[...] = l_new
    acc_sc[...] = acc_new

    @pl.when(c == n_chunks - 1)
    def _():
        kn = jnp.concatenate([kn_ref[0], kn_ref[0]], axis=0).astype(F32)
        vn = jnp.concatenate([vn_ref[0], vn_ref[0]], axis=0).astype(F32)
        s_n = jnp.sum(qblk.astype(F32) * kn, axis=1, keepdims=True)
        m_f = jnp.maximum(m_new, s_n)
        a_f = jnp.exp2(m_new - m_f)
        p_n = jnp.exp2(s_n - m_f)
        l_f = a_f * l_new + p_n
        o = (a_f * acc_new + p_n * vn) / l_f
        od = o[:N_HEADS] - _diff_lambda(lam_ref, lam_init) * o[N_HEADS:]
        o_ref[0] = (_rmsnorm(od, g_ref[...]) * (1.0 - lam_init)).astype(BF16)


def _attn_decode(page_table, q3, kn3, vn3, lam4, subln_row, k2, v2, *, page_rows, lam_init):
    nb, n_pages = page_table.shape
    chunk_pages = DEC_CHUNK_PAGES
    n_chunks = n_pages // chunk_pages
    rows = chunk_pages * page_rows
    head_spec = pl.BlockSpec((1, N_HEADS, V_DIM), lambda b, c, pt: (b, 0, 0))
    grid_spec = pltpu.PrefetchScalarGridSpec(
        num_scalar_prefetch=1,
        grid=(nb, n_chunks),
        in_specs=[head_spec, head_spec, head_spec,
                  pl.BlockSpec((4, HEAD_DIM), lambda b, c, pt: (0, 0)),
                  pl.BlockSpec((1, V_DIM), lambda b, c, pt: (0, 0)),
                  pl.BlockSpec(memory_space=pl.ANY),
                  pl.BlockSpec(memory_space=pl.ANY)],
        out_specs=head_spec,
        scratch_shapes=[pltpu.VMEM((2, rows, V_DIM), F32), pltpu.VMEM((2, rows, V_DIM), F32),
                        pltpu.SemaphoreType.DMA((2, 2)),
                        pltpu.VMEM((2 * N_HEADS, 1), F32), pltpu.VMEM((2 * N_HEADS, 1), F32),
                        pltpu.VMEM((2 * N_HEADS, V_DIM), F32)])
    return pl.pallas_call(
        functools.partial(_decode_kernel, n_chunks=n_chunks, chunk_pages=chunk_pages,
                          page_rows=page_rows, lam_init=lam_init),
        grid_spec=grid_spec,
        out_shape=jax.ShapeDtypeStruct((nb, N_HEADS, V_DIM), BF16),
        compiler_params=pltpu.CompilerParams(dimension_semantics=("arbitrary", "arbitrary"),
                                             vmem_limit_bytes=VMEM_LIMIT),
        name="attn_decode",
    )(page_table.reshape(-1), q3, kn3, vn3, lam4, subln_row, k2, v2)


def _pool_project(d_groups, pw_ref, ps_ref):
    ys = [_dot(d.astype(BF16), pw_ref[g].astype(BF16)) for g, d in enumerate(d_groups)]
    return jnp.concatenate(ys, axis=1) * ps_ref[...]


def _argmax_first(x, iota, n):
    mx = jnp.max(x, axis=0, keepdims=True)
    ix = jnp.min(jnp.where(x == mx, iota, float(n)), axis=0, keepdims=True)
    return mx, ix


def _row_iota(shape):
    return lax.broadcasted_iota(jnp.int32, shape, 0).astype(F32)


def _route(h2, rwt_ref, rb_ref):
    rw = rwt_ref[...]
    rw_hi = rw.astype(BF16)
    rw_lo = (rw - rw_hi.astype(F32)).astype(BF16)
    h_hi = h2.astype(BF16)
    h_lo = (h2 - h_hi.astype(F32)).astype(BF16)
    logits = _dot_nt(rw_hi, h_hi) + (_dot_nt(rw_hi, h_lo) + _dot_nt(rw_lo, h_hi))
    scores = jax.nn.sigmoid(logits)
    sel = scores + rb_ref[...]
    tm = sel.shape[1]
    iota_g = _row_iota((GROUP_SIZE, tm))
    group_rows = []
    for g in range(N_EXPERT_GROUPS):
        xg = sel[g * GROUP_SIZE:(g + 1) * GROUP_SIZE]
        m1, i1 = _argmax_first(xg, iota_g, GROUP_SIZE)
        m2 = jnp.max(jnp.where(iota_g == i1, NEG_INF, xg), axis=0, keepdims=True)
        group_rows.append(m1 + m2)
    gscore = jnp.concatenate(group_rows, axis=0)
    iota_gr = _row_iota(gscore.shape)
    gmask = jnp.zeros(gscore.shape, F32)
    for _ in range(TOPK_GROUPS):
        _, ig = _argmax_first(gscore, iota_gr, N_EXPERT_GROUPS)
        hit = iota_gr == ig
        gmask = jnp.where(hit, 1.0, gmask)
        gscore = jnp.where(hit, NEG_INF, gscore)
    masked = jnp.concatenate(
        [jnp.where(gmask[g:g + 1] > 0.5, sel[g * GROUP_SIZE:(g + 1) * GROUP_SIZE], NEG_INF)
         for g in range(N_EXPERT_GROUPS)], axis=0)
    iota_e = _row_iota(masked.shape)
    w = jnp.zeros(masked.shape, F32)
    idx_rows, score_rows = [], []
    for _ in range(TOP_K):
        _, ie = _argmax_first(masked, iota_e, N_EXPERTS)
        hit = iota_e == ie
        w = jnp.where(hit, scores, w)
        masked = jnp.where(hit, NEG_INF, masked)
        idx_rows.append(ie)
        score_rows.append(jnp.sum(jnp.where(hit, scores, 0.0), axis=0, keepdims=True))
    wsum = jnp.sum(w, axis=0, keepdims=True)
    idx_t = jnp.concatenate(idx_rows, axis=0).astype(jnp.int32)
    w_t = jnp.concatenate(score_rows, axis=0) / wsum * ROUTED_SCALE
    return idx_t, w_t


def _to_token_rows(x, out_ref):
    n_chunks = x.shape[1] // LANES
    for i in range(x.shape[0] // 8):
        for j in range(n_chunks):
            out_ref[0, pl.ds(i * 8 * n_chunks + j, 8, stride=n_chunks), :] = x[8 * i:8 * i + 8, j * LANES:(j + 1) * LANES]


def _from_token_rows(in_ref, tm):
    n_chunks = D_MODEL // LANES
    return jnp.concatenate([in_ref[0, pl.ds(j, tm, stride=n_chunks), :] for j in range(n_chunks)], axis=1)


def _mix_tail(o_b, pz, x, g1, sh2, sc2, wout_ref, gpost_ref, gpre_ref, rwt_ref, rb_ref,
              x1_ref, h2_ref, idx_ref, w_ref, h2r_ref):
    m = _dot(o_b, wout_ref[:ATTN_W, :]) + _dot(pz.astype(BF16), wout_ref[ATTN_W:, :])
    x1 = x + g1 * _rmsnorm(m, gpost_ref[...])
    h2 = _rmsnorm(x1, gpre_ref[...]) * (1.0 + sc2) + sh2
    x1_ref[...] = x1
    h2_ref[...] = h2.astype(BF16)
    idx_ref[...], w_ref[...] = _route(h2, rwt_ref, rb_ref)
    _to_token_rows(h2, h2r_ref)


def _postmix_prompt_kernel(o_ref, z_ref, zprev_ref, x_ref, g1_ref, sh2_ref, sc2_ref,
                           wout_ref, pw_ref, ps_ref, gpost_ref, gpre_ref, rwt_ref, rb_ref,
                           x1_ref, h2_ref, idx_ref, w_ref, h2r_ref, zbuf, *, seq_tiles):
    tm = z_ref.shape[0]
    it = pl.program_id(0) % seq_tiles
    zt = z_ref[...]
    zbuf[0:POOL_HALO, :] = jnp.where(it == 0, 0.0, zprev_ref[...])
    zbuf[POOL_HALO:, :] = zt
    pos = it * tm + lax.broadcasted_iota(jnp.int32, (tm, 1), 0)
    d_groups = []
    for g, w in enumerate(POOL_WINDOWS):
        cols = slice(g * POOL_GW, (g + 1) * POOL_GW)
        win = zt[:, cols]
        for s in range(1, w):
            win = win + zbuf[POOL_HALO - s:POOL_HALO - s + tm, cols]
        cnt = jnp.minimum(pos + 1, w).astype(F32)
        d_groups.append(win / cnt - zt[:, cols])
    pz = _pool_project(d_groups, pw_ref, ps_ref)
    _mix_tail(o_ref[...], pz, x_ref[...], g1_ref[0], sh2_ref[0], sc2_ref[0], wout_ref, gpost_ref, gpre_ref,
              rwt_ref, rb_ref, x1_ref, h2_ref, idx_ref, w_ref, h2r_ref)


def _postmix_sample_kernel(o_ref, z_ref, st_ref, x_ref, g1_ref, sh2_ref, sc2_ref,
                           wout_ref, pw_ref, ps_ref, gpost_ref, gpre_ref, rwt_ref, rb_ref,
                           x1_ref, h2_ref, idx_ref, w_ref, h2r_ref, *, past_len):
    zt = z_ref[...]
    d_groups = []
    for g, w in enumerate(POOL_WINDOWS):
        cols = slice(g * POOL_GW, (g + 1) * POOL_GW)
        win = zt[:, cols]
        for s in range(1, w):
            win = win + st_ref[POOL_BUF - s][:, cols]
        d_groups.append(win / float(min(past_len + 1, w)) - zt[:, cols])
    pz = _pool_project(d_groups, pw_ref, ps_ref)
    _mix_tail(o_ref[...], pz, x_ref[...], g1_ref[0], sh2_ref[0], sc2_ref[0], wout_ref, gpost_ref, gpre_ref,
              rwt_ref, rb_ref, x1_ref, h2_ref, idx_ref, w_ref, h2r_ref)


def _postmix(o, z, hist, x, g1, sh2, sc2, wout_b, pool_w, pool_scale, g_post, g_pre, rwt, rb_col,
             *, tm, rows_per_mod, seq_tiles, past_len):
    t = x.shape[0]
    n_tiles = t // tm
    mod_rows = g1.shape[1]
    rows8 = D_MODEL // LANES
    mod_spec = pl.BlockSpec((1, mod_rows, D_MODEL), lambda i: (i // rows_per_mod, 0, 0))
    full = lambda shape: pl.BlockSpec(shape, lambda i: (0,) * len(shape))
    if past_len is None:
        body = functools.partial(_postmix_prompt_kernel, seq_tiles=seq_tiles)
        halo_blocks = tm // POOL_HALO
        hist_spec = pl.BlockSpec((POOL_HALO, POOL_W), lambda i: (jnp.maximum(i * halo_blocks - 1, 0), 0))
        scratch = [pltpu.VMEM((POOL_HALO + tm, POOL_W), F32)]
    else:
        body = functools.partial(_postmix_sample_kernel, past_len=past_len)
        hist_spec = full(hist.shape)
        scratch = []
    return pl.pallas_call(
        body,
        grid=(n_tiles,),
        in_specs=[pl.BlockSpec((tm, ATTN_W), lambda i: (i, 0)),
                  pl.BlockSpec((tm, POOL_W), lambda i: (i, 0)),
                  hist_spec,
                  pl.BlockSpec((tm, D_MODEL), lambda i: (i, 0)),
                  mod_spec, mod_spec, mod_spec,
                  full((D_MODEL, D_MODEL)), full(pool_w.shape), full((1, POOL_W)),
                  full((1, D_MODEL)), full((1, D_MODEL)), full((N_EXPERTS, D_MODEL)), full((N_EXPERTS, 1))],
        out_specs=[pl.BlockSpec((tm, D_MODEL), lambda i: (i, 0)),
                   pl.BlockSpec((tm, D_MODEL), lambda i: (i, 0)),
                   pl.BlockSpec((TOP_K, tm), lambda i: (0, i)), pl.BlockSpec((TOP_K, tm), lambda i: (0, i)),
                   pl.BlockSpec((1, tm * rows8, LANES), lambda i: (i // seq_tiles, i % seq_tiles, 0))],
        out_shape=[jax.ShapeDtypeStruct((t, D_MODEL), F32), jax.ShapeDtypeStruct((t, D_MODEL), BF16),
                   jax.ShapeDtypeStruct((TOP_K, t), jnp.int32), jax.ShapeDtypeStruct((TOP_K, t), F32),
                   jax.ShapeDtypeStruct((n_tiles // seq_tiles, seq_tiles * tm * rows8, LANES), F32)],
        scratch_shapes=scratch,
        compiler_params=pltpu.CompilerParams(dimension_semantics=("parallel",), vmem_limit_bytes=VMEM_LIMIT),
        name="postmix",
    )(o, z, hist, x, g1, sh2, sc2, wout_b, pool_w, pool_scale, g_post, g_pre, rwt, rb_col)


def _swiglu(xb, wg, wu):
    return _silu(_dot(xb, wg.astype(BF16))) * _dot(xb, wu.astype(BF16))


def _list_window(range_slots, bm):
    return (-(-range_slots // bm) + 2) * bm


def _dispatch_tables(idx_p, w_p, idx_x, w_x, n_ranges, rtok, bm):
    k, extra = idx_x.shape
    slots = rtok + extra
    per_p, per_x = k * rtok, k * extra
    i32 = jnp.int32
    e_p = idx_p.reshape(k, n_ranges, rtok).transpose(1, 0, 2).reshape(n_ranges, per_p)
    keys_p = jnp.sort(e_p * per_p + jnp.arange(per_p, dtype=i32), axis=1)
    keys_x = jnp.sort(idx_x.reshape(per_x) * per_x + jnp.arange(per_x, dtype=i32))
    experts1 = jnp.arange(N_EXPERTS + 1, dtype=i32)
    below_p = jnp.sum((keys_p[:, :, None] < experts1 * per_p).astype(i32), axis=1)
    below_x = jnp.sum((keys_x[:, None] < experts1 * per_x).astype(i32), axis=0)
    below_x = jnp.where((jnp.arange(n_ranges) == n_ranges - 1)[:, None], below_x[None, :], 0)
    start_p, start_x = below_p[:, :N_EXPERTS], below_x[:, :N_EXPERTS]
    n_sub = (below_p[:, 1:] - start_p + below_x[:, 1:] - start_x + bm - 1) // bm
    first_sub = jnp.cumsum(n_sub, axis=1) - n_sub
    tot_sub = -(-(per_p + per_x) // bm) + N_EXPERTS + _list_window(slots, bm) // bm
    sub = jnp.arange(tot_sub, dtype=i32)
    begun = sub[None, :, None] >= first_sub[:, None, :]
    at_begun = lambda a: jnp.max(jnp.where(begun, a[:, None, :], 0), axis=2)
    at_next = lambda a, end: jnp.min(jnp.where(begun, end, a[:, None, :]), axis=2)
    lo_p, hi_p = at_begun(start_p), at_next(start_p, per_p)
    lo_x, hi_x = at_begun(start_x), at_next(start_x, below_x[:, N_EXPERTS:, None])
    pos = ((sub - at_begun(first_sub)) * bm)[:, :, None] + jnp.arange(bm, dtype=i32)
    n_p = (hi_p - lo_p)[:, :, None]
    flat = lambda a: a.reshape(n_ranges, tot_sub * bm)
    in_p = flat(pos < n_p)
    valid = flat(pos < n_p + (hi_x - lo_x)[:, :, None])
    src = flat(jnp.where(pos < n_p, lo_p[:, :, None] + pos, per_p + lo_x[:, :, None] + pos - n_p))
    keys = jnp.concatenate([keys_p, jnp.broadcast_to(keys_x, (n_ranges, per_x))], axis=1)
    key = jnp.take_along_axis(keys, jnp.clip(src, 0, per_p + per_x - 1), axis=1)
    slot = jnp.where(in_p, key % per_p, key % per_x)
    tok = jnp.where(in_p, slot % rtok, rtok + slot % extra)
    w = jnp.concatenate([w_p.reshape(k, n_ranges, rtok).transpose(1, 0, 2).reshape(n_ranges, per_p),
                         jnp.broadcast_to(w_x.reshape(per_x), (n_ranges, per_x))], axis=1)
    wl = jnp.where(valid, jnp.take_along_axis(w, jnp.where(in_p, slot, per_p + slot), axis=1), 0.0)
    spare = slots + jnp.arange(tot_sub * bm, dtype=i32) % bm
    src8 = jnp.where(valid, tok, 0) * 8
    dst8 = jnp.where(valid, tok, spare) * 8
    g = n_ranges * N_EXPERTS
    return first_sub.reshape(g), n_sub.reshape(g), src8[:, None, :], dst8[:, None, :], wl[:, None, :]


def _moe_sorted_kernel(first_ref, nsub_ref, src_ref, tok_ref, w_ref, h_hbm, hx_hbm, wg_ref, wu_ref, wd_ref,
                       acc_ref, hbuf, sem, wg_b, wu_b, wd_b, xg, ys, *, bm):
    r = pl.program_id(0)
    e = pl.program_id(1)
    n_chunks = D_MODEL // LANES
    n_blocks = nsub_ref[r * N_EXPERTS + e]
    range_rows = h_hbm.shape[1]

    @pl.when(e == 0)
    def _():
        main = pltpu.make_async_copy(h_hbm.at[r], hbuf.at[pl.ds(0, range_rows), :], sem.at[0])
        main.start()

        @pl.when(r == pl.num_programs(0) - 1)
        def _():
            extra = pltpu.make_async_copy(hx_hbm.at[0], hbuf.at[pl.ds(range_rows, hx_hbm.shape[1]), :], sem.at[1])
            extra.start()
            extra.wait()

        acc_ref[...] = jnp.zeros(acc_ref.shape, F32)
        main.wait()

    def gather(q0):
        for rr in range(bm):
            t8 = pl.multiple_of(src_ref[0, 0, q0 + rr], n_chunks)
            xg[rr * n_chunks:(rr + 1) * n_chunks, :] = hbuf[pl.ds(t8, n_chunks), :]

    def scatter(q0):
        for b0 in range(0, bm, SCATTER_BATCH):
            rows = range(b0, b0 + SCATTER_BATCH)
            toks = [pl.multiple_of(tok_ref[0, 0, q0 + rr], n_chunks) for rr in rows]
            wts = [w_ref[0, 0, q0 + rr] for rr in rows]
            olds = [acc_ref[0, pl.ds(t8, n_chunks), :] for t8 in toks]
            news = [o + wv * ys[rr * n_chunks:(rr + 1) * n_chunks, :] for o, wv, rr in zip(olds, wts, rows)]
            for t8, nv in zip(toks, news):
                acc_ref[0, pl.ds(t8, n_chunks), :] = nv

    def block(b, scatter_previous):
        x = jnp.concatenate([xg[pl.ds(j, bm, stride=n_chunks), :] for j in range(n_chunks)], axis=1).astype(BF16)
        gather((b + 1) * bm)
        if scatter_previous:
            scatter((b - 1) * bm)
        hh = _silu(_dot(x, wg_b[...])) * _dot(x, wu_b[...])
        y = _dot(hh.astype(BF16), wd_b[...])
        for i in range(bm // 8):
            for j in range(n_chunks):
                ys[pl.ds(i * 8 * n_chunks + j, 8, stride=n_chunks), :] = y[8 * i:8 * i + 8, j * LANES:(j + 1) * LANES]

    @pl.when(n_blocks > 0)
    def _():
        wg_b[...] = wg_ref[0].astype(BF16)
        wu_b[...] = wu_ref[0].astype(BF16)
        wd_b[...] = wd_ref[0].astype(BF16)
        gather(0)
        block(0, False)

        def body(b, carry):
            block(b, True)
            return carry

        lax.fori_loop(1, n_blocks, body, 0)
        scatter((n_blocks - 1) * bm)


def _moe_sorted(first_sub, n_sub, src8, dst8, wl, h2r, h2r_x, wg, wu, wd, *, bm):
    n_ranges, rows, _ = h2r.shape
    n_chunks = D_MODEL // LANES
    rtok = (rows + h2r_x.shape[1]) // n_chunks
    cap = _list_window(rtok, bm)
    one = pl.Element(1)
    window = lambda r, e, first, nsub: (r, 0, first[r * N_EXPERTS + e] * bm)
    list_spec = pl.BlockSpec((one, one, pl.Element(cap)), window, memory_space=pltpu.SMEM)
    grid_spec = pltpu.PrefetchScalarGridSpec(
        num_scalar_prefetch=2,
        grid=(n_ranges, N_EXPERTS),
        in_specs=[list_spec, list_spec, list_spec,
                  pl.BlockSpec(memory_space=pl.ANY), pl.BlockSpec(memory_space=pl.ANY),
                  pl.BlockSpec((1, D_MODEL, EXPERT_DIM), lambda r, e, *_: (e, 0, 0)),
                  pl.BlockSpec((1, D_MODEL, EXPERT_DIM), lambda r, e, *_: (e, 0, 0)),
                  pl.BlockSpec((1, EXPERT_DIM, D_MODEL), lambda r, e, *_: (e, 0, 0))],
        out_specs=pl.BlockSpec((1, (rtok + bm) * n_chunks, LANES), lambda r, e, *_: (r, 0, 0),
                               pipeline_mode=pl.Buffered(1)),
        scratch_shapes=[pltpu.VMEM((rtok * n_chunks, LANES), F32), pltpu.SemaphoreType.DMA((2,)),
                        pltpu.VMEM((D_MODEL, EXPERT_DIM), BF16), pltpu.VMEM((D_MODEL, EXPERT_DIM), BF16),
                        pltpu.VMEM((EXPERT_DIM, D_MODEL), BF16),
                        pltpu.VMEM((bm * n_chunks, LANES), F32), pltpu.VMEM((bm * n_chunks, LANES), F32)])
    return pl.pallas_call(
        functools.partial(_moe_sorted_kernel, bm=bm),
        grid_spec=grid_spec,
        out_shape=jax.ShapeDtypeStruct((n_ranges, (rtok + bm) * n_chunks, LANES), F32),
        compiler_params=pltpu.CompilerParams(dimension_semantics=("arbitrary", "arbitrary"),
                                             vmem_limit_bytes=VMEM_LIMIT),
        name="moe_grouped",
    )(first_sub, n_sub, src8, dst8, wl, h2r, h2r_x, wg, wu, wd)


def _final_grouped_kernel(x1_ref, fr_ref, h_ref, sg_ref, su_ref, sd_ref, g2_ref, gain_ref, y_ref):
    shared = _dot(_swiglu(h_ref[...], sg_ref[...], su_ref[...]).astype(BF16), sd_ref[...].astype(BF16))
    f = _from_token_rows(fr_ref, x1_ref.shape[0]) + shared
    y_ref[...] = x1_ref[...] + g2_ref[0] * _rmsnorm(f, gain_ref[...])


def _final_grouped(x1, fr, h2b, sg, su, sd, g2, gain, *, tm, fr_index, mod_index):
    t = x1.shape[0]
    n_chunks = D_MODEL // LANES
    row = pl.BlockSpec((tm, D_MODEL), lambda i: (i, 0))
    full = lambda shape: pl.BlockSpec(shape, lambda i: (0,) * len(shape))
    return pl.pallas_call(
        _final_grouped_kernel,
        grid=(t // tm,),
        in_specs=[row,
                  pl.BlockSpec((1, tm * n_chunks, LANES), lambda i: (*fr_index(i), 0)),
                  row, full(sg.shape), full(su.shape), full(sd.shape),
                  pl.BlockSpec((1, g2.shape[1], D_MODEL), lambda i: (mod_index(i), 0, 0)),
                  pl.BlockSpec((1, D_MODEL), lambda i: (0, 0))],
        out_specs=row,
        out_shape=jax.ShapeDtypeStruct((t, D_MODEL), F32),
        compiler_params=pltpu.CompilerParams(dimension_semantics=("parallel",), vmem_limit_bytes=VMEM_LIMIT),
        name="final_grouped",
    )(x1, fr, h2b, sg, su, sd, g2, gain)


def _rope_tables(pos):
    half = HEAD_DIM // 2
    inv_freq = ROPE_THETA ** (-jnp.arange(half, dtype=F32) / half)
    ang = pos.astype(F32)[:, None] * inv_freq[None, :]
    cos, sin = jnp.cos(ang), jnp.sin(ang)
    reps = LANES // HEAD_DIM
    return jnp.tile(jnp.concatenate([cos, cos], axis=1), (1, reps)), jnp.tile(jnp.concatenate([-sin, sin], axis=1), (1, reps))


def kernel(x_prompt, x_sample, c_prompt, c_sample, cache_k, cache_v, state_pool, page_table, ada_w, ada_b, pre_mix_g, post_mix_g, pre_ffn_g, post_ffn_g, w_in, w_out, lambda_q1, lambda_k1, lambda_q2, lambda_k2, subln_g, pool_w, pool_scale, router_w, router_bias, expert_w_gate, expert_w_up, expert_w_down, shared_w_gate, shared_w_up, shared_w_down):
    depth = ada_w.shape[0]
    assert depth == 1, "single-layer step"
    nb, seq, _ = x_prompt.shape
    db, dec_seq, _ = x_sample.shape
    assert dec_seq == 1
    page_size = cache_k.shape[2]
    n_pages = page_table.shape[1]
    past_len = n_pages * page_size
    lam_init = 0.8 - 0.6 * math.exp(-0.3 * 0)
    l = 0

    c_all = jnp.concatenate([c_prompt, c_sample], axis=0)
    pad = (-c_all.shape[0]) % 16
    mod = _ada(jnp.pad(c_all, ((0, pad), (0, 0))), ada_w[l], ada_b[l][None, :])
    mod_p = [m.reshape(nb, 1, D_MODEL) for m in jnp.split(mod[:nb], 6, axis=-1)]
    mod_s = [m.reshape(1, db, D_MODEL) for m in jnp.split(mod[nb:nb + db], 6, axis=-1)]

    w_in_b = w_in[l].astype(BF16)
    w_out_b = w_out[l].astype(BF16)
    lam4 = jnp.stack([lambda_q1[l], lambda_k1[l], lambda_q2[l], lambda_k2[l]])
    row = lambda v: v.reshape(1, -1)
    rwt = router_w[l].T
    rb_col = router_bias[l].reshape(N_EXPERTS, 1)
    experts = (expert_w_gate[l], expert_w_up[l], expert_w_down[l], shared_w_gate[l], shared_w_up[l], shared_w_down[l])

    t_p = nb * seq
    tm = TM_TOKEN
    seq_tiles = seq // tm
    cos_p, sin_p = _rope_tables(jnp.arange(seq))
    xp = x_prompt.reshape(t_p, D_MODEL)
    k_p, v_p, z_p, qb, kb, vt = _premix(xp, mod_p[0], mod_p[1], row(pre_mix_g[l]), w_in_b, cos_p, sin_p,
                                        tm=tm, rows_per_mod=seq_tiles, seq_tiles=seq_tiles, tk=TK_ATTN)
    o_p = _attn_prompt(qb.reshape(nb, seq, ATTN_W), kb.reshape(nb, seq, ATTN_W), vt, lam4,
                       subln_g[l].reshape(V_DIM, 1), tq=TQ_ATTN, tk=TK_ATTN, hps=HEADS_PER_STEP,
                       lam_init=lam_init)
    x1_p, h2_p, idx_p, w_p, h2r = _postmix(o_p.reshape(t_p, ATTN_W), z_p, z_p, xp, mod_p[2], mod_p[3], mod_p[4],
                                           w_out_b, pool_w[l], row(pool_scale[l]), row(post_mix_g[l]),
                                           row(pre_ffn_g[l]), rwt, rb_col,
                                           tm=tm, rows_per_mod=seq_tiles, seq_tiles=seq_tiles, past_len=None)

    cos_s, sin_s = _rope_tables(jnp.full((db,), past_len))
    xs = x_sample.reshape(db, D_MODEL)
    k_s, v_s, z_s, qb_s, kb_s = _premix(xs, mod_s[0], mod_s[1], row(pre_mix_g[l]), w_in_b, cos_s, sin_s,
                                        tm=db, rows_per_mod=1, seq_tiles=1, tk=0)
    page_rows = page_size * N_HEADS
    k2 = cache_k[l].reshape(-1, V_DIM)
    v2 = cache_v[l].reshape(-1, V_DIM)
    o_s = _attn_decode(page_table, qb_s.reshape(db, N_HEADS, V_DIM), kb_s.reshape(db, N_HEADS, V_DIM),
                       v_s.reshape(db, N_HEADS, V_DIM), lam4, row(subln_g[l]), k2, v2,
                       page_rows=page_rows, lam_init=lam_init)
    hist_s = jnp.transpose(state_pool[l], (1, 0, 2))
    x1_s, h2_s, idx_s, w_s, h2r_s = _postmix(o_s.reshape(db, ATTN_W), z_s, hist_s, xs, mod_s[2], mod_s[3], mod_s[4],
                                             w_out_b, pool_w[l], row(pool_scale[l]), row(post_mix_g[l]),
                                             row(pre_ffn_g[l]), rwt, rb_col,
                                             tm=db, rows_per_mod=1, seq_tiles=1, past_len=past_len)

    first_sub, n_sub, src8, dst8, wl = _dispatch_tables(idx_p, w_p, idx_s, w_s, nb, seq, MOE_BM)
    fr = _moe_sorted(first_sub, n_sub, src8, dst8, wl, h2r, h2r_s, *experts[:3], bm=MOE_BM)
    final_tiles = seq // TM_FINAL
    y_p = _final_grouped(x1_p, fr, h2_p, *experts[3:], mod_p[5], row(post_ffn_g[l]), tm=TM_FINAL,
                         fr_index=lambda i: (i // final_tiles, i % final_tiles), mod_index=lambda i: i // final_tiles)
    y_s = _final_grouped(x1_s, fr, h2_s, *experts[3:], mod_s[5], row(post_ffn_g[l]), tm=db,
                         fr_index=lambda i: (nb - 1, seq // db), mod_index=lambda i: 0)

    pool_p = z_p.reshape(nb, seq, POOL_W)[:, seq - POOL_BUF:]
    pool_s = jnp.concatenate([state_pool[l][:, 1:], z_s[:, None, :]], axis=1)
    return (y_p.reshape(nb, seq, D_MODEL), y_s.reshape(db, 1, D_MODEL),
            k_p.reshape(1, nb, seq, N_HEADS, V_DIM), v_p.reshape(1, nb, seq, N_HEADS, V_DIM), pool_p[None],
            k_s.reshape(1, db, 1, N_HEADS, V_DIM), v_s.reshape(1, db, 1, N_HEADS, V_DIM), pool_s[None])
```

```python
import functools
import math

import jax
import jax.numpy as jnp
from jax import lax
from jax.experimental import pallas as pl
from jax.experimental.pallas import tpu as pltpu

F32 = jnp.float32
BF16 = jnp.bfloat16

D_MODEL = 1024
ATTN_W = 512
POOL_W = 512
HEAD_DIM = 64
N_HEADS = 4
V_DIM = 2 * HEAD_DIM
IN_COLS = 3 * ATTN_W + POOL_W
POOL_WINDOWS = (2, 4, 8, 16)
POOL_GW = 128
POOL_BUF = 15
POOL_HALO = 16
ROPE_THETA = 10000.0
N_EXPERTS = 64
TOP_K = 8
N_EXPERT_GROUPS = 8
GROUP_SIZE = N_EXPERTS // N_EXPERT_GROUPS
TOPK_GROUPS = 4
EXPERT_DIM = 256
ROUTED_SCALE = 2.5
EPS = 1e-6
LANES = 128
NEG_INF = float("-inf")
Q_SCALE = HEAD_DIM ** -0.5 * math.log2(math.e)
VMEM_LIMIT = 56 * 1024 * 1024

TM_TOKEN = 512
TQ_ATTN = 512
TK_ATTN = 256
HEADS_PER_STEP = 4
MOE_BM = 128
SCATTER_BATCH = 8
TM_FINAL = 512
DEC_CHUNK_PAGES = 32


def _silu(x):
    return x * jax.nn.sigmoid(x)


def _rmsnorm(x, g):
    ms = jnp.mean(x * x, axis=-1, keepdims=True)
    return x * lax.rsqrt(ms + EPS) * g


def _dot(a, b):
    return jnp.dot(a, b, preferred_element_type=F32)


def _dot_nt(a, b):
    return lax.dot_general(a, b, (((1,), (1,)), ((), ())), preferred_element_type=F32)


def _ada_kernel(c_ref, w_ref, b_ref, o_ref):
    s = _silu(c_ref[...])
    o_ref[...] = _dot(s.astype(BF16), w_ref[...].astype(BF16)) + b_ref[...]


def _ada(c_pad, w, b):
    rows, n = c_pad.shape[0], w.shape[1]
    tn = 1536
    return pl.pallas_call(
        _ada_kernel,
        grid=(n // tn,),
        in_specs=[pl.BlockSpec((rows, D_MODEL), lambda j: (0, 0)),
                  pl.BlockSpec((D_MODEL, tn), lambda j: (0, j)),
                  pl.BlockSpec((1, tn), lambda j: (0, j))],
        out_specs=pl.BlockSpec((rows, tn), lambda j: (0, j)),
        out_shape=jax.ShapeDtypeStruct((rows, n), F32),
        compiler_params=pltpu.CompilerParams(dimension_semantics=("arbitrary",), vmem_limit_bytes=VMEM_LIMIT),
        name="ada_mod",
    )(c_pad, w, b)


def _rope(x, cos, sin_signed):
    outs = []
    for c in range(x.shape[1] // LANES):
        blk = x[:, c * LANES:(c + 1) * LANES]
        lane = lax.broadcasted_iota(jnp.int32, blk.shape, 1)
        partner = jnp.where((lane % HEAD_DIM) < HEAD_DIM // 2,
                            pltpu.roll(blk, LANES - HEAD_DIM // 2, 1),
                            pltpu.roll(blk, HEAD_DIM // 2, 1))
        outs.append(blk * cos + partner * sin_signed)
    return jnp.concatenate(outs, axis=1)


def _premix_kernel(x_ref, sh_ref, sc_ref, g_ref, w_ref, cos_ref, sin_ref,
                   k_ref, v_ref, z_ref, qb_ref, kb_ref, *rest, n_kv):
    h = _rmsnorm(x_ref[...], g_ref[...]) * (1.0 + sc_ref[0]) + sh_ref[0]
    proj = _dot(h.astype(BF16), w_ref[...])
    cos, sin = cos_ref[...], sin_ref[...]
    q = _rope(proj[:, :ATTN_W], cos, sin)
    k = _rope(proj[:, ATTN_W:2 * ATTN_W], cos, sin)
    v = proj[:, 2 * ATTN_W:3 * ATTN_W]
    for out_ref, val in ((k_ref, k), (v_ref, v)):
        for i in range(val.shape[0] // 8):
            for hh in range(N_HEADS):
                out_ref[pl.ds(i * 8 * N_HEADS + hh, 8, stride=N_HEADS), :] = val[8 * i:8 * i + 8, hh * V_DIM:(hh + 1) * V_DIM]
    z_ref[...] = proj[:, 3 * ATTN_W:]
    qb_ref[...] = (q * Q_SCALE).astype(BF16)
    kb_ref[...] = k.astype(BF16)
    if n_kv:
        vt_ref = rest[0]
        vt = v.T.astype(BF16)
        tk = vt.shape[1] // n_kv
        for hh in range(N_HEADS):
            for c in range(n_kv):
                vt_ref[0, hh, c] = vt[hh * V_DIM:(hh + 1) * V_DIM, c * tk:(c + 1) * tk]


def _premix(x, shift, scale, gain, w_in_b, cos, sin, *, tm, rows_per_mod, seq_tiles, tk):
    t = x.shape[0]
    n_tiles = t // tm
    mod_rows = shift.shape[1]
    mod_idx = (lambda i: (i // rows_per_mod, 0, 0))
    n_kv = tm // tk if tk else 0
    out_shape = [jax.ShapeDtypeStruct((t * N_HEADS, V_DIM), F32), jax.ShapeDtypeStruct((t * N_HEADS, V_DIM), F32),
                 jax.ShapeDtypeStruct((t, POOL_W), F32), jax.ShapeDtypeStruct((t, ATTN_W), BF16),
                 jax.ShapeDtypeStruct((t, ATTN_W), BF16)]
    row_spec = pl.BlockSpec((tm, ATTN_W), lambda i: (i, 0))
    head_rows = pl.BlockSpec((tm * N_HEADS, V_DIM), lambda i: (i, 0))
    out_specs = [head_rows, head_rows, row_spec, row_spec, row_spec]
    if n_kv:
        nb = n_tiles // seq_tiles
        out_shape.append(jax.ShapeDtypeStruct((nb, N_HEADS, seq_tiles * n_kv, V_DIM, tk), BF16))
        out_specs.append(pl.BlockSpec((1, N_HEADS, n_kv, V_DIM, tk),
                                      lambda i: (i // seq_tiles, 0, i % seq_tiles, 0, 0)))
    return pl.pallas_call(
        functools.partial(_premix_kernel, n_kv=n_kv),
        grid=(n_tiles,),
        in_specs=[pl.BlockSpec((tm, D_MODEL), lambda i: (i, 0)),
                  pl.BlockSpec((1, mod_rows, D_MODEL), mod_idx),
                  pl.BlockSpec((1, mod_rows, D_MODEL), mod_idx),
                  pl.BlockSpec((1, D_MODEL), lambda i: (0, 0)),
                  pl.BlockSpec((D_MODEL, IN_COLS), lambda i: (0, 0)),
                  pl.BlockSpec((tm, LANES), lambda i: (i % seq_tiles, 0)),
                  pl.BlockSpec((tm, LANES), lambda i: (i % seq_tiles, 0))],
        out_specs=out_specs,
        out_shape=out_shape,
        compiler_params=pltpu.CompilerParams(dimension_semantics=("parallel",), vmem_limit_bytes=VMEM_LIMIT),
        name="premix",
    )(x, shift, scale, gain, w_in_b, cos, sin)


def _diff_lambda(lam_ref, lam_init):
    lam = lam_ref[...]
    s1 = jnp.sum(lam[0:1] * lam[1:2], axis=1, keepdims=True)
    s2 = jnp.sum(lam[2:3] * lam[3:4], axis=1, keepdims=True)
    return jnp.exp(s1) - jnp.exp(s2) + lam_init


def _split_maps(q):
    lane = lax.broadcasted_iota(jnp.int32, q.shape, 1)
    zero = jnp.zeros_like(q)
    return jnp.concatenate([jnp.where(lane < HEAD_DIM, q, zero), jnp.where(lane >= HEAD_DIM, q, zero)], axis=0)


def _attn_kernel(q_ref, k_ref, vt_ref, lam_ref, g_ref, o_ref, s_even, s_odd, acc_ref, *, tq, tk, hps, lam_init):
    i = pl.program_id(2)
    r = tq // tk
    heads = range(hps)
    cols = [slice(hh * V_DIM, (hh + 1) * V_DIM) for hh in heads]
    qpads = [_split_maps(q_ref[0, :, cols[hh]]) for hh in heads]

    def logits(j, buf):
        rows = pl.ds(pl.multiple_of(j * tk, tk), tk)
        for hh in heads:
            buf[hh] = _dot_nt(k_ref[0, rows, cols[hh]], qpads[hh])

    def consume(j, buf, carry, masked):
        def read(hh):
            s = buf[hh]
            if masked:
                kpos = j * tk + lax.broadcasted_iota(jnp.int32, s.shape, 0)
                qpos = i * tq + lax.broadcasted_iota(jnp.int32, s.shape, 1) % tq
                s = jnp.where(kpos <= qpos, s, NEG_INF)
            return s

        ps, alphas, out = [], [], []
        for hh in heads:
            m, l = carry[hh]
            m_new = jnp.maximum(m, jnp.max(read(hh), axis=0, keepdims=True))
            alpha = jnp.exp2(m - m_new)
            p = jnp.exp2(read(hh) - m_new)
            out.append((m_new, alpha * l + jnp.sum(p, axis=0, keepdims=True)))
            alphas.append(alpha)
            ps.append(p.astype(BF16))
        for hh in heads:
            acc_ref[hh] = alphas[hh] * acc_ref[hh] + _dot(vt_ref[0, hh, j], ps[hh])
        return tuple(out)

    def pair(p, carry):
        j = 2 * p
        logits(j + 1, s_odd)
        carry = consume(j, s_even, carry, False)
        logits(j + 2, s_even)
        return consume(j + 1, s_odd, carry, False)

    assert r == 2, "the written-out schedule handles two key blocks per query block"
    init = tuple((jnp.full((1, 2 * tq), -1e30, F32), jnp.zeros((1, 2 * tq), F32)) for _ in heads)
    acc_ref[...] = jnp.zeros(acc_ref.shape, F32)
    logits(0, s_even)
    carry = lax.fori_loop(0, i, pair, init)
    logits(2 * i + 1, s_odd)
    carry = consume(2 * i, s_even, carry, True)
    carry = consume(2 * i + 1, s_odd, carry, True)
    lam = _diff_lambda(lam_ref, lam_init)
    for hh in heads:
        o = acc_ref[hh] / carry[hh][1]
        od = o[:, :tq] - lam * o[:, tq:]
        ms = jnp.mean(od * od, axis=0, keepdims=True)
        on = od * lax.rsqrt(ms + EPS) * g_ref[...] * (1.0 - lam_init)
        o_ref[0, :, cols[hh]] = on.T.astype(BF16)


def _attn_prompt(qb, kb, vt, lam4, subln_col, *, tq, tk, hps, lam_init):
    b, s, _ = qb.shape
    nq, nkv = s // tq, s // tk
    w = hps * V_DIM
    return pl.pallas_call(
        functools.partial(_attn_kernel, tq=tq, tk=tk, hps=hps, lam_init=lam_init),
        grid=(b, N_HEADS // hps, nq),
        in_specs=[pl.BlockSpec((1, tq, w), lambda bb, h, i: (bb, i, h)),
                  pl.BlockSpec((1, s, w), lambda bb, h, i: (bb, 0, h)),
                  pl.BlockSpec((1, hps, nkv, V_DIM, tk), lambda bb, h, i: (bb, h, 0, 0, 0)),
                  pl.BlockSpec((4, HEAD_DIM), lambda bb, h, i: (0, 0)),
                  pl.BlockSpec((V_DIM, 1), lambda bb, h, i: (0, 0))],
        out_specs=pl.BlockSpec((1, tq, w), lambda bb, h, i: (bb, i, h)),
        out_shape=jax.ShapeDtypeStruct((b, s, ATTN_W), BF16),
        scratch_shapes=[pltpu.VMEM((hps, tk, 2 * tq), F32), pltpu.VMEM((hps, tk, 2 * tq), F32),
                        pltpu.VMEM((hps, V_DIM, 2 * tq), F32)],
        compiler_params=pltpu.CompilerParams(dimension_semantics=("parallel", "parallel", "arbitrary"),
                                             vmem_limit_bytes=VMEM_LIMIT),
        name="attn_prompt",
    )(qb, kb, vt, lam4, subln_col)


def _decode_kernel(pt_ref, q_ref, kn_ref, vn_ref, lam_ref, g_ref, k_hbm, v_hbm, o_ref,
                   kbuf, vbuf, sem, m_sc, l_sc, acc_sc, *, n_chunks, chunk_pages, page_rows, lam_init):
    b = pl.program_id(0)
    c = pl.program_id(1)
    nb = pl.num_programs(0)
    step = b * n_chunks + c
    slot = step % 2

    def chunk_copies(bb, cc, sl):
        copies = []
        for p in range(chunk_pages):
            phys = pt_ref[bb * (n_chunks * chunk_pages) + cc * chunk_pages + p]
            src = pl.ds(pl.multiple_of(phys * page_rows, page_rows), page_rows)
            dst = pl.ds(p * page_rows, page_rows)
            copies.append(pltpu.make_async_copy(k_hbm.at[src, :], kbuf.at[sl, dst, :], sem.at[0, sl]))
            copies.append(pltpu.make_async_copy(v_hbm.at[src, :], vbuf.at[sl, dst, :], sem.at[1, sl]))
        return copies

    @pl.when(step == 0)
    def _():
        for cp in chunk_copies(b, c, slot):
            cp.start()

    @pl.when(step + 1 < nb * n_chunks)
    def _():
        nxt = step + 1
        for cp in chunk_copies(nxt // n_chunks, nxt % n_chunks, 1 - slot):
            cp.start()

    @pl.when(c == 0)
    def _():
        m_sc[...] = jnp.full(m_sc.shape, -1e30, F32)
        l_sc[...] = jnp.zeros(l_sc.shape, F32)
        acc_sc[...] = jnp.zeros(acc_sc.shape, F32)

    qblk = _split_maps(q_ref[0])
    for cp in chunk_copies(b, c, slot):
        cp.wait()
    kc = kbuf[slot].astype(BF16)
    vc = vbuf[slot].astype(BF16)
    s = _dot_nt(qblk, kc)
    row_head = lax.broadcasted_iota(jnp.int32, s.shape, 0) % N_HEADS
    col_head = lax.broadcasted_iota(jnp.int32, s.shape, 1) % N_HEADS
    s = jnp.where(row_head == col_head, s, NEG_INF)
    m_old = m_sc[...]
    m_new = jnp.maximum(m_old, jnp.max(s, axis=1, keepdims=True))
    alpha = jnp.exp2(m_old - m_new)
    p = jnp.exp2(s - m_new)
    l_new = alpha * l_sc[...] + jnp.sum(p, axis=1, keepdims=True)
    acc_new = alpha * acc_sc[...] + _dot(p.astype(BF16), vc)
    m_sc[...] = m_new
    l_sc[...] = l_new
    acc_sc[...] = acc_new

    @pl.when(c == n_chunks - 1)
    def _():
        kn = jnp.concatenate([kn_ref[0], kn_ref[0]], axis=0).astype(F32)
        vn = jnp.concatenate([vn_ref[0], vn_ref[0]], axis=0).astype(F32)
        s_n = jnp.sum(qblk.astype(F32) * kn, axis=1, keepdims=True)
        m_f = jnp.maximum(m_new, s_n)
        a_f = jnp.exp2(m_new - m_f)
        p_n = jnp.exp2(s_n - m_f)
        l_f = a_f * l_new + p_n
        o = (a_f * acc_new + p_n * vn) / l_f
        od = o[:N_HEADS] - _diff_lambda(lam_ref, lam_init) * o[N_HEADS:]
        o_ref[0] = (_rmsnorm(od, g_ref[...]) * (1.0 - lam_init)).astype(BF16)


def _attn_decode(page_table, q3, kn3, vn3, lam4, subln_row, k2, v2, *, page_rows, lam_init):
    nb, n_pages = page_table.shape
    chunk_pages = DEC_CHUNK_PAGES
    n_chunks = n_pages // chunk_pages
    rows = chunk_pages * page_rows
    head_spec = pl.BlockSpec((1, N_HEADS, V_DIM), lambda b, c, pt: (b, 0, 0))
    grid_spec = pltpu.PrefetchScalarGridSpec(
        num_scalar_prefetch=1,
        grid=(nb, n_chunks),
        in_specs=[head_spec, head_spec, head_spec,
                  pl.BlockSpec((4, HEAD_DIM), lambda b, c, pt: (0, 0)),
                  pl.BlockSpec((1, V_DIM), lambda b, c, pt: (0, 0)),
                  pl.BlockSpec(memory_space=pl.ANY),
                  pl.BlockSpec(memory_space=pl.ANY)],
        out_specs=head_spec,
        scratch_shapes=[pltpu.VMEM((2, rows, V_DIM), F32), pltpu.VMEM((2, rows, V_DIM), F32),
                        pltpu.SemaphoreType.DMA((2, 2)),
                        pltpu.VMEM((2 * N_HEADS, 1), F32), pltpu.VMEM((2 * N_HEADS, 1), F32),
                        pltpu.VMEM((2 * N_HEADS, V_DIM), F32)])
    return pl.pallas_call(
        functools.partial(_decode_kernel, n_chunks=n_chunks, chunk_pages=chunk_pages,
                          page_rows=page_rows, lam_init=lam_init),
        grid_spec=grid_spec,
        out_shape=jax.ShapeDtypeStruct((nb, N_HEADS, V_DIM), BF16),
        compiler_params=pltpu.CompilerParams(dimension_semantics=("arbitrary", "arbitrary"),
                                             vmem_limit_bytes=VMEM_LIMIT),
        name="attn_decode",
    )(page_table.reshape(-1), q3, kn3, vn3, lam4, subln_row, k2, v2)


def _pool_project(d_groups, pw_ref, ps_ref):
    ys = [_dot(d.astype(BF16), pw_ref[g].astype(BF16)) for g, d in enumerate(d_groups)]
    return jnp.concatenate(ys, axis=1) * ps_ref[...]


def _argmax_first(x, iota, n):
    mx = jnp.max(x, axis=0, keepdims=True)
    ix = jnp.min(jnp.where(x == mx, iota, float(n)), axis=0, keepdims=True)
    return mx, ix


def _row_iota(shape):
    return lax.broadcasted_iota(jnp.int32, shape, 0).astype(F32)


def _route(h2, rwt_ref, rb_ref):
    rw = rwt_ref[...]
    rw_hi = rw.astype(BF16)
    rw_lo = (rw - rw_hi.astype(F32)).astype(BF16)
    h_hi = h2.astype(BF16)
    h_lo = (h2 - h_hi.astype(F32)).astype(BF16)
    logits = _dot_nt(rw_hi, h_hi) + (_dot_nt(rw_hi, h_lo) + _dot_nt(rw_lo, h_hi))
    scores = jax.nn.sigmoid(logits)
    sel = scores + rb_ref[...]
    tm = sel.shape[1]
    iota_g = _row_iota((GROUP_SIZE, tm))
    group_rows = []
    for g in range(N_EXPERT_GROUPS):
        xg = sel[g * GROUP_SIZE:(g + 1) * GROUP_SIZE]
        m1, i1 = _argmax_first(xg, iota_g, GROUP_SIZE)
        m2 = jnp.max(jnp.where(iota_g == i1, NEG_INF, xg), axis=0, keepdims=True)
        group_rows.append(m1 + m2)
    gscore = jnp.concatenate(group_rows, axis=0)
    iota_gr = _row_iota(gscore.shape)
    gmask = jnp.zeros(gscore.shape, F32)
    for _ in range(TOPK_GROUPS):
        _, ig = _argmax_first(gscore, iota_gr, N_EXPERT_GROUPS)
        hit = iota_gr == ig
        gmask = jnp.where(hit, 1.0, gmask)
        gscore = jnp.where(hit, NEG_INF, gscore)
    masked = jnp.concatenate(
        [jnp.where(gmask[g:g + 1] > 0.5, sel[g * GROUP_SIZE:(g + 1) * GROUP_SIZE], NEG_INF)
         for g in range(N_EXPERT_GROUPS)], axis=0)
    iota_e = _row_iota(masked.shape)
    w = jnp.zeros(masked.shape, F32)
    chosen = jnp.zeros(masked.shape, F32)
    idx_rows, score_rows = [], []
    for _ in range(TOP_K):
        _, ie = _argmax_first(masked, iota_e, N_EXPERTS)
        hit = iota_e == ie
        w = jnp.where(hit, scores, w)
        chosen = jnp.where(hit, 1.0, chosen)
        masked = jnp.where(hit, NEG_INF, masked)
        idx_rows.append(ie)
        score_rows.append(jnp.sum(jnp.where(hit, scores, 0.0), axis=0, keepdims=True))
    wsum = jnp.sum(w, axis=0, keepdims=True)
    idx_t = jnp.concatenate(idx_rows, axis=0).astype(jnp.int32)
    w_t = jnp.concatenate(score_rows, axis=0) / wsum * ROUTED_SCALE
    return idx_t, w_t, jnp.sum(chosen, axis=1, keepdims=True)


def _to_token_rows(x, out_ref):
    n_chunks = x.shape[1] // LANES
    for i in range(x.shape[0] // 8):
        for j in range(n_chunks):
            out_ref[0, pl.ds(i * 8 * n_chunks + j, 8, stride=n_chunks), :] = x[8 * i:8 * i + 8, j * LANES:(j + 1) * LANES]


def _from_token_rows(in_ref, tm):
    n_chunks = D_MODEL // LANES
    return jnp.concatenate([in_ref[0, pl.ds(j, tm, stride=n_chunks), :] for j in range(n_chunks)], axis=1)


def _mix_tail(o_b, pz, x, g1, sh2, sc2, wout_ref, gpost_ref, gpre_ref, rwt_ref, rb_ref,
              x1_ref, h2_ref, idx_ref, w_ref, h2r_ref, cnt_ref):
    m = _dot(o_b, wout_ref[:ATTN_W, :]) + _dot(pz.astype(BF16), wout_ref[ATTN_W:, :])
    x1 = x + g1 * _rmsnorm(m, gpost_ref[...])
    h2 = _rmsnorm(x1, gpre_ref[...]) * (1.0 + sc2) + sh2
    x1_ref[...] = x1
    h2_ref[...] = h2.astype(BF16)
    idx_ref[...], w_ref[...], cnt_ref[0] = _route(h2, rwt_ref, rb_ref)
    _to_token_rows(h2, h2r_ref)


def _postmix_prompt_kernel(o_ref, z_ref, zprev_ref, x_ref, g1_ref, sh2_ref, sc2_ref,
                           wout_ref, pw_ref, ps_ref, gpost_ref, gpre_ref, rwt_ref, rb_ref,
                           x1_ref, h2_ref, idx_ref, w_ref, h2r_ref, cnt_ref, zbuf, *, seq_tiles):
    tm = z_ref.shape[0]
    it = pl.program_id(0) % seq_tiles
    zt = z_ref[...]
    zbuf[0:POOL_HALO, :] = jnp.where(it == 0, 0.0, zprev_ref[...])
    zbuf[POOL_HALO:, :] = zt
    pos = it * tm + lax.broadcasted_iota(jnp.int32, (tm, 1), 0)
    d_groups = []
    for g, w in enumerate(POOL_WINDOWS):
        cols = slice(g * POOL_GW, (g + 1) * POOL_GW)
        win = zt[:, cols]
        for s in range(1, w):
            win = win + zbuf[POOL_HALO - s:POOL_HALO - s + tm, cols]
        cnt = jnp.minimum(pos + 1, w).astype(F32)
        d_groups.append(win / cnt - zt[:, cols])
    pz = _pool_project(d_groups, pw_ref, ps_ref)
    _mix_tail(o_ref[...], pz, x_ref[...], g1_ref[0], sh2_ref[0], sc2_ref[0], wout_ref, gpost_ref, gpre_ref,
              rwt_ref, rb_ref, x1_ref, h2_ref, idx_ref, w_ref, h2r_ref, cnt_ref)


def _postmix_sample_kernel(o_ref, z_ref, st_ref, x_ref, g1_ref, sh2_ref, sc2_ref,
                           wout_ref, pw_ref, ps_ref, gpost_ref, gpre_ref, rwt_ref, rb_ref,
                           x1_ref, h2_ref, idx_ref, w_ref, h2r_ref, cnt_ref, *, past_len):
    zt = z_ref[...]
    d_groups = []
    for g, w in enumerate(POOL_WINDOWS):
        cols = slice(g * POOL_GW, (g + 1) * POOL_GW)
        win = zt[:, cols]
        for s in range(1, w):
            win = win + st_ref[POOL_BUF - s][:, cols]
        d_groups.append(win / float(min(past_len + 1, w)) - zt[:, cols])
    pz = _pool_project(d_groups, pw_ref, ps_ref)
    _mix_tail(o_ref[...], pz, x_ref[...], g1_ref[0], sh2_ref[0], sc2_ref[0], wout_ref, gpost_ref, gpre_ref,
              rwt_ref, rb_ref, x1_ref, h2_ref, idx_ref, w_ref, h2r_ref, cnt_ref)


def _postmix(o, z, hist, x, g1, sh2, sc2, wout_b, pool_w, pool_scale, g_post, g_pre, rwt, rb_col,
             *, tm, rows_per_mod, seq_tiles, past_len):
    t = x.shape[0]
    n_tiles = t // tm
    mod_rows = g1.shape[1]
    rows8 = D_MODEL // LANES
    mod_spec = pl.BlockSpec((1, mod_rows, D_MODEL), lambda i: (i // rows_per_mod, 0, 0))
    full = lambda shape: pl.BlockSpec(shape, lambda i: (0,) * len(shape))
    if past_len is None:
        body = functools.partial(_postmix_prompt_kernel, seq_tiles=seq_tiles)
        halo_blocks = tm // POOL_HALO
        hist_spec = pl.BlockSpec((POOL_HALO, POOL_W), lambda i: (jnp.maximum(i * halo_blocks - 1, 0), 0))
        scratch = [pltpu.VMEM((POOL_HALO + tm, POOL_W), F32)]
    else:
        body = functools.partial(_postmix_sample_kernel, past_len=past_len)
        hist_spec = full(hist.shape)
        scratch = []
    return pl.pallas_call(
        body,
        grid=(n_tiles,),
        in_specs=[pl.BlockSpec((tm, ATTN_W), lambda i: (i, 0)),
                  pl.BlockSpec((tm, POOL_W), lambda i: (i, 0)),
                  hist_spec,
                  pl.BlockSpec((tm, D_MODEL), lambda i: (i, 0)),
                  mod_spec, mod_spec, mod_spec,
                  full((D_MODEL, D_MODEL)), full(pool_w.shape), full((1, POOL_W)),
                  full((1, D_MODEL)), full((1, D_MODEL)), full((N_EXPERTS, D_MODEL)), full((N_EXPERTS, 1))],
        out_specs=[pl.BlockSpec((tm, D_MODEL), lambda i: (i, 0)),
                   pl.BlockSpec((tm, D_MODEL), lambda i: (i, 0)),
                   pl.BlockSpec((TOP_K, tm), lambda i: (0, i)), pl.BlockSpec((TOP_K, tm), lambda i: (0, i)),
                   pl.BlockSpec((1, tm * rows8, LANES), lambda i: (i // seq_tiles, i % seq_tiles, 0)),
                   pl.BlockSpec((1, N_EXPERTS, 1), lambda i: (i, 0, 0))],
        out_shape=[jax.ShapeDtypeStruct((t, D_MODEL), F32), jax.ShapeDtypeStruct((t, D_MODEL), BF16),
                   jax.ShapeDtypeStruct((TOP_K, t), jnp.int32), jax.ShapeDtypeStruct((TOP_K, t), F32),
                   jax.ShapeDtypeStruct((n_tiles // seq_tiles, seq_tiles * tm * rows8, LANES), F32),
                   jax.ShapeDtypeStruct((n_tiles, N_EXPERTS, 1), F32)],
        scratch_shapes=scratch,
        compiler_params=pltpu.CompilerParams(dimension_semantics=("parallel",), vmem_limit_bytes=VMEM_LIMIT),
        name="postmix",
    )(o, z, hist, x, g1, sh2, sc2, wout_b, pool_w, pool_scale, g_post, g_pre, rwt, rb_col)


def _swiglu(xb, wg, wu):
    return _silu(_dot(xb, wg.astype(BF16))) * _dot(xb, wu.astype(BF16))


def _list_window(range_slots, bm):
    return (-(-range_slots // bm) + 2) * bm


def _dispatch_tables(idx_p, w_p, cnt_p, idx_x, w_x, cnt_x, n_ranges, rtok, bm):
    k, extra = idx_x.shape
    slots = rtok + extra
    per_p, per_x = k * rtok, k * extra
    i32 = jnp.int32
    e_p = idx_p.reshape(k, n_ranges, rtok).transpose(1, 0, 2).reshape(n_ranges, per_p)
    keys_p = jnp.sort(e_p * per_p + jnp.arange(per_p, dtype=i32), axis=1)
    keys_x = jnp.sort(idx_x.reshape(per_x) * per_x + jnp.arange(per_x, dtype=i32))
    cnt_x = jnp.where((jnp.arange(n_ranges) == n_ranges - 1)[:, None], cnt_x[None, :], 0)
    start_p = jnp.cumsum(cnt_p, axis=1) - cnt_p
    start_x = jnp.cumsum(cnt_x, axis=1) - cnt_x
    n_sub = (cnt_p + cnt_x + bm - 1) // bm
    first_sub = jnp.cumsum(n_sub, axis=1) - n_sub
    tot_sub = -(-(per_p + per_x) // bm) + N_EXPERTS + _list_window(slots, bm) // bm
    sub = jnp.arange(tot_sub, dtype=i32)
    begun = sub[None, :, None] >= first_sub[:, None, :]
    at_begun = lambda a: jnp.max(jnp.where(begun, a[:, None, :], 0), axis=2)
    at_next = lambda a, end: jnp.min(jnp.where(begun, end, a[:, None, :]), axis=2)
    lo_p, hi_p = at_begun(start_p), at_next(start_p, per_p)
    lo_x, hi_x = at_begun(start_x), at_next(start_x, jnp.sum(cnt_x, axis=1)[:, None, None])
    pos = ((sub - at_begun(first_sub)) * bm)[:, :, None] + jnp.arange(bm, dtype=i32)
    n_p = (hi_p - lo_p)[:, :, None]
    flat = lambda a: a.reshape(n_ranges, tot_sub * bm)
    in_p = flat(pos < n_p)
    valid = flat(pos < n_p + (hi_x - lo_x)[:, :, None])
    src = flat(jnp.where(pos < n_p, lo_p[:, :, None] + pos, per_p + lo_x[:, :, None] + pos - n_p))
    keys = jnp.concatenate([keys_p, jnp.broadcast_to(keys_x, (n_ranges, per_x))], axis=1)
    key = jnp.take_along_axis(keys, jnp.clip(src, 0, per_p + per_x - 1), axis=1)
    slot = jnp.where(in_p, key % per_p, key % per_x)
    tok = jnp.where(in_p, slot % rtok, rtok + slot % extra)
    w = jnp.concatenate([w_p.reshape(k, n_ranges, rtok).transpose(1, 0, 2).reshape(n_ranges, per_p),
                         jnp.broadcast_to(w_x.reshape(per_x), (n_ranges, per_x))], axis=1)
    wl = jnp.where(valid, jnp.take_along_axis(w, jnp.where(in_p, slot, per_p + slot), axis=1), 0.0)
    spare = slots + jnp.arange(tot_sub * bm, dtype=i32) % bm
    src8 = jnp.where(valid, tok, 0) * 8
    dst8 = jnp.where(valid, tok, spare) * 8
    g = n_ranges * N_EXPERTS
    return first_sub.reshape(g), n_sub.reshape(g), src8[:, None, :], dst8[:, None, :], wl[:, None, :]


def _moe_sorted_kernel(first_ref, nsub_ref, src_ref, tok_ref, w_ref, h_hbm, hx_hbm, wg_ref, wu_ref, wd_ref,
                       acc_ref, hbuf, sem, wg_b, wu_b, wd_b, xg, ys, *, bm):
    r = pl.program_id(0)
    e = pl.program_id(1)
    n_chunks = D_MODEL // LANES
    n_blocks = nsub_ref[r * N_EXPERTS + e]
    range_rows = h_hbm.shape[1]

    @pl.when(e == 0)
    def _():
        main = pltpu.make_async_copy(h_hbm.at[r], hbuf.at[pl.ds(0, range_rows), :], sem.at[0])
        main.start()

        @pl.when(r == pl.num_programs(0) - 1)
        def _():
            extra = pltpu.make_async_copy(hx_hbm.at[0], hbuf.at[pl.ds(range_rows, hx_hbm.shape[1]), :], sem.at[1])
            extra.start()
            extra.wait()

        acc_ref[...] = jnp.zeros(acc_ref.shape, F32)
        main.wait()

    def gather(q0):
        for rr in range(bm):
            t8 = pl.multiple_of(src_ref[0, 0, q0 + rr], n_chunks)
            xg[rr * n_chunks:(rr + 1) * n_chunks, :] = hbuf[pl.ds(t8, n_chunks), :]

    def scatter(q0):
        for b0 in range(0, bm, SCATTER_BATCH):
            rows = range(b0, b0 + SCATTER_BATCH)
            toks = [pl.multiple_of(tok_ref[0, 0, q0 + rr], n_chunks) for rr in rows]
            wts = [w_ref[0, 0, q0 + rr] for rr in rows]
            olds = [acc_ref[0, pl.ds(t8, n_chunks), :] for t8 in toks]
            news = [o + wv * ys[rr * n_chunks:(rr + 1) * n_chunks, :] for o, wv, rr in zip(olds, wts, rows)]
            for t8, nv in zip(toks, news):
                acc_ref[0, pl.ds(t8, n_chunks), :] = nv

    def block(b, scatter_previous):
        x = jnp.concatenate([xg[pl.ds(j, bm, stride=n_chunks), :] for j in range(n_chunks)], axis=1).astype(BF16)
        gather((b + 1) * bm)
        if scatter_previous:
            scatter((b - 1) * bm)
        hh = _silu(_dot(x, wg_b[...])) * _dot(x, wu_b[...])
        y = _dot(hh.astype(BF16), wd_b[...])
        for i in range(bm // 8):
            for j in range(n_chunks):
                ys[pl.ds(i * 8 * n_chunks + j, 8, stride=n_chunks), :] = y[8 * i:8 * i + 8, j * LANES:(j + 1) * LANES]

    @pl.when(n_blocks > 0)
    def _():
        wg_b[...] = wg_ref[0].astype(BF16)
        wu_b[...] = wu_ref[0].astype(BF16)
        wd_b[...] = wd_ref[0].astype(BF16)
        gather(0)
        block(0, False)

        def body(b, carry):
            block(b, True)
            return carry

        lax.fori_loop(1, n_blocks, body, 0)
        scatter((n_blocks - 1) * bm)


def _moe_sorted(first_sub, n_sub, src8, dst8, wl, h2r, h2r_x, wg, wu, wd, *, bm):
    n_ranges, rows, _ = h2r.shape
    n_chunks = D_MODEL // LANES
    rtok = (rows + h2r_x.shape[1]) // n_chunks
    cap = _list_window(rtok, bm)
    one = pl.Element(1)
    window = lambda r, e, first, nsub: (r, 0, first[r * N_EXPERTS + e] * bm)
    list_spec = pl.BlockSpec((one, one, pl.Element(cap)), window, memory_space=pltpu.SMEM)
    grid_spec = pltpu.PrefetchScalarGridSpec(
        num_scalar_prefetch=2,
        grid=(n_ranges, N_EXPERTS),
        in_specs=[list_spec, list_spec, list_spec,
                  pl.BlockSpec(memory_space=pl.ANY), pl.BlockSpec(memory_space=pl.ANY),
                  pl.BlockSpec((1, D_MODEL, EXPERT_DIM), lambda r, e, *_: (e, 0, 0)),
                  pl.BlockSpec((1, D_MODEL, EXPERT_DIM), lambda r, e, *_: (e, 0, 0)),
                  pl.BlockSpec((1, EXPERT_DIM, D_MODEL), lambda r, e, *_: (e, 0, 0))],
        out_specs=pl.BlockSpec((1, (rtok + bm) * n_chunks, LANES), lambda r, e, *_: (r, 0, 0),
                               pipeline_mode=pl.Buffered(1)),
        scratch_shapes=[pltpu.VMEM((rtok * n_chunks, LANES), F32), pltpu.SemaphoreType.DMA((2,)),
                        pltpu.VMEM((D_MODEL, EXPERT_DIM), BF16), pltpu.VMEM((D_MODEL, EXPERT_DIM), BF16),
                        pltpu.VMEM((EXPERT_DIM, D_MODEL), BF16),
                        pltpu.VMEM((bm * n_chunks, LANES), F32), pltpu.VMEM((bm * n_chunks, LANES), F32)])
    return pl.pallas_call(
        functools.partial(_moe_sorted_kernel, bm=bm),
        grid_spec=grid_spec,
        out_shape=jax.ShapeDtypeStruct((n_ranges, (rtok + bm) * n_chunks, LANES), F32),
        compiler_params=pltpu.CompilerParams(dimension_semantics=("arbitrary", "arbitrary"),
                                             vmem_limit_bytes=VMEM_LIMIT),
        name="moe_grouped",
    )(first_sub, n_sub, src8, dst8, wl, h2r, h2r_x, wg, wu, wd)


def _final_grouped_kernel(x1_ref, fr_ref, h_ref, sg_ref, su_ref, sd_ref, g2_ref, gain_ref, y_ref):
    shared = _dot(_swiglu(h_ref[...], sg_ref[...], su_ref[...]).astype(BF16), sd_ref[...].astype(BF16))
    f = _from_token_rows(fr_ref, x1_ref.shape[0]) + shared
    y_ref[...] = x1_ref[...] + g2_ref[0] * _rmsnorm(f, gain_ref[...])


def _final_grouped(x1, fr, h2b, sg, su, sd, g2, gain, *, tm, fr_index, mod_index):
    t = x1.shape[0]
    n_chunks = D_MODEL // LANES
    row = pl.BlockSpec((tm, D_MODEL), lambda i: (i, 0))
    full = lambda shape: pl.BlockSpec(shape, lambda i: (0,) * len(shape))
    return pl.pallas_call(
        _final_grouped_kernel,
        grid=(t // tm,),
        in_specs=[row,
                  pl.BlockSpec((1, tm * n_chunks, LANES), lambda i: (*fr_index(i), 0)),
                  row, full(sg.shape), full(su.shape), full(sd.shape),
                  pl.BlockSpec((1, g2.shape[1], D_MODEL), lambda i: (mod_index(i), 0, 0)),
                  pl.BlockSpec((1, D_MODEL), lambda i: (0, 0))],
        out_specs=row,
        out_shape=jax.ShapeDtypeStruct((t, D_MODEL), F32),
        compiler_params=pltpu.CompilerParams(dimension_semantics=("parallel",), vmem_limit_bytes=VMEM_LIMIT),
        name="final_grouped",
    )(x1, fr, h2b, sg, su, sd, g2, gain)


def _rope_tables(pos):
    half = HEAD_DIM // 2
    inv_freq = ROPE_THETA ** (-jnp.arange(half, dtype=F32) / half)
    ang = pos.astype(F32)[:, None] * inv_freq[None, :]
    cos, sin = jnp.cos(ang), jnp.sin(ang)
    reps = LANES // HEAD_DIM
    return jnp.tile(jnp.concatenate([cos, cos], axis=1), (1, reps)), jnp.tile(jnp.concatenate([-sin, sin], axis=1), (1, reps))


def kernel(x_prompt, x_sample, c_prompt, c_sample, cache_k, cache_v, state_pool, page_table, ada_w, ada_b, pre_mix_g, post_mix_g, pre_ffn_g, post_ffn_g, w_in, w_out, lambda_q1, lambda_k1, lambda_q2, lambda_k2, subln_g, pool_w, pool_scale, router_w, router_bias, expert_w_gate, expert_w_up, expert_w_down, shared_w_gate, shared_w_up, shared_w_down):
    depth = ada_w.shape[0]
    assert depth == 1, "single-layer step"
    nb, seq, _ = x_prompt.shape
    db, dec_seq, _ = x_sample.shape
    assert dec_seq == 1
    page_size = cache_k.shape[2]
    n_pages = page_table.shape[1]
    past_len = n_pages * page_size
    lam_init = 0.8 - 0.6 * math.exp(-0.3 * 0)
    l = 0

    c_all = jnp.concatenate([c_prompt, c_sample], axis=0)
    pad = (-c_all.shape[0]) % 16
    mod = _ada(jnp.pad(c_all, ((0, pad), (0, 0))), ada_w[l], ada_b[l][None, :])
    mod_p = [m.reshape(nb, 1, D_MODEL) for m in jnp.split(mod[:nb], 6, axis=-1)]
    mod_s = [m.reshape(1, db, D_MODEL) for m in jnp.split(mod[nb:nb + db], 6, axis=-1)]

    w_in_b = w_in[l].astype(BF16)
    w_out_b = w_out[l].astype(BF16)
    lam4 = jnp.stack([lambda_q1[l], lambda_k1[l], lambda_q2[l], lambda_k2[l]])
    row = lambda v: v.reshape(1, -1)
    rwt = router_w[l].T
    rb_col = router_bias[l].reshape(N_EXPERTS, 1)
    experts = (expert_w_gate[l], expert_w_up[l], expert_w_down[l], shared_w_gate[l], shared_w_up[l], shared_w_down[l])

    t_p = nb * seq
    tm = TM_TOKEN
    seq_tiles = seq // tm
    cos_p, sin_p = _rope_tables(jnp.arange(seq))
    xp = x_prompt.reshape(t_p, D_MODEL)
    k_p, v_p, z_p, qb, kb, vt = _premix(xp, mod_p[0], mod_p[1], row(pre_mix_g[l]), w_in_b, cos_p, sin_p,
                                        tm=tm, rows_per_mod=seq_tiles, seq_tiles=seq_tiles, tk=TK_ATTN)
    o_p = _attn_prompt(qb.reshape(nb, seq, ATTN_W), kb.reshape(nb, seq, ATTN_W), vt, lam4,
                       subln_g[l].reshape(V_DIM, 1), tq=TQ_ATTN, tk=TK_ATTN, hps=HEADS_PER_STEP,
                       lam_init=lam_init)
    x1_p, h2_p, idx_p, w_p, h2r, cnt_p = _postmix(o_p.reshape(t_p, ATTN_W), z_p, z_p, xp, mod_p[2], mod_p[3],
                                                  mod_p[4], w_out_b, pool_w[l], row(pool_scale[l]),
                                                  row(post_mix_g[l]), row(pre_ffn_g[l]), rwt, rb_col,
                                                  tm=tm, rows_per_mod=seq_tiles, seq_tiles=seq_tiles, past_len=None)

    cos_s, sin_s = _rope_tables(jnp.full((db,), past_len))
    xs = x_sample.reshape(db, D_MODEL)
    k_s, v_s, z_s, qb_s, kb_s = _premix(xs, mod_s[0], mod_s[1], row(pre_mix_g[l]), w_in_b, cos_s, sin_s,
                                        tm=db, rows_per_mod=1, seq_tiles=1, tk=0)
    page_rows = page_size * N_HEADS
    k2 = cache_k[l].reshape(-1, V_DIM)
    v2 = cache_v[l].reshape(-1, V_DIM)
    o_s = _attn_decode(page_table, qb_s.reshape(db, N_HEADS, V_DIM), kb_s.reshape(db, N_HEADS, V_DIM),
                       v_s.reshape(db, N_HEADS, V_DIM), lam4, row(subln_g[l]), k2, v2,
                       page_rows=page_rows, lam_init=lam_init)
    hist_s = jnp.transpose(state_pool[l], (1, 0, 2))
    x1_s, h2_s, idx_s, w_s, h2r_s, cnt_s = _postmix(o_s.reshape(db, ATTN_W), z_s, hist_s, xs, mod_s[2], mod_s[3],
                                                    mod_s[4], w_out_b, pool_w[l], row(pool_scale[l]),
                                                    row(post_mix_g[l]), row(pre_ffn_g[l]), rwt, rb_col,
                                                    tm=db, rows_per_mod=1, seq_tiles=1, past_len=past_len)

    per_range = lambda c: jnp.sum(c.reshape(nb, -1, N_EXPERTS), axis=1).astype(jnp.int32)
    first_sub, n_sub, src8, dst8, wl = _dispatch_tables(idx_p, w_p, per_range(cnt_p), idx_s, w_s,
                                                        cnt_s.reshape(N_EXPERTS).astype(jnp.int32), nb, seq, MOE_BM)
    fr = _moe_sorted(first_sub, n_sub, src8, dst8, wl, h2r, h2r_s, *experts[:3], bm=MOE_BM)
    final_tiles = seq // TM_FINAL
    y_p = _final_grouped(x1_p, fr, h2_p, *experts[3:], mod_p[5], row(post_ffn_g[l]), tm=TM_FINAL,
                         fr_index=lambda i: (i // final_tiles, i % final_tiles), mod_index=lambda i: i // final_tiles)
    y_s = _final_grouped(x1_s, fr, h2_s, *experts[3:], mod_s[5], row(post_ffn_g[l]), tm=db,
                         fr_index=lambda i: (nb - 1, seq // db), mod_index=lambda i: 0)

    pool_p = z_p.reshape(nb, seq, POOL_W)[:, seq - POOL_BUF:]
    pool_s = jnp.concatenate([state_pool[l][:, 1:], z_s[:, None, :]], axis=1)
    return (y_p.reshape(nb, seq, D_MODEL), y_s.reshape(db, 1, D_MODEL),
            k_p.reshape(1, nb, seq, N_HEADS, V_DIM), v_p.reshape(1, nb, seq, N_HEADS, V_DIM), pool_p[None],
            k_s.reshape(1, db, 1, N_HEADS, V_DIM), v_s.reshape(1, db, 1, N_HEADS, V_DIM), pool_s[None])
```

```python
import functools
import math

import jax
import jax.numpy as jnp
from jax import lax
from jax.experimental import pallas as pl
from jax.experimental.pallas import tpu as pltpu

F32 = jnp.float32
BF16 = jnp.bfloat16

D_MODEL = 1024
ATTN_W = 512
POOL_W = 512
HEAD_DIM = 64
N_HEADS = 4
V_DIM = 2 * HEAD_DIM
IN_COLS = 3 * ATTN_W + POOL_W
POOL_WINDOWS = (2, 4, 8, 16)
POOL_GW = 128
POOL_BUF = 15
POOL_HALO = 16
ROPE_THETA = 10000.0
N_EXPERTS = 64
TOP_K = 8
N_EXPERT_GROUPS = 8
GROUP_SIZE = N_EXPERTS // N_EXPERT_GROUPS
TOPK_GROUPS = 4
EXPERT_DIM = 256
ROUTED_SCALE = 2.5
EPS = 1e-6
LANES = 128
NEG_INF = float("-inf")
Q_SCALE = HEAD_DIM ** -0.5 * math.log2(math.e)
VMEM_LIMIT = 56 * 1024 * 1024

TM_TOKEN = 512
TQ_ATTN = 512
TK_ATTN = 256
HEADS_PER_STEP = 4
MOE_BM = 128
SCATTER_BATCH = 8
TM_FINAL = 512
DEC_CHUNK_PAGES = 32


def _silu(x):
    return x * jax.nn.sigmoid(x)


def _rmsnorm(x, g):
    ms = jnp.mean(x * x, axis=-1, keepdims=True)
    return x * lax.rsqrt(ms + EPS) * g


def _dot(a, b):
    return jnp.dot(a, b, preferred_element_type=F32)


def _dot_nt(a, b):
    return lax.dot_general(a, b, (((1,), (1,)), ((), ())), preferred_element_type=F32)


def _ada_kernel(c_ref, w_ref, b_ref, o_ref):
    s = _silu(c_ref[...])
    o_ref[...] = _dot(s.astype(BF16), w_ref[...].astype(BF16)) + b_ref[...]


def _ada(c_pad, w, b):
    rows, n = c_pad.shape[0], w.shape[1]
    tn = 1536
    return pl.pallas_call(
        _ada_kernel,
        grid=(n // tn,),
        in_specs=[pl.BlockSpec((rows, D_MODEL), lambda j: (0, 0)),
                  pl.BlockSpec((D_MODEL, tn), lambda j: (0, j)),
                  pl.BlockSpec((1, tn), lambda j: (0, j))],
        out_specs=pl.BlockSpec((rows, tn), lambda j: (0, j)),
        out_shape=jax.ShapeDtypeStruct((rows, n), F32),
        compiler_params=pltpu.CompilerParams(dimension_semantics=("arbitrary",), vmem_limit_bytes=VMEM_LIMIT),
        name="ada_mod",
    )(c_pad, w, b)


def _rope(x, cos, sin_signed):
    outs = []
    for c in range(x.shape[1] // LANES):
        blk = x[:, c * LANES:(c + 1) * LANES]
        lane = lax.broadcasted_iota(jnp.int32, blk.shape, 1)
        partner = jnp.where((lane % HEAD_DIM) < HEAD_DIM // 2,
                            pltpu.roll(blk, LANES - HEAD_DIM // 2, 1),
                            pltpu.roll(blk, HEAD_DIM // 2, 1))
        outs.append(blk * cos + partner * sin_signed)
    return jnp.concatenate(outs, axis=1)


def _premix_kernel(x_ref, sh_ref, sc_ref, g_ref, w_ref, cos_ref, sin_ref,
                   k_ref, v_ref, z_ref, qb_ref, kb_ref, *rest, n_kv):
    h = _rmsnorm(x_ref[...], g_ref[...]) * (1.0 + sc_ref[0]) + sh_ref[0]
    proj = _dot(h.astype(BF16), w_ref[...])
    cos, sin = cos_ref[...], sin_ref[...]
    q = _rope(proj[:, :ATTN_W], cos, sin)
    k = _rope(proj[:, ATTN_W:2 * ATTN_W], cos, sin)
    v = proj[:, 2 * ATTN_W:3 * ATTN_W]
    for out_ref, val in ((k_ref, k), (v_ref, v)):
        for i in range(val.shape[0] // 8):
            for hh in range(N_HEADS):
                out_ref[pl.ds(i * 8 * N_HEADS + hh, 8, stride=N_HEADS), :] = val[8 * i:8 * i + 8, hh * V_DIM:(hh + 1) * V_DIM]
    z_ref[...] = proj[:, 3 * ATTN_W:]
    qb_ref[...] = (q * Q_SCALE).astype(BF16)
    kb_ref[...] = k.astype(BF16)
    if n_kv:
        vt_ref = rest[0]
        vt = v.T.astype(BF16)
        tk = vt.shape[1] // n_kv
        for hh in range(N_HEADS):
            for c in range(n_kv):
                vt_ref[0, hh, c] = vt[hh * V_DIM:(hh + 1) * V_DIM, c * tk:(c + 1) * tk]


def _premix(x, shift, scale, gain, w_in_b, cos, sin, *, tm, rows_per_mod, seq_tiles, tk):
    t = x.shape[0]
    n_tiles = t // tm
    mod_rows = shift.shape[1]
    mod_idx = (lambda i: (i // rows_per_mod, 0, 0))
    n_kv = tm // tk if tk else 0
    out_shape = [jax.ShapeDtypeStruct((t * N_HEADS, V_DIM), F32), jax.ShapeDtypeStruct((t * N_HEADS, V_DIM), F32),
                 jax.ShapeDtypeStruct((t, POOL_W), F32), jax.ShapeDtypeStruct((t, ATTN_W), BF16),
                 jax.ShapeDtypeStruct((t, ATTN_W), BF16)]
    row_spec = pl.BlockSpec((tm, ATTN_W), lambda i: (i, 0))
    head_rows = pl.BlockSpec((tm * N_HEADS, V_DIM), lambda i: (i, 0))
    out_specs = [head_rows, head_rows, row_spec, row_spec, row_spec]
    if n_kv:
        nb = n_tiles // seq_tiles
        out_shape.append(jax.ShapeDtypeStruct((nb, N_HEADS, seq_tiles * n_kv, V_DIM, tk), BF16))
        out_specs.append(pl.BlockSpec((1, N_HEADS, n_kv, V_DIM, tk),
                                      lambda i: (i // seq_tiles, 0, i % seq_tiles, 0, 0)))
    return pl.pallas_call(
        functools.partial(_premix_kernel, n_kv=n_kv),
        grid=(n_tiles,),
        in_specs=[pl.BlockSpec((tm, D_MODEL), lambda i: (i, 0)),
                  pl.BlockSpec((1, mod_rows, D_MODEL), mod_idx),
                  pl.BlockSpec((1, mod_rows, D_MODEL), mod_idx),
                  pl.BlockSpec((1, D_MODEL), lambda i: (0, 0)),
                  pl.BlockSpec((D_MODEL, IN_COLS), lambda i: (0, 0)),
                  pl.BlockSpec((tm, LANES), lambda i: (i % seq_tiles, 0)),
                  pl.BlockSpec((tm, LANES), lambda i: (i % seq_tiles, 0))],
        out_specs=out_specs,
        out_shape=out_shape,
        compiler_params=pltpu.CompilerParams(dimension_semantics=("parallel",), vmem_limit_bytes=VMEM_LIMIT),
        name="premix",
    )(x, shift, scale, gain, w_in_b, cos, sin)


def _diff_lambda(lam_ref, lam_init):
    lam = lam_ref[...]
    s1 = jnp.sum(lam[0:1] * lam[1:2], axis=1, keepdims=True)
    s2 = jnp.sum(lam[2:3] * lam[3:4], axis=1, keepdims=True)
    return jnp.exp(s1) - jnp.exp(s2) + lam_init


def _split_maps(q):
    lane = lax.broadcasted_iota(jnp.int32, q.shape, 1)
    zero = jnp.zeros_like(q)
    return jnp.concatenate([jnp.where(lane < HEAD_DIM, q, zero), jnp.where(lane >= HEAD_DIM, q, zero)], axis=0)


def _attn_kernel(q_ref, k_ref, vt_ref, lam_ref, g_ref, o_ref, s_even, s_odd, acc_ref, *, tq, tk, hps, lam_init):
    i = pl.program_id(2)
    r = tq // tk
    heads = range(hps)
    cols = [slice(hh * V_DIM, (hh + 1) * V_DIM) for hh in heads]
    qpads = [_split_maps(q_ref[0, :, cols[hh]]) for hh in heads]

    def logits(j, buf):
        rows = pl.ds(pl.multiple_of(j * tk, tk), tk)
        for hh in heads:
            buf[hh] = _dot_nt(k_ref[0, rows, cols[hh]], qpads[hh])

    def consume(j, buf, carry, masked):
        def read(hh):
            s = buf[hh]
            if masked:
                kpos = j * tk + lax.broadcasted_iota(jnp.int32, s.shape, 0)
                qpos = i * tq + lax.broadcasted_iota(jnp.int32, s.shape, 1) % tq
                s = jnp.where(kpos <= qpos, s, NEG_INF)
            return s

        ps, alphas, out = [], [], []
        for hh in heads:
            m, l = carry[hh]
            m_new = jnp.maximum(m, jnp.max(read(hh), axis=0, keepdims=True))
            alpha = jnp.exp2(m - m_new)
            p = jnp.exp2(read(hh) - m_new)
            out.append((m_new, alpha * l + jnp.sum(p, axis=0, keepdims=True)))
            alphas.append(alpha)
            ps.append(p.astype(BF16))
        for hh in heads:
            acc_ref[hh] = alphas[hh] * acc_ref[hh] + _dot(vt_ref[0, hh, j], ps[hh])
        return tuple(out)

    def pair(p, carry):
        j = 2 * p
        logits(j + 1, s_odd)
        carry = consume(j, s_even, carry, False)
        logits(j + 2, s_even)
        return consume(j + 1, s_odd, carry, False)

    assert r == 2, "the written-out schedule handles two key blocks per query block"
    init = tuple((jnp.full((1, 2 * tq), -1e30, F32), jnp.zeros((1, 2 * tq), F32)) for _ in heads)
    acc_ref[...] = jnp.zeros(acc_ref.shape, F32)
    logits(0, s_even)
    carry = lax.fori_loop(0, i, pair, init)
    logits(2 * i + 1, s_odd)
    carry = consume(2 * i, s_even, carry, True)
    carry = consume(2 * i + 1, s_odd, carry, True)
    lam = _diff_lambda(lam_ref, lam_init)
    for hh in heads:
        o = acc_ref[hh] / carry[hh][1]
        od = o[:, :tq] - lam * o[:, tq:]
        ms = jnp.mean(od * od, axis=0, keepdims=True)
        on = od * lax.rsqrt(ms + EPS) * g_ref[...] * (1.0 - lam_init)
        o_ref[0, :, cols[hh]] = on.T.astype(BF16)


def _attn_prompt(qb, kb, vt, lam4, subln_col, *, tq, tk, hps, lam_init):
    b, s, _ = qb.shape
    nq, nkv = s // tq, s // tk
    w = hps * V_DIM
    return pl.pallas_call(
        functools.partial(_attn_kernel, tq=tq, tk=tk, hps=hps, lam_init=lam_init),
        grid=(b, N_HEADS // hps, nq),
        in_specs=[pl.BlockSpec((1, tq, w), lambda bb, h, i: (bb, i, h)),
                  pl.BlockSpec((1, s, w), lambda bb, h, i: (bb, 0, h)),
                  pl.BlockSpec((1, hps, nkv, V_DIM, tk), lambda bb, h, i: (bb, h, 0, 0, 0)),
                  pl.BlockSpec((4, HEAD_DIM), lambda bb, h, i: (0, 0)),
                  pl.BlockSpec((V_DIM, 1), lambda bb, h, i: (0, 0))],
        out_specs=pl.BlockSpec((1, tq, w), lambda bb, h, i: (bb, i, h)),
        out_shape=jax.ShapeDtypeStruct((b, s, ATTN_W), BF16),
        scratch_shapes=[pltpu.VMEM((hps, tk, 2 * tq), F32), pltpu.VMEM((hps, tk, 2 * tq), F32),
                        pltpu.VMEM((hps, V_DIM, 2 * tq), F32)],
        compiler_params=pltpu.CompilerParams(dimension_semantics=("parallel", "parallel", "arbitrary"),
                                             vmem_limit_bytes=VMEM_LIMIT),
        name="attn_prompt",
    )(qb, kb, vt, lam4, subln_col)


def _decode_kernel(pt_ref, q_ref, kn_ref, vn_ref, lam_ref, g_ref, k_hbm, v_hbm, o_ref,
                   kbuf, vbuf, sem, m_sc, l_sc, acc_sc, *, n_chunks, chunk_pages, page_rows, lam_init):
    b = pl.program_id(0)
    c = pl.program_id(1)
    nb = pl.num_programs(0)
    step = b * n_chunks + c
    slot = step % 2

    def chunk_copies(bb, cc, sl):
        copies = []
        for p in range(chunk_pages):
            phys = pt_ref[bb * (n_chunks * chunk_pages) + cc * chunk_pages + p]
            src = pl.ds(pl.multiple_of(phys * page_rows, page_rows), page_rows)
            dst = pl.ds(p * page_rows, page_rows)
            copies.append(pltpu.make_async_copy(k_hbm.at[src, :], kbuf.at[sl, dst, :], sem.at[0, sl]))
            copies.append(pltpu.make_async_copy(v_hbm.at[src, :], vbuf.at[sl, dst, :], sem.at[1, sl]))
        return copies

    @pl.when(step == 0)
    def _():
        for cp in chunk_copies(b, c, slot):
            cp.start()

    @pl.when(step + 1 < nb * n_chunks)
    def _():
        nxt = step + 1
        for cp in chunk_copies(nxt // n_chunks, nxt % n_chunks, 1 - slot):
            cp.start()

    @pl.when(c == 0)
    def _():
        m_sc[...] = jnp.full(m_sc.shape, -1e30, F32)
        l_sc[...] = jnp.zeros(l_sc.shape, F32)
        acc_sc[...] = jnp.zeros(acc_sc.shape, F32)

    qblk = _split_maps(q_ref[0])
    for cp in chunk_copies(b, c, slot):
        cp.wait()
    kc = kbuf[slot].astype(BF16)
    vc = vbuf[slot].astype(BF16)
    s = _dot_nt(qblk, kc)
    row_head = lax.broadcasted_iota(jnp.int32, s.shape, 0) % N_HEADS
    col_head = lax.broadcasted_iota(jnp.int32, s.shape, 1) % N_HEADS
    s = jnp.where(row_head == col_head, s, NEG_INF)
    m_old = m_sc[...]
    m_new = jnp.maximum(m_old, jnp.max(s, axis=1, keepdims=True))
    alpha = jnp.exp2(m_old - m_new)
    p = jnp.exp2(s - m_new)
    l_new = alpha * l_sc[...] + jnp.sum(p, axis=1, keepdims=True)
    acc_new = alpha * acc_sc[...] + _dot(p.astype(BF16), vc)
    m_sc[...] = m_new
    l_sc[...] = l_new
    acc_sc[...] = acc_new

    @pl.when(c == n_chunks - 1)
    def _():
        kn = jnp.concatenate([kn_ref[0], kn_ref[0]], axis=0).astype(F32)
        vn = jnp.concatenate([vn_ref[0], vn_ref[0]], axis=0).astype(F32)
        s_n = jnp.sum(qblk.astype(F32) * kn, axis=1, keepdims=True)
        m_f = jnp.maximum(m_new, s_n)
        a_f = jnp.exp2(m_new - m_f)
        p_n = jnp.exp2(s_n - m_f)
        l_f = a_f * l_new + p_n
        o = (a_f * acc_new + p_n * vn) / l_f
        od = o[:N_HEADS] - _diff_lambda(lam_ref, lam_init) * o[N_HEADS:]
        o_ref[0] = (_rmsnorm(od, g_ref[...]) * (1.0 - lam_init)).astype(BF16)


def _attn_decode(page_table, q3, kn3, vn3, lam4, subln_row, k2, v2, *, page_rows, lam_init):
    nb, n_pages = page_table.shape
    chunk_pages = DEC_CHUNK_PAGES
    n_chunks = n_pages // chunk_pages
    rows = chunk_pages * page_rows
    head_spec = pl.BlockSpec((1, N_HEADS, V_DIM), lambda b, c, pt: (b, 0, 0))
    grid_spec = pltpu.PrefetchScalarGridSpec(
        num_scalar_prefetch=1,
        grid=(nb, n_chunks),
        in_specs=[head_spec, head_spec, head_spec,
                  pl.BlockSpec((4, HEAD_DIM), lambda b, c, pt: (0, 0)),
                  pl.BlockSpec((1, V_DIM), lambda b, c, pt: (0, 0)),
                  pl.BlockSpec(memory_space=pl.ANY),
                  pl.BlockSpec(memory_space=pl.ANY)],
        out_specs=head_spec,
        scratch_shapes=[pltpu.VMEM((2, rows, V_DIM), F32), pltpu.VMEM((2, rows, V_DIM), F32),
                        pltpu.SemaphoreType.DMA((2, 2)),
                        pltpu.VMEM((2 * N_HEADS, 1), F32), pltpu.VMEM((2 * N_HEADS, 1), F32),
                        pltpu.VMEM((2 * N_HEADS, V_DIM), F32)])
    return pl.pallas_call(
        functools.partial(_decode_kernel, n_chunks=n_chunks, chunk_pages=chunk_pages,
                          page_rows=page_rows, lam_init=lam_init),
        grid_spec=grid_spec,
        out_shape=jax.ShapeDtypeStruct((nb, N_HEADS, V_DIM), BF16),
        compiler_params=pltpu.CompilerParams(dimension_semantics=("arbitrary", "arbitrary"),
                                             vmem_limit_bytes=VMEM_LIMIT),
        name="attn_decode",
    )(page_table.reshape(-1), q3, kn3, vn3, lam4, subln_row, k2, v2)


def _pool_project(d_groups, pw_ref, ps_ref):
    ys = [_dot(d.astype(BF16), pw_ref[g].astype(BF16)) for g, d in enumerate(d_groups)]
    return jnp.concatenate(ys, axis=1) * ps_ref[...]


def _argmax_first(x, iota, n):
    mx = jnp.max(x, axis=0, keepdims=True)
    ix = jnp.min(jnp.where(x == mx, iota, float(n)), axis=0, keepdims=True)
    return mx, ix


def _row_iota(shape):
    return lax.broadcasted_iota(jnp.int32, shape, 0).astype(F32)


def _route(h2, rwt_ref, rb_ref):
    rw = rwt_ref[...]
    rw_hi = rw.astype(BF16)
    rw_lo = (rw - rw_hi.astype(F32)).astype(BF16)
    h_hi = h2.astype(BF16)
    h_lo = (h2 - h_hi.astype(F32)).astype(BF16)
    logits = _dot_nt(rw_hi, h_hi) + (_dot_nt(rw_hi, h_lo) + _dot_nt(rw_lo, h_hi))
    scores = jax.nn.sigmoid(logits)
    sel = scores + rb_ref[...]
    tm = sel.shape[1]
    iota_g = _row_iota((GROUP_SIZE, tm))
    group_rows = []
    for g in range(N_EXPERT_GROUPS):
        xg = sel[g * GROUP_SIZE:(g + 1) * GROUP_SIZE]
        m1, i1 = _argmax_first(xg, iota_g, GROUP_SIZE)
        m2 = jnp.max(jnp.where(iota_g == i1, NEG_INF, xg), axis=0, keepdims=True)
        group_rows.append(m1 + m2)
    gscore = jnp.concatenate(group_rows, axis=0)
    iota_gr = _row_iota(gscore.shape)
    gmask = jnp.zeros(gscore.shape, F32)
    for _ in range(TOPK_GROUPS):
        _, ig = _argmax_first(gscore, iota_gr, N_EXPERT_GROUPS)
        hit = iota_gr == ig
        gmask = jnp.where(hit, 1.0, gmask)
        gscore = jnp.where(hit, NEG_INF, gscore)
    masked = jnp.concatenate(
        [jnp.where(gmask[g:g + 1] > 0.5, sel[g * GROUP_SIZE:(g + 1) * GROUP_SIZE], NEG_INF)
         for g in range(N_EXPERT_GROUPS)], axis=0)
    iota_e = _row_iota(masked.shape)
    w = jnp.zeros(masked.shape, F32)
    chosen = jnp.zeros(masked.shape, F32)
    idx_rows, score_rows = [], []
    for _ in range(TOP_K):
        _, ie = _argmax_first(masked, iota_e, N_EXPERTS)
        hit = iota_e == ie
        w = jnp.where(hit, scores, w)
        chosen = jnp.where(hit, 1.0, chosen)
        masked = jnp.where(hit, NEG_INF, masked)
        idx_rows.append(ie)
        score_rows.append(jnp.sum(jnp.where(hit, scores, 0.0), axis=0, keepdims=True))
    wsum = jnp.sum(w, axis=0, keepdims=True)
    idx_t = jnp.concatenate(idx_rows, axis=0).astype(jnp.int32)
    w_t = jnp.concatenate(score_rows, axis=0) / wsum * ROUTED_SCALE
    return idx_t, w_t, jnp.sum(chosen, axis=1, keepdims=True)


def _to_token_rows(x, out_ref):
    n_chunks = x.shape[1] // LANES
    for i in range(x.shape[0] // 8):
        for j in range(n_chunks):
            out_ref[0, pl.ds(i * 8 * n_chunks + j, 8, stride=n_chunks), :] = x[8 * i:8 * i + 8, j * LANES:(j + 1) * LANES]


def _from_token_rows(in_ref, tm):
    n_chunks = D_MODEL // LANES
    return jnp.concatenate([in_ref[0, pl.ds(j, tm, stride=n_chunks), :] for j in range(n_chunks)], axis=1)


def _mix_tail(o_b, pz, x, g1, sh2, sc2, wout_ref, gpost_ref, gpre_ref, rwt_ref, rb_ref,
              x1_ref, h2_ref, idx_ref, w_ref, h2r_ref, cnt_ref):
    m = _dot(o_b, wout_ref[:ATTN_W, :]) + _dot(pz.astype(BF16), wout_ref[ATTN_W:, :])
    x1 = x + g1 * _rmsnorm(m, gpost_ref[...])
    h2 = _rmsnorm(x1, gpre_ref[...]) * (1.0 + sc2) + sh2
    x1_ref[...] = x1
    h2_ref[...] = h2.astype(BF16)
    idx_ref[...], w_ref[...], cnt_ref[0] = _route(h2, rwt_ref, rb_ref)
    _to_token_rows(h2, h2r_ref)


def _postmix_prompt_kernel(o_ref, z_ref, zprev_ref, x_ref, g1_ref, sh2_ref, sc2_ref,
                           wout_ref, pw_ref, ps_ref, gpost_ref, gpre_ref, rwt_ref, rb_ref,
                           x1_ref, h2_ref, idx_ref, w_ref, h2r_ref, cnt_ref, zbuf, *, seq_tiles):
    tm = z_ref.shape[0]
    it = pl.program_id(0) % seq_tiles
    zt = z_ref[...]
    zbuf[0:POOL_HALO, :] = jnp.where(it == 0, 0.0, zprev_ref[...])
    zbuf[POOL_HALO:, :] = zt
    pos = it * tm + lax.broadcasted_iota(jnp.int32, (tm, 1), 0)
    d_groups = []
    for g, w in enumerate(POOL_WINDOWS):
        cols = slice(g * POOL_GW, (g + 1) * POOL_GW)
        win = zt[:, cols]
        for s in range(1, w):
            win = win + zbuf[POOL_HALO - s:POOL_HALO - s + tm, cols]
        cnt = jnp.minimum(pos + 1, w).astype(F32)
        d_groups.append(win / cnt - zt[:, cols])
    pz = _pool_project(d_groups, pw_ref, ps_ref)
    _mix_tail(o_ref[...], pz, x_ref[...], g1_ref[0], sh2_ref[0], sc2_ref[0], wout_ref, gpost_ref, gpre_ref,
              rwt_ref, rb_ref, x1_ref, h2_ref, idx_ref, w_ref, h2r_ref, cnt_ref)


def _postmix_sample_kernel(o_ref, z_ref, st_ref, x_ref, g1_ref, sh2_ref, sc2_ref,
                           wout_ref, pw_ref, ps_ref, gpost_ref, gpre_ref, rwt_ref, rb_ref,
                           x1_ref, h2_ref, idx_ref, w_ref, h2r_ref, cnt_ref, *, past_len):
    zt = z_ref[...]
    d_groups = []
    for g, w in enumerate(POOL_WINDOWS):
        cols = slice(g * POOL_GW, (g + 1) * POOL_GW)
        win = zt[:, cols]
        for s in range(1, w):
            win = win + st_ref[POOL_BUF - s][:, cols]
        d_groups.append(win / float(min(past_len + 1, w)) - zt[:, cols])
    pz = _pool_project(d_groups, pw_ref, ps_ref)
    _mix_tail(o_ref[...], pz, x_ref[...], g1_ref[0], sh2_ref[0], sc2_ref[0], wout_ref, gpost_ref, gpre_ref,
              rwt_ref, rb_ref, x1_ref, h2_ref, idx_ref, w_ref, h2r_ref, cnt_ref)


def _postmix(o, z, hist, x, g1, sh2, sc2, wout_b, pool_w, pool_scale, g_post, g_pre, rwt, rb_col,
             *, tm, rows_per_mod, seq_tiles, past_len):
    t = x.shape[0]
    n_tiles = t // tm
    mod_rows = g1.shape[1]
    rows8 = D_MODEL // LANES
    mod_spec = pl.BlockSpec((1, mod_rows, D_MODEL), lambda i: (i // rows_per_mod, 0, 0))
    full = lambda shape: pl.BlockSpec(shape, lambda i: (0,) * len(shape))
    if past_len is None:
        body = functools.partial(_postmix_prompt_kernel, seq_tiles=seq_tiles)
        halo_blocks = tm // POOL_HALO
        hist_spec = pl.BlockSpec((POOL_HALO, POOL_W), lambda i: (jnp.maximum(i * halo_blocks - 1, 0), 0))
        scratch = [pltpu.VMEM((POOL_HALO + tm, POOL_W), F32)]
    else:
        body = functools.partial(_postmix_sample_kernel, past_len=past_len)
        hist_spec = full(hist.shape)
        scratch = []
    return pl.pallas_call(
        body,
        grid=(n_tiles,),
        in_specs=[pl.BlockSpec((tm, ATTN_W), lambda i: (i, 0)),
                  pl.BlockSpec((tm, POOL_W), lambda i: (i, 0)),
                  hist_spec,
                  pl.BlockSpec((tm, D_MODEL), lambda i: (i, 0)),
                  mod_spec, mod_spec, mod_spec,
                  full((D_MODEL, D_MODEL)), full(pool_w.shape), full((1, POOL_W)),
                  full((1, D_MODEL)), full((1, D_MODEL)), full((N_EXPERTS, D_MODEL)), full((N_EXPERTS, 1))],
        out_specs=[pl.BlockSpec((tm, D_MODEL), lambda i: (i, 0)),
                   pl.BlockSpec((tm, D_MODEL), lambda i: (i, 0)),
                   pl.BlockSpec((TOP_K, tm), lambda i: (0, i)), pl.BlockSpec((TOP_K, tm), lambda i: (0, i)),
                   pl.BlockSpec((1, tm * rows8, LANES), lambda i: (i // seq_tiles, i % seq_tiles, 0)),
                   pl.BlockSpec((1, N_EXPERTS, 1), lambda i: (i, 0, 0))],
        out_shape=[jax.ShapeDtypeStruct((t, D_MODEL), F32), jax.ShapeDtypeStruct((t, D_MODEL), BF16),
                   jax.ShapeDtypeStruct((TOP_K, t), jnp.int32), jax.ShapeDtypeStruct((TOP_K, t), F32),
                   jax.ShapeDtypeStruct((n_tiles // seq_tiles, seq_tiles * tm * rows8, LANES), F32),
                   jax.ShapeDtypeStruct((n_tiles, N_EXPERTS, 1), F32)],
        scratch_shapes=scratch,
        compiler_params=pltpu.CompilerParams(dimension_semantics=("parallel",), vmem_limit_bytes=VMEM_LIMIT),
        name="postmix",
    )(o, z, hist, x, g1, sh2, sc2, wout_b, pool_w, pool_scale, g_post, g_pre, rwt, rb_col)


def _swiglu(xb, wg, wu):
    return _silu(_dot(xb, wg.astype(BF16))) * _dot(xb, wu.astype(BF16))


def _list_window(range_slots, bm):
    return (-(-range_slots // bm) + 3) * bm


def _dispatch_tables(idx_p, w_p, cnt_p, idx_x, w_x, cnt_x, n_ranges, rtok, bm):
    k, extra = idx_x.shape
    slots = rtok + extra
    per_p, per_x = k * rtok, k * extra
    i32 = jnp.int32
    e_p = idx_p.reshape(k, n_ranges, rtok).transpose(1, 0, 2).reshape(n_ranges, per_p)
    keys_p = jnp.sort(e_p * per_p + jnp.arange(per_p, dtype=i32), axis=1)
    keys_x = jnp.sort(idx_x.reshape(per_x) * per_x + jnp.arange(per_x, dtype=i32))
    cnt_x = jnp.where((jnp.arange(n_ranges) == n_ranges - 1)[:, None], cnt_x[None, :], 0)
    start_p = jnp.cumsum(cnt_p, axis=1) - cnt_p
    start_x = jnp.cumsum(cnt_x, axis=1) - cnt_x
    n_sub = (cnt_p + cnt_x + bm - 1) // bm
    first_sub = jnp.cumsum(n_sub, axis=1) - n_sub
    tot_sub = 1 + -(-(per_p + per_x) // bm) + N_EXPERTS
    sub = jnp.arange(tot_sub, dtype=i32) - 1
    begun = sub[None, :, None] >= first_sub[:, None, :]
    at_begun = lambda a: jnp.max(jnp.where(begun, a[:, None, :], 0), axis=2)
    at_next = lambda a, end: jnp.min(jnp.where(begun, end, a[:, None, :]), axis=2)
    lo_p, hi_p = at_begun(start_p), at_next(start_p, per_p)
    lo_x, hi_x = at_begun(start_x), at_next(start_x, jnp.sum(cnt_x, axis=1)[:, None, None])
    pos = ((sub - at_begun(first_sub)) * bm)[:, :, None] + jnp.arange(bm, dtype=i32)
    n_p = (hi_p - lo_p)[:, :, None]
    flat = lambda a: a.reshape(n_ranges, tot_sub * bm)
    in_p = flat(pos < n_p)
    valid = flat(jnp.logical_and(pos < n_p + (hi_x - lo_x)[:, :, None], (sub >= 0)[None, :, None]))
    src = flat(jnp.where(pos < n_p, lo_p[:, :, None] + pos, per_p + lo_x[:, :, None] + pos - n_p))
    keys = jnp.concatenate([keys_p, jnp.broadcast_to(keys_x, (n_ranges, per_x))], axis=1)
    key = jnp.take_along_axis(keys, jnp.clip(src, 0, per_p + per_x - 1), axis=1)
    slot = jnp.where(in_p, key % per_p, key % per_x)
    tok = jnp.where(in_p, slot % rtok, rtok + slot % extra)
    w = jnp.concatenate([w_p.reshape(k, n_ranges, rtok).transpose(1, 0, 2).reshape(n_ranges, per_p),
                         jnp.broadcast_to(w_x.reshape(per_x), (n_ranges, per_x))], axis=1)
    wl = jnp.where(valid, jnp.take_along_axis(w, jnp.where(in_p, slot, per_p + slot), axis=1), 0.0)
    spare = slots + jnp.arange(tot_sub * bm, dtype=i32) % bm
    tail = _list_window(slots, bm)
    padded = lambda a, fill: jnp.concatenate([a, jnp.broadcast_to(fill, (n_ranges, tail)).astype(a.dtype)], axis=1)
    src8 = padded(jnp.where(valid, tok, 0) * 8, 0)
    dst8 = padded(jnp.where(valid, tok, spare) * 8, (slots + jnp.arange(tail, dtype=i32) % bm) * 8)
    g = n_ranges * N_EXPERTS
    return first_sub.reshape(g), n_sub.reshape(g), src8[:, None, :], dst8[:, None, :], padded(wl, 0.0)[:, None, :]


def _moe_sorted_kernel(first_ref, nsub_ref, src_ref, tok_ref, w_ref, h_hbm, hx_hbm, wg_ref, wu_ref, wd_ref,
                       acc_ref, hbuf, sem, wg_b, wu_b, wd_b, xg, ys, *, bm):
    r = pl.program_id(0)
    e = pl.program_id(1)
    n_chunks = D_MODEL // LANES
    n_blocks = nsub_ref[r * N_EXPERTS + e]
    range_rows = h_hbm.shape[1]

    @pl.when(e == 0)
    def _():
        main = pltpu.make_async_copy(h_hbm.at[r], hbuf.at[pl.ds(0, range_rows), :], sem.at[0])
        main.start()

        @pl.when(r == pl.num_programs(0) - 1)
        def _():
            extra = pltpu.make_async_copy(hx_hbm.at[0], hbuf.at[pl.ds(range_rows, hx_hbm.shape[1]), :], sem.at[1])
            extra.start()
            extra.wait()

        acc_ref[...] = jnp.zeros(acc_ref.shape, F32)
        ys[...] = jnp.zeros(ys.shape, F32)
        main.wait()

    def gather(q0):
        for rr in range(bm):
            t8 = pl.multiple_of(src_ref[0, 0, q0 + rr], n_chunks)
            xg[rr * n_chunks:(rr + 1) * n_chunks, :] = hbuf[pl.ds(t8, n_chunks), :]

    def scatter(q0):
        for b0 in range(0, bm, SCATTER_BATCH):
            rows = range(b0, b0 + SCATTER_BATCH)
            toks = [pl.multiple_of(tok_ref[0, 0, q0 + rr], n_chunks) for rr in rows]
            wts = [w_ref[0, 0, q0 + rr] for rr in rows]
            olds = [acc_ref[0, pl.ds(t8, n_chunks), :] for t8 in toks]
            news = [o + wv * ys[rr * n_chunks:(rr + 1) * n_chunks, :] for o, wv, rr in zip(olds, wts, rows)]
            for t8, nv in zip(toks, news):
                acc_ref[0, pl.ds(t8, n_chunks), :] = nv

    def block(b, carry):
        x = jnp.concatenate([xg[pl.ds(j, bm, stride=n_chunks), :] for j in range(n_chunks)], axis=1).astype(BF16)
        gather((b + 2) * bm)
        scatter(b * bm)
        hh = _silu(_dot(x, wg_b[...])) * _dot(x, wu_b[...])
        y = _dot(hh.astype(BF16), wd_b[...])
        for i in range(bm // 8):
            for j in range(n_chunks):
                ys[pl.ds(i * 8 * n_chunks + j, 8, stride=n_chunks), :] = y[8 * i:8 * i + 8, j * LANES:(j + 1) * LANES]
        return carry

    @pl.when(e == 0)
    def _():
        gather(bm)

    @pl.when(n_blocks > 0)
    def _():
        wg_b[...] = wg_ref[0].astype(BF16)
        wu_b[...] = wu_ref[0].astype(BF16)
        wd_b[...] = wd_ref[0].astype(BF16)
        lax.fori_loop(0, n_blocks, block, 0)

    @pl.when(e == pl.num_programs(1) - 1)
    def _():
        scatter(n_blocks * bm)


def _moe_sorted(first_sub, n_sub, src8, dst8, wl, h2r, h2r_x, wg, wu, wd, *, bm):
    n_ranges, rows, _ = h2r.shape
    n_chunks = D_MODEL // LANES
    rtok = (rows + h2r_x.shape[1]) // n_chunks
    cap = _list_window(rtok, bm)
    one = pl.Element(1)
    window = lambda r, e, first, nsub: (r, 0, first[r * N_EXPERTS + e] * bm)
    list_spec = pl.BlockSpec((one, one, pl.Element(cap)), window, memory_space=pltpu.SMEM)
    grid_spec = pltpu.PrefetchScalarGridSpec(
        num_scalar_prefetch=2,
        grid=(n_ranges, N_EXPERTS),
        in_specs=[list_spec, list_spec, list_spec,
                  pl.BlockSpec(memory_space=pl.ANY), pl.BlockSpec(memory_space=pl.ANY),
                  pl.BlockSpec((1, D_MODEL, EXPERT_DIM), lambda r, e, *_: (e, 0, 0)),
                  pl.BlockSpec((1, D_MODEL, EXPERT_DIM), lambda r, e, *_: (e, 0, 0)),
                  pl.BlockSpec((1, EXPERT_DIM, D_MODEL), lambda r, e, *_: (e, 0, 0))],
        out_specs=pl.BlockSpec((1, (rtok + bm) * n_chunks, LANES), lambda r, e, *_: (r, 0, 0),
                               pipeline_mode=pl.Buffered(1)),
        scratch_shapes=[pltpu.VMEM((rtok * n_chunks, LANES), F32), pltpu.SemaphoreType.DMA((2,)),
                        pltpu.VMEM((D_MODEL, EXPERT_DIM), BF16), pltpu.VMEM((D_MODEL, EXPERT_DIM), BF16),
                        pltpu.VMEM((EXPERT_DIM, D_MODEL), BF16),
                        pltpu.VMEM((bm * n_chunks, LANES), F32), pltpu.VMEM((bm * n_chunks, LANES), F32)])
    return pl.pallas_call(
        functools.partial(_moe_sorted_kernel, bm=bm),
        grid_spec=grid_spec,
        out_shape=jax.ShapeDtypeStruct((n_ranges, (rtok + bm) * n_chunks, LANES), F32),
        compiler_params=pltpu.CompilerParams(dimension_semantics=("arbitrary", "arbitrary"),
                                             vmem_limit_bytes=VMEM_LIMIT),
        name="moe_grouped",
    )(first_sub, n_sub, src8, dst8, wl, h2r, h2r_x, wg, wu, wd)


def _final_grouped_kernel(x1_ref, fr_ref, h_ref, sg_ref, su_ref, sd_ref, g2_ref, gain_ref, y_ref):
    shared = _dot(_swiglu(h_ref[...], sg_ref[...], su_ref[...]).astype(BF16), sd_ref[...].astype(BF16))
    f = _from_token_rows(fr_ref, x1_ref.shape[0]) + shared
    y_ref[...] = x1_ref[...] + g2_ref[0] * _rmsnorm(f, gain_ref[...])


def _final_grouped(x1, fr, h2b, sg, su, sd, g2, gain, *, tm, fr_index, mod_index):
    t = x1.shape[0]
    n_chunks = D_MODEL // LANES
    row = pl.BlockSpec((tm, D_MODEL), lambda i: (i, 0))
    full = lambda shape: pl.BlockSpec(shape, lambda i: (0,) * len(shape))
    return pl.pallas_call(
        _final_grouped_kernel,
        grid=(t // tm,),
        in_specs=[row,
                  pl.BlockSpec((1, tm * n_chunks, LANES), lambda i: (*fr_index(i), 0)),
                  row, full(sg.shape), full(su.shape), full(sd.shape),
                  pl.BlockSpec((1, g2.shape[1], D_MODEL), lambda i: (mod_index(i), 0, 0)),
                  pl.BlockSpec((1, D_MODEL), lambda i: (0, 0))],
        out_specs=row,
        out_shape=jax.ShapeDtypeStruct((t, D_MODEL), F32),
        compiler_params=pltpu.CompilerParams(dimension_semantics=("parallel",), vmem_limit_bytes=VMEM_LIMIT),
        name="final_grouped",
    )(x1, fr, h2b, sg, su, sd, g2, gain)


def _rope_tables(pos):
    half = HEAD_DIM // 2
    inv_freq = ROPE_THETA ** (-jnp.arange(half, dtype=F32) / half)
    ang = pos.astype(F32)[:, None] * inv_freq[None, :]
    cos, sin = jnp.cos(ang), jnp.sin(ang)
    reps = LANES // HEAD_DIM
    return jnp.tile(jnp.concatenate([cos, cos], axis=1), (1, reps)), jnp.tile(jnp.concatenate([-sin, sin], axis=1), (1, reps))


def kernel(x_prompt, x_sample, c_prompt, c_sample, cache_k, cache_v, state_pool, page_table, ada_w, ada_b, pre_mix_g, post_mix_g, pre_ffn_g, post_ffn_g, w_in, w_out, lambda_q1, lambda_k1, lambda_q2, lambda_k2, subln_g, pool_w, pool_scale, router_w, router_bias, expert_w_gate, expert_w_up, expert_w_down, shared_w_gate, shared_w_up, shared_w_down):
    depth = ada_w.shape[0]
    assert depth == 1, "single-layer step"
    nb, seq, _ = x_prompt.shape
    db, dec_seq, _ = x_sample.shape
    assert dec_seq == 1
    page_size = cache_k.shape[2]
    n_pages = page_table.shape[1]
    past_len = n_pages * page_size
    lam_init = 0.8 - 0.6 * math.exp(-0.3 * 0)
    l = 0

    c_all = jnp.concatenate([c_prompt, c_sample], axis=0)
    pad = (-c_all.shape[0]) % 16
    mod = _ada(jnp.pad(c_all, ((0, pad), (0, 0))), ada_w[l], ada_b[l][None, :])
    mod_p = [m.reshape(nb, 1, D_MODEL) for m in jnp.split(mod[:nb], 6, axis=-1)]
    mod_s = [m.reshape(1, db, D_MODEL) for m in jnp.split(mod[nb:nb + db], 6, axis=-1)]

    w_in_b = w_in[l].astype(BF16)
    w_out_b = w_out[l].astype(BF16)
    lam4 = jnp.stack([lambda_q1[l], lambda_k1[l], lambda_q2[l], lambda_k2[l]])
    row = lambda v: v.reshape(1, -1)
    rwt = router_w[l].T
    rb_col = router_bias[l].reshape(N_EXPERTS, 1)
    experts = (expert_w_gate[l], expert_w_up[l], expert_w_down[l], shared_w_gate[l], shared_w_up[l], shared_w_down[l])

    t_p = nb * seq
    tm = TM_TOKEN
    seq_tiles = seq // tm
    cos_p, sin_p = _rope_tables(jnp.arange(seq))
    xp = x_prompt.reshape(t_p, D_MODEL)
    k_p, v_p, z_p, qb, kb, vt = _premix(xp, mod_p[0], mod_p[1], row(pre_mix_g[l]), w_in_b, cos_p, sin_p,
                                        tm=tm, rows_per_mod=seq_tiles, seq_tiles=seq_tiles, tk=TK_ATTN)
    o_p = _attn_prompt(qb.reshape(nb, seq, ATTN_W), kb.reshape(nb, seq, ATTN_W), vt, lam4,
                       subln_g[l].reshape(V_DIM, 1), tq=TQ_ATTN, tk=TK_ATTN, hps=HEADS_PER_STEP,
                       lam_init=lam_init)
    x1_p, h2_p, idx_p, w_p, h2r, cnt_p = _postmix(o_p.reshape(t_p, ATTN_W), z_p, z_p, xp, mod_p[2], mod_p[3],
                                                  mod_p[4], w_out_b, pool_w[l], row(pool_scale[l]),
                                                  row(post_mix_g[l]), row(pre_ffn_g[l]), rwt, rb_col,
                                                  tm=tm, rows_per_mod=seq_tiles, seq_tiles=seq_tiles, past_len=None)

    cos_s, sin_s = _rope_tables(jnp.full((db,), past_len))
    xs = x_sample.reshape(db, D_MODEL)
    k_s, v_s, z_s, qb_s, kb_s = _premix(xs, mod_s[0], mod_s[1], row(pre_mix_g[l]), w_in_b, cos_s, sin_s,
                                        tm=db, rows_per_mod=1, seq_tiles=1, tk=0)
    page_rows = page_size * N_HEADS
    k2 = cache_k[l].reshape(-1, V_DIM)
    v2 = cache_v[l].reshape(-1, V_DIM)
    o_s = _attn_decode(page_table, qb_s.reshape(db, N_HEADS, V_DIM), kb_s.reshape(db, N_HEADS, V_DIM),
                       v_s.reshape(db, N_HEADS, V_DIM), lam4, row(subln_g[l]), k2, v2,
                       page_rows=page_rows, lam_init=lam_init)
    hist_s = jnp.transpose(state_pool[l], (1, 0, 2))
    x1_s, h2_s, idx_s, w_s, h2r_s, cnt_s = _postmix(o_s.reshape(db, ATTN_W), z_s, hist_s, xs, mod_s[2], mod_s[3],
                                                    mod_s[4], w_out_b, pool_w[l], row(pool_scale[l]),
                                                    row(post_mix_g[l]), row(pre_ffn_g[l]), rwt, rb_col,
                                                    tm=db, rows_per_mod=1, seq_tiles=1, past_len=past_len)

    per_range = lambda c: jnp.sum(c.reshape(nb, -1, N_EXPERTS), axis=1).astype(jnp.int32)
    first_sub, n_sub, src8, dst8, wl = _dispatch_tables(idx_p, w_p, per_range(cnt_p), idx_s, w_s,
                                                        cnt_s.reshape(N_EXPERTS).astype(jnp.int32), nb, seq, MOE_BM)
    fr = _moe_sorted(first_sub, n_sub, src8, dst8, wl, h2r, h2r_s, *experts[:3], bm=MOE_BM)
    final_tiles = seq // TM_FINAL
    y_p = _final_grouped(x1_p, fr, h2_p, *experts[3:], mod_p[5], row(post_ffn_g[l]), tm=TM_FINAL,
                         fr_index=lambda i: (i // final_tiles, i % final_tiles), mod_index=lambda i: i // final_tiles)
    y_s = _final_grouped(x1_s, fr, h2_s, *experts[3:], mod_s[5], row(post_ffn_g[l]), tm=db,
                         fr_index=lambda i: (nb - 1, seq // db), mod_index=lambda i: 0)

    pool_p = z_p.reshape(nb, seq, POOL_W)[:, seq - POOL_BUF:]
    pool_s = jnp.concatenate([state_pool[l][:, 1:], z_s[:, None, :]], axis=1)
    return (y_p.reshape(nb, seq, D_MODEL), y_s.reshape(db, 1, D_MODEL),
            k_p.reshape(1, nb, seq, N_HEADS, V_DIM), v_p.reshape(1, nb, seq, N_HEADS, V_DIM), pool_p[None],
            k_s.reshape(1, db, 1, N_HEADS, V_DIM), v_s.reshape(1, db, 1, N_HEADS, V_DIM), pool_s[None])
```

```python
import functools
import math

import jax
import jax.numpy as jnp
from jax import lax
from jax.experimental import pallas as pl
from jax.experimental.pallas import tpu as pltpu

F32 = jnp.float32
BF16 = jnp.bfloat16

D_MODEL = 1024
ATTN_W = 512
POOL_W = 512
HEAD_DIM = 64
N_HEADS = 4
V_DIM = 2 * HEAD_DIM
IN_COLS = 3 * ATTN_W + POOL_W
POOL_WINDOWS = (2, 4, 8, 16)
POOL_GW = 128
POOL_BUF = 15
POOL_HALO = 16
ROPE_THETA = 10000.0
N_EXPERTS = 64
TOP_K = 8
N_EXPERT_GROUPS = 8
GROUP_SIZE = N_EXPERTS // N_EXPERT_GROUPS
TOPK_GROUPS = 4
EXPERT_DIM = 256
ROUTED_SCALE = 2.5
EPS = 1e-6
LANES = 128
NEG_INF = float("-inf")
Q_SCALE = HEAD_DIM ** -0.5 * math.log2(math.e)
VMEM_LIMIT = 56 * 1024 * 1024

TM_TOKEN = 512
TQ_ATTN = 512
TK_ATTN = 256
HEADS_PER_STEP = 4
MOE_BM = 128
SCATTER_BATCH = 8
TM_FINAL = 512
DEC_CHUNK_PAGES = 32


def _silu(x):
    return x * jax.nn.sigmoid(x)


def _rmsnorm(x, g):
    ms = jnp.mean(x * x, axis=-1, keepdims=True)
    return x * lax.rsqrt(ms + EPS) * g


def _dot(a, b):
    return jnp.dot(a, b, preferred_element_type=F32)


def _dot_nt(a, b):
    return lax.dot_general(a, b, (((1,), (1,)), ((), ())), preferred_element_type=F32)


def _ada_kernel(c_ref, w_ref, b_ref, o_ref):
    s = _silu(c_ref[...])
    o_ref[...] = _dot(s.astype(BF16), w_ref[...].astype(BF16)) + b_ref[...]


def _ada(c_pad, w, b):
    rows, n = c_pad.shape[0], w.shape[1]
    tn = 1536
    return pl.pallas_call(
        _ada_kernel,
        grid=(n // tn,),
        in_specs=[pl.BlockSpec((rows, D_MODEL), lambda j: (0, 0)),
                  pl.BlockSpec((D_MODEL, tn), lambda j: (0, j)),
                  pl.BlockSpec((1, tn), lambda j: (0, j))],
        out_specs=pl.BlockSpec((rows, tn), lambda j: (0, j)),
        out_shape=jax.ShapeDtypeStruct((rows, n), F32),
        compiler_params=pltpu.CompilerParams(dimension_semantics=("arbitrary",), vmem_limit_bytes=VMEM_LIMIT),
        name="ada_mod",
    )(c_pad, w, b)


def _rope(x, cos, sin_signed):
    outs = []
    for c in range(x.shape[1] // LANES):
        blk = x[:, c * LANES:(c + 1) * LANES]
        lane = lax.broadcasted_iota(jnp.int32, blk.shape, 1)
        partner = jnp.where((lane % HEAD_DIM) < HEAD_DIM // 2,
                            pltpu.roll(blk, LANES - HEAD_DIM // 2, 1),
                            pltpu.roll(blk, HEAD_DIM // 2, 1))
        outs.append(blk * cos + partner * sin_signed)
    return jnp.concatenate(outs, axis=1)


def _premix_kernel(x_ref, sh_ref, sc_ref, g_ref, w_ref, cos_ref, sin_ref,
                   k_ref, v_ref, z_ref, qb_ref, kb_ref, *rest, n_kv):
    h = _rmsnorm(x_ref[...], g_ref[...]) * (1.0 + sc_ref[0]) + sh_ref[0]
    proj = _dot(h.astype(BF16), w_ref[...])
    cos, sin = cos_ref[...], sin_ref[...]
    q = _rope(proj[:, :ATTN_W], cos, sin)
    k = _rope(proj[:, ATTN_W:2 * ATTN_W], cos, sin)
    v = proj[:, 2 * ATTN_W:3 * ATTN_W]
    for out_ref, val in ((k_ref, k), (v_ref, v)):
        for i in range(val.shape[0] // 8):
            for hh in range(N_HEADS):
                out_ref[pl.ds(i * 8 * N_HEADS + hh, 8, stride=N_HEADS), :] = val[8 * i:8 * i + 8, hh * V_DIM:(hh + 1) * V_DIM]
    z_ref[...] = proj[:, 3 * ATTN_W:]
    qb_ref[...] = (q * Q_SCALE).astype(BF16)
    kb_ref[...] = k.astype(BF16)
    if n_kv:
        vt_ref = rest[0]
        vt = v.T.astype(BF16)
        tk = vt.shape[1] // n_kv
        for hh in range(N_HEADS):
            for c in range(n_kv):
                vt_ref[0, hh, c] = vt[hh * V_DIM:(hh + 1) * V_DIM, c * tk:(c + 1) * tk]


def _premix(x, shift, scale, gain, w_in_b, cos, sin, *, tm, rows_per_mod, seq_tiles, tk):
    t = x.shape[0]
    n_tiles = t // tm
    mod_rows = shift.shape[1]
    mod_idx = (lambda i: (i // rows_per_mod, 0, 0))
    n_kv = tm // tk if tk else 0
    out_shape = [jax.ShapeDtypeStruct((t * N_HEADS, V_DIM), F32), jax.ShapeDtypeStruct((t * N_HEADS, V_DIM), F32),
                 jax.ShapeDtypeStruct((t, POOL_W), F32), jax.ShapeDtypeStruct((t, ATTN_W), BF16),
                 jax.ShapeDtypeStruct((t, ATTN_W), BF16)]
    row_spec = pl.BlockSpec((tm, ATTN_W), lambda i: (i, 0))
    head_rows = pl.BlockSpec((tm * N_HEADS, V_DIM), lambda i: (i, 0))
    out_specs = [head_rows, head_rows, row_spec, row_spec, row_spec]
    if n_kv:
        nb = n_tiles // seq_tiles
        out_shape.append(jax.ShapeDtypeStruct((nb, N_HEADS, seq_tiles * n_kv, V_DIM, tk), BF16))
        out_specs.append(pl.BlockSpec((1, N_HEADS, n_kv, V_DIM, tk),
                                      lambda i: (i // seq_tiles, 0, i % seq_tiles, 0, 0)))
    return pl.pallas_call(
        functools.partial(_premix_kernel, n_kv=n_kv),
        grid=(n_tiles,),
        in_specs=[pl.BlockSpec((tm, D_MODEL), lambda i: (i, 0)),
                  pl.BlockSpec((1, mod_rows, D_MODEL), mod_idx),
                  pl.BlockSpec((1, mod_rows, D_MODEL), mod_idx),
                  pl.BlockSpec((1, D_MODEL), lambda i: (0, 0)),
                  pl.BlockSpec((D_MODEL, IN_COLS), lambda i: (0, 0)),
                  pl.BlockSpec((tm, LANES), lambda i: (i % seq_tiles, 0)),
                  pl.BlockSpec((tm, LANES), lambda i: (i % seq_tiles, 0))],
        out_specs=out_specs,
        out_shape=out_shape,
        compiler_params=pltpu.CompilerParams(dimension_semantics=("parallel",), vmem_limit_bytes=VMEM_LIMIT),
        name="premix",
    )(x, shift, scale, gain, w_in_b, cos, sin)


def _diff_lambda(lam_ref, lam_init):
    lam = lam_ref[...]
    s1 = jnp.sum(lam[0:1] * lam[1:2], axis=1, keepdims=True)
    s2 = jnp.sum(lam[2:3] * lam[3:4], axis=1, keepdims=True)
    return jnp.exp(s1) - jnp.exp(s2) + lam_init


def _split_maps(q):
    lane = lax.broadcasted_iota(jnp.int32, q.shape, 1)
    zero = jnp.zeros_like(q)
    return jnp.concatenate([jnp.where(lane < HEAD_DIM, q, zero), jnp.where(lane >= HEAD_DIM, q, zero)], axis=0)


def _attn_kernel(q_ref, k_ref, vt_ref, lam_ref, g_ref, o_ref, s_even, s_odd, acc_ref, *, tq, tk, hps, lam_init):
    i = pl.program_id(2)
    r = tq // tk
    heads = range(hps)
    cols = [slice(hh * V_DIM, (hh + 1) * V_DIM) for hh in heads]
    qpads = [_split_maps(q_ref[0, :, cols[hh]]) for hh in heads]

    def logits(j, buf):
        rows = pl.ds(pl.multiple_of(j * tk, tk), tk)
        for hh in heads:
            buf[hh] = _dot_nt(k_ref[0, rows, cols[hh]], qpads[hh])

    def consume(j, buf, carry, masked):
        def read(hh):
            s = buf[hh]
            if masked:
                kpos = j * tk + lax.broadcasted_iota(jnp.int32, s.shape, 0)
                qpos = i * tq + lax.broadcasted_iota(jnp.int32, s.shape, 1) % tq
                s = jnp.where(kpos <= qpos, s, NEG_INF)
            return s

        ps, alphas, out = [], [], []
        for hh in heads:
            m, l = carry[hh]
            m_new = jnp.maximum(m, jnp.max(read(hh), axis=0, keepdims=True))
            alpha = jnp.exp2(m - m_new)
            p = jnp.exp2(read(hh) - m_new)
            out.append((m_new, alpha * l + jnp.sum(p, axis=0, keepdims=True)))
            alphas.append(alpha)
            ps.append(p.astype(BF16))
        for hh in heads:
            acc_ref[hh] = alphas[hh] * acc_ref[hh] + _dot(vt_ref[0, hh, j], ps[hh])
        return tuple(out)

    def pair(p, carry):
        j = 2 * p
        logits(j + 1, s_odd)
        carry = consume(j, s_even, carry, False)
        logits(j + 2, s_even)
        return consume(j + 1, s_odd, carry, False)

    assert r == 2, "the written-out schedule handles two key blocks per query block"
    init = tuple((jnp.full((1, 2 * tq), -1e30, F32), jnp.zeros((1, 2 * tq), F32)) for _ in heads)
    acc_ref[...] = jnp.zeros(acc_ref.shape, F32)
    logits(0, s_even)
    carry = lax.fori_loop(0, i, pair, init)
    logits(2 * i + 1, s_odd)
    carry = consume(2 * i, s_even, carry, True)
    carry = consume(2 * i + 1, s_odd, carry, True)
    lam = _diff_lambda(lam_ref, lam_init)
    for hh in heads:
        o = acc_ref[hh] / carry[hh][1]
        od = o[:, :tq] - lam * o[:, tq:]
        ms = jnp.mean(od * od, axis=0, keepdims=True)
        on = od * lax.rsqrt(ms + EPS) * g_ref[...] * (1.0 - lam_init)
        o_ref[0, :, cols[hh]] = on.T.astype(BF16)


def _attn_prompt(qb, kb, vt, lam4, subln_col, *, tq, tk, hps, lam_init):
    b, s, _ = qb.shape
    nq, nkv = s // tq, s // tk
    w = hps * V_DIM
    return pl.pallas_call(
        functools.partial(_attn_kernel, tq=tq, tk=tk, hps=hps, lam_init=lam_init),
        grid=(b, N_HEADS // hps, nq),
        in_specs=[pl.BlockSpec((1, tq, w), lambda bb, h, i: (bb, i, h)),
                  pl.BlockSpec((1, s, w), lambda bb, h, i: (bb, 0, h)),
                  pl.BlockSpec((1, hps, nkv, V_DIM, tk), lambda bb, h, i: (bb, h, 0, 0, 0)),
                  pl.BlockSpec((4, HEAD_DIM), lambda bb, h, i: (0, 0)),
                  pl.BlockSpec((V_DIM, 1), lambda bb, h, i: (0, 0))],
        out_specs=pl.BlockSpec((1, tq, w), lambda bb, h, i: (bb, i, h)),
        out_shape=jax.ShapeDtypeStruct((b, s, ATTN_W), BF16),
        scratch_shapes=[pltpu.VMEM((hps, tk, 2 * tq), F32), pltpu.VMEM((hps, tk, 2 * tq), F32),
                        pltpu.VMEM((hps, V_DIM, 2 * tq), F32)],
        compiler_params=pltpu.CompilerParams(dimension_semantics=("parallel", "parallel", "arbitrary"),
                                             vmem_limit_bytes=VMEM_LIMIT),
        name="attn_prompt",
    )(qb, kb, vt, lam4, subln_col)


def _decode_kernel(pt_ref, q_ref, kn_ref, vn_ref, lam_ref, g_ref, k_hbm, v_hbm, o_ref,
                   kbuf, vbuf, sem, m_sc, l_sc, acc_sc, *, n_chunks, chunk_pages, page_rows, lam_init):
    b = pl.program_id(0)
    c = pl.program_id(1)
    nb = pl.num_programs(0)
    step = b * n_chunks + c
    slot = step % 2

    def chunk_copies(bb, cc, sl):
        copies = []
        for p in range(chunk_pages):
            phys = pt_ref[bb * (n_chunks * chunk_pages) + cc * chunk_pages + p]
            src = pl.ds(pl.multiple_of(phys * page_rows, page_rows), page_rows)
            dst = pl.ds(p * page_rows, page_rows)
            copies.append(pltpu.make_async_copy(k_hbm.at[src, :], kbuf.at[sl, dst, :], sem.at[0, sl]))
            copies.append(pltpu.make_async_copy(v_hbm.at[src, :], vbuf.at[sl, dst, :], sem.at[1, sl]))
        return copies

    @pl.when(step == 0)
    def _():
        for cp in chunk_copies(b, c, slot):
            cp.start()

    @pl.when(step + 1 < nb * n_chunks)
    def _():
        nxt = step + 1
        for cp in chunk_copies(nxt // n_chunks, nxt % n_chunks, 1 - slot):
            cp.start()

    @pl.when(c == 0)
    def _():
        m_sc[...] = jnp.full(m_sc.shape, -1e30, F32)
        l_sc[...] = jnp.zeros(l_sc.shape, F32)
        acc_sc[...] = jnp.zeros(acc_sc.shape, F32)

    qblk = _split_maps(q_ref[0])
    for cp in chunk_copies(b, c, slot):
        cp.wait()
    kc = kbuf[slot].astype(BF16)
    vc = vbuf[slot].astype(BF16)
    s = _dot_nt(qblk, kc)
    row_head = lax.broadcasted_iota(jnp.int32, s.shape, 0) % N_HEADS
    col_head = lax.broadcasted_iota(jnp.int32, s.shape, 1) % N_HEADS
    s = jnp.where(row_head == col_head, s, NEG_INF)
    m_old = m_sc[...]
    m_new = jnp.maximum(m_old, jnp.max(s, axis=1, keepdims=True))
    alpha = jnp.exp2(m_old - m_new)
    p = jnp.exp2(s - m_new)
    l_new = alpha * l_sc[...] + jnp.sum(p, axis=1, keepdims=True)
    acc_new = alpha * acc_sc[...] + _dot(p.astype(BF16), vc)
    m_sc[...] = m_new
    l_sc[...] = l_new
    acc_sc[...] = acc_new

    @pl.when(c == n_chunks - 1)
    def _():
        kn = jnp.concatenate([kn_ref[0], kn_ref[0]], axis=0).astype(F32)
        vn = jnp.concatenate([vn_ref[0], vn_ref[0]], axis=0).astype(F32)
        s_n = jnp.sum(qblk.astype(F32) * kn, axis=1, keepdims=True)
        m_f = jnp.maximum(m_new, s_n)
        a_f = jnp.exp2(m_new - m_f)
        p_n = jnp.exp2(s_n - m_f)
        l_f = a_f * l_new + p_n
        o = (a_f * acc_new + p_n * vn) / l_f
        od = o[:N_HEADS] - _diff_lambda(lam_ref, lam_init) * o[N_HEADS:]
        o_ref[0] = (_rmsnorm(od, g_ref[...]) * (1.0 - lam_init)).astype(BF16)


def _attn_decode(page_table, q3, kn3, vn3, lam4, subln_row, k2, v2, *, page_rows, lam_init):
    nb, n_pages = page_table.shape
    chunk_pages = DEC_CHUNK_PAGES
    n_chunks = n_pages // chunk_pages
    rows = chunk_pages * page_rows
    head_spec = pl.BlockSpec((1, N_HEADS, V_DIM), lambda b, c, pt: (b, 0, 0))
    grid_spec = pltpu.PrefetchScalarGridSpec(
        num_scalar_prefetch=1,
        grid=(nb, n_chunks),
        in_specs=[head_spec, head_spec, head_spec,
                  pl.BlockSpec((4, HEAD_DIM), lambda b, c, pt: (0, 0)),
                  pl.BlockSpec((1, V_DIM), lambda b, c, pt: (0, 0)),
                  pl.BlockSpec(memory_space=pl.ANY),
                  pl.BlockSpec(memory_space=pl.ANY)],
        out_specs=head_spec,
        scratch_shapes=[pltpu.VMEM((2, rows, V_DIM), F32), pltpu.VMEM((2, rows, V_DIM), F32),
                        pltpu.SemaphoreType.DMA((2, 2)),
                        pltpu.VMEM((2 * N_HEADS, 1), F32), pltpu.VMEM((2 * N_HEADS, 1), F32),
                        pltpu.VMEM((2 * N_HEADS, V_DIM), F32)])
    return pl.pallas_call(
        functools.partial(_decode_kernel, n_chunks=n_chunks, chunk_pages=chunk_pages,
                          page_rows=page_rows, lam_init=lam_init),
        grid_spec=grid_spec,
        out_shape=jax.ShapeDtypeStruct((nb, N_HEADS, V_DIM), BF16),
        compiler_params=pltpu.CompilerParams(dimension_semantics=("arbitrary", "arbitrary"),
                                             vmem_limit_bytes=VMEM_LIMIT),
        name="attn_decode",
    )(page_table.reshape(-1), q3, kn3, vn3, lam4, subln_row, k2, v2)


def _pool_project(d_groups, pw_ref, ps_ref):
    ys = [_dot(d.astype(BF16), pw_ref[g].astype(BF16)) for g, d in enumerate(d_groups)]
    return jnp.concatenate(ys, axis=1) * ps_ref[...]


def _argmax_first(x, iota, n):
    mx = jnp.max(x, axis=0, keepdims=True)
    ix = jnp.min(jnp.where(x == mx, iota, float(n)), axis=0, keepdims=True)
    return mx, ix


def _row_iota(shape):
    return lax.broadcasted_iota(jnp.int32, shape, 0).astype(F32)


def _route(h2, rwt_ref, rb_ref):
    rw = rwt_ref[...]
    rw_hi = rw.astype(BF16)
    rw_lo = (rw - rw_hi.astype(F32)).astype(BF16)
    h_hi = h2.astype(BF16)
    h_lo = (h2 - h_hi.astype(F32)).astype(BF16)
    logits = _dot_nt(rw_hi, h_hi) + (_dot_nt(rw_hi, h_lo) + _dot_nt(rw_lo, h_hi))
    scores = jax.nn.sigmoid(logits)
    sel = scores + rb_ref[...]
    tm = sel.shape[1]
    iota_g = _row_iota((GROUP_SIZE, tm))
    group_rows = []
    for g in range(N_EXPERT_GROUPS):
        xg = sel[g * GROUP_SIZE:(g + 1) * GROUP_SIZE]
        m1, i1 = _argmax_first(xg, iota_g, GROUP_SIZE)
        m2 = jnp.max(jnp.where(iota_g == i1, NEG_INF, xg), axis=0, keepdims=True)
        group_rows.append(m1 + m2)
    gscore = jnp.concatenate(group_rows, axis=0)
    iota_gr = _row_iota(gscore.shape)
    gmask = jnp.zeros(gscore.shape, F32)
    for _ in range(TOPK_GROUPS):
        _, ig = _argmax_first(gscore, iota_gr, N_EXPERT_GROUPS)
        hit = iota_gr == ig
        gmask = jnp.where(hit, 1.0, gmask)
        gscore = jnp.where(hit, NEG_INF, gscore)
    masked = jnp.concatenate(
        [jnp.where(gmask[g:g + 1] > 0.5, sel[g * GROUP_SIZE:(g + 1) * GROUP_SIZE], NEG_INF)
         for g in range(N_EXPERT_GROUPS)], axis=0)
    iota_e = _row_iota(masked.shape)
    w = jnp.zeros(masked.shape, F32)
    chosen = jnp.zeros(masked.shape, F32)
    idx_rows, score_rows = [], []
    for _ in range(TOP_K):
        _, ie = _argmax_first(masked, iota_e, N_EXPERTS)
        hit = iota_e == ie
        w = jnp.where(hit, scores, w)
        chosen = jnp.where(hit, 1.0, chosen)
        masked = jnp.where(hit, NEG_INF, masked)
        idx_rows.append(ie)
        score_rows.append(jnp.sum(jnp.where(hit, scores, 0.0), axis=0, keepdims=True))
    wsum = jnp.sum(w, axis=0, keepdims=True)
    idx_t = jnp.concatenate(idx_rows, axis=0).astype(jnp.int32)
    w_t = jnp.concatenate(score_rows, axis=0) / wsum * ROUTED_SCALE
    return idx_t, w_t, jnp.sum(chosen, axis=1, keepdims=True)


def _to_token_rows(x, out_ref):
    n_chunks = x.shape[1] // LANES
    for i in range(x.shape[0] // 8):
        for j in range(n_chunks):
            out_ref[0, pl.ds(i * 8 * n_chunks + j, 8, stride=n_chunks), :] = x[8 * i:8 * i + 8, j * LANES:(j + 1) * LANES]


def _from_token_rows(in_ref, tm):
    n_chunks = D_MODEL // LANES
    return jnp.concatenate([in_ref[0, pl.ds(j, tm, stride=n_chunks), :] for j in range(n_chunks)], axis=1)


def _mix_tail(o_b, pz, x, g1, sh2, sc2, wout_ref, gpost_ref, gpre_ref, rwt_ref, rb_ref,
              x1_ref, h2_ref, idx_ref, w_ref, h2r_ref, cnt_ref):
    m = _dot(o_b, wout_ref[:ATTN_W, :]) + _dot(pz.astype(BF16), wout_ref[ATTN_W:, :])
    x1 = x + g1 * _rmsnorm(m, gpost_ref[...])
    h2 = _rmsnorm(x1, gpre_ref[...]) * (1.0 + sc2) + sh2
    x1_ref[...] = x1
    h2_ref[...] = h2.astype(BF16)
    idx_ref[...], w_ref[...], cnt_ref[0] = _route(h2, rwt_ref, rb_ref)
    _to_token_rows(h2, h2r_ref)


def _postmix_prompt_kernel(o_ref, z_ref, zprev_ref, x_ref, g1_ref, sh2_ref, sc2_ref,
                           wout_ref, pw_ref, ps_ref, gpost_ref, gpre_ref, rwt_ref, rb_ref,
                           x1_ref, h2_ref, idx_ref, w_ref, h2r_ref, cnt_ref, zbuf, *, seq_tiles):
    tm = z_ref.shape[0]
    it = pl.program_id(0) % seq_tiles
    zt = z_ref[...]
    zbuf[0:POOL_HALO, :] = jnp.where(it == 0, 0.0, zprev_ref[...])
    zbuf[POOL_HALO:, :] = zt
    pos = it * tm + lax.broadcasted_iota(jnp.int32, (tm, 1), 0)
    d_groups = []
    for g, w in enumerate(POOL_WINDOWS):
        cols = slice(g * POOL_GW, (g + 1) * POOL_GW)
        win = zt[:, cols]
        for s in range(1, w):
            win = win + zbuf[POOL_HALO - s:POOL_HALO - s + tm, cols]
        cnt = jnp.minimum(pos + 1, w).astype(F32)
        d_groups.append(win / cnt - zt[:, cols])
    pz = _pool_project(d_groups, pw_ref, ps_ref)
    _mix_tail(o_ref[...], pz, x_ref[...], g1_ref[0], sh2_ref[0], sc2_ref[0], wout_ref, gpost_ref, gpre_ref,
              rwt_ref, rb_ref, x1_ref, h2_ref, idx_ref, w_ref, h2r_ref, cnt_ref)


def _postmix_sample_kernel(o_ref, z_ref, st_ref, x_ref, g1_ref, sh2_ref, sc2_ref,
                           wout_ref, pw_ref, ps_ref, gpost_ref, gpre_ref, rwt_ref, rb_ref,
                           x1_ref, h2_ref, idx_ref, w_ref, h2r_ref, cnt_ref, *, past_len):
    zt = z_ref[...]
    d_groups = []
    for g, w in enumerate(POOL_WINDOWS):
        cols = slice(g * POOL_GW, (g + 1) * POOL_GW)
        win = zt[:, cols]
        for s in range(1, w):
            win = win + st_ref[POOL_BUF - s][:, cols]
        d_groups.append(win / float(min(past_len + 1, w)) - zt[:, cols])
    pz = _pool_project(d_groups, pw_ref, ps_ref)
    _mix_tail(o_ref[...], pz, x_ref[...], g1_ref[0], sh2_ref[0], sc2_ref[0], wout_ref, gpost_ref, gpre_ref,
              rwt_ref, rb_ref, x1_ref, h2_ref, idx_ref, w_ref, h2r_ref, cnt_ref)


def _postmix(o, z, hist, x, g1, sh2, sc2, wout_b, pool_w, pool_scale, g_post, g_pre, rwt, rb_col,
             *, tm, rows_per_mod, seq_tiles, past_len):
    t = x.shape[0]
    n_tiles = t // tm
    mod_rows = g1.shape[1]
    rows8 = D_MODEL // LANES
    mod_spec = pl.BlockSpec((1, mod_rows, D_MODEL), lambda i: (i // rows_per_mod, 0, 0))
    full = lambda shape: pl.BlockSpec(shape, lambda i: (0,) * len(shape))
    if past_len is None:
        body = functools.partial(_postmix_prompt_kernel, seq_tiles=seq_tiles)
        halo_blocks = tm // POOL_HALO
        hist_spec = pl.BlockSpec((POOL_HALO, POOL_W), lambda i: (jnp.maximum(i * halo_blocks - 1, 0), 0))
        scratch = [pltpu.VMEM((POOL_HALO + tm, POOL_W), F32)]
    else:
        body = functools.partial(_postmix_sample_kernel, past_len=past_len)
        hist_spec = full(hist.shape)
        scratch = []
    return pl.pallas_call(
        body,
        grid=(n_tiles,),
        in_specs=[pl.BlockSpec((tm, ATTN_W), lambda i: (i, 0)),
                  pl.BlockSpec((tm, POOL_W), lambda i: (i, 0)),
                  hist_spec,
                  pl.BlockSpec((tm, D_MODEL), lambda i: (i, 0)),
                  mod_spec, mod_spec, mod_spec,
                  full((D_MODEL, D_MODEL)), full(pool_w.shape), full((1, POOL_W)),
                  full((1, D_MODEL)), full((1, D_MODEL)), full((N_EXPERTS, D_MODEL)), full((N_EXPERTS, 1))],
        out_specs=[pl.BlockSpec((tm, D_MODEL), lambda i: (i, 0)),
                   pl.BlockSpec((tm, D_MODEL), lambda i: (i, 0)),
                   pl.BlockSpec((TOP_K, tm), lambda i: (0, i)), pl.BlockSpec((TOP_K, tm), lambda i: (0, i)),
                   pl.BlockSpec((1, tm * rows8, LANES), lambda i: (i // seq_tiles, i % seq_tiles, 0)),
                   pl.BlockSpec((1, N_EXPERTS, 1), lambda i: (i, 0, 0))],
        out_shape=[jax.ShapeDtypeStruct((t, D_MODEL), F32), jax.ShapeDtypeStruct((t, D_MODEL), BF16),
                   jax.ShapeDtypeStruct((TOP_K, t), jnp.int32), jax.ShapeDtypeStruct((TOP_K, t), F32),
                   jax.ShapeDtypeStruct((n_tiles // seq_tiles, seq_tiles * tm * rows8, LANES), F32),
                   jax.ShapeDtypeStruct((n_tiles, N_EXPERTS, 1), F32)],
        scratch_shapes=scratch,
        compiler_params=pltpu.CompilerParams(dimension_semantics=("parallel",), vmem_limit_bytes=VMEM_LIMIT),
        name="postmix",
    )(o, z, hist, x, g1, sh2, sc2, wout_b, pool_w, pool_scale, g_post, g_pre, rwt, rb_col)


def _swiglu(xb, wg, wu):
    return _silu(_dot(xb, wg.astype(BF16))) * _dot(xb, wu.astype(BF16))


def _list_window(range_slots, bm):
    return (-(-range_slots // bm) + 3) * bm


def _dispatch_tables(idx_p, w_p, cnt_p, idx_x, w_x, cnt_x, n_ranges, rtok, bm):
    k, extra = idx_x.shape
    slots = rtok + extra
    per_p, per_x = k * rtok, k * extra
    i32 = jnp.int32
    e_p = idx_p.reshape(k, n_ranges, rtok).transpose(1, 0, 2).reshape(n_ranges, per_p)
    keys_p = jnp.sort(e_p * per_p + jnp.arange(per_p, dtype=i32), axis=1)
    keys_x = jnp.sort(idx_x.reshape(per_x) * per_x + jnp.arange(per_x, dtype=i32))
    cnt_x = jnp.where((jnp.arange(n_ranges) == n_ranges - 1)[:, None], cnt_x[None, :], 0)
    start_p = jnp.cumsum(cnt_p, axis=1) - cnt_p
    start_x = jnp.cumsum(cnt_x, axis=1) - cnt_x
    n_sub = (cnt_p + cnt_x + bm - 1) // bm
    first_sub = jnp.cumsum(n_sub, axis=1) - n_sub
    tot_sub = 1 + -(-(per_p + per_x) // bm) + N_EXPERTS
    sub = jnp.arange(tot_sub, dtype=i32) - 1
    begun = sub[None, :, None] >= first_sub[:, None, :]
    at_begun = lambda a: jnp.max(jnp.where(begun, a[:, None, :], 0), axis=2)
    at_next = lambda a, end: jnp.min(jnp.where(begun, end, a[:, None, :]), axis=2)
    lo_p, hi_p = at_begun(start_p), at_next(start_p, per_p)
    lo_x, hi_x = at_begun(start_x), at_next(start_x, jnp.sum(cnt_x, axis=1)[:, None, None])
    pos = ((sub - at_begun(first_sub)) * bm)[:, :, None] + jnp.arange(bm, dtype=i32)
    n_p = (hi_p - lo_p)[:, :, None]
    flat = lambda a: a.reshape(n_ranges, tot_sub * bm)
    in_p = flat(pos < n_p)
    valid = flat(jnp.logical_and(pos < n_p + (hi_x - lo_x)[:, :, None], (sub >= 0)[None, :, None]))
    src = flat(jnp.where(pos < n_p, lo_p[:, :, None] + pos, per_p + lo_x[:, :, None] + pos - n_p))
    keys = jnp.concatenate([keys_p, jnp.broadcast_to(keys_x, (n_ranges, per_x))], axis=1)
    key = jnp.take_along_axis(keys, jnp.clip(src, 0, per_p + per_x - 1), axis=1)
    slot = jnp.where(in_p, key % per_p, key % per_x)
    tok = jnp.where(in_p, slot % rtok, rtok + slot % extra)
    w = jnp.concatenate([w_p.reshape(k, n_ranges, rtok).transpose(1, 0, 2).reshape(n_ranges, per_p),
                         jnp.broadcast_to(w_x.reshape(per_x), (n_ranges, per_x))], axis=1)
    wl = jnp.where(valid, jnp.take_along_axis(w, jnp.where(in_p, slot, per_p + slot), axis=1), 0.0)
    spare = slots + jnp.arange(tot_sub * bm, dtype=i32) % bm
    tail = _list_window(slots, bm)
    padded = lambda a, fill: jnp.concatenate([a, jnp.broadcast_to(fill, (n_ranges, tail)).astype(a.dtype)], axis=1)
    src8 = padded(jnp.where(valid, tok, 0) * 8, 0)
    dst8 = padded(jnp.where(valid, tok, spare) * 8, (slots + jnp.arange(tail, dtype=i32) % bm) * 8)
    g = n_ranges * N_EXPERTS
    return first_sub.reshape(g), n_sub.reshape(g), src8[:, None, :], dst8[:, None, :], padded(wl, 0.0)[:, None, :]


def _moe_sorted_kernel(first_ref, nsub_ref, src_ref, tok_ref, w_ref, h_hbm, hx_hbm, wg_ref, wu_ref, wd_ref,
                       acc_ref, hbuf, sem, wg_b, wu_b, wd_b, xg, ys, *, bm):
    r = pl.program_id(0)
    e = pl.program_id(1)
    n_chunks = D_MODEL // LANES
    n_blocks = nsub_ref[r * N_EXPERTS + e]
    range_rows = h_hbm.shape[1]

    @pl.when(e == 0)
    def _():
        main = pltpu.make_async_copy(h_hbm.at[r], hbuf.at[pl.ds(0, range_rows), :], sem.at[0])
        main.start()

        @pl.when(r == pl.num_programs(0) - 1)
        def _():
            extra = pltpu.make_async_copy(hx_hbm.at[0], hbuf.at[pl.ds(range_rows, hx_hbm.shape[1]), :], sem.at[1])
            extra.start()
            extra.wait()

        acc_ref[...] = jnp.zeros(acc_ref.shape, F32)
        ys[...] = jnp.zeros(ys.shape, F32)
        main.wait()

    def gather(q0):
        for rr in range(bm):
            t8 = pl.multiple_of(src_ref[0, 0, q0 + rr], n_chunks)
            xg[rr * n_chunks:(rr + 1) * n_chunks, :] = hbuf[pl.ds(t8, n_chunks), :]

    def scatter(q0):
        for b0 in range(0, bm, SCATTER_BATCH):
            rows = range(b0, b0 + SCATTER_BATCH)
            toks = [pl.multiple_of(tok_ref[0, 0, q0 + rr], n_chunks) for rr in rows]
            wts = [w_ref[0, 0, q0 + rr] for rr in rows]
            olds = [acc_ref[0, pl.ds(t8, n_chunks), :] for t8 in toks]
            news = [o + wv * ys[rr * n_chunks:(rr + 1) * n_chunks, :] for o, wv, rr in zip(olds, wts, rows)]
            for t8, nv in zip(toks, news):
                acc_ref[0, pl.ds(t8, n_chunks), :] = nv

    def block(b, carry):
        x = jnp.concatenate([xg[pl.ds(j, bm, stride=n_chunks), :] for j in range(n_chunks)], axis=1).astype(BF16)
        gather((b + 2) * bm)
        scatter(b * bm)
        hh = _silu(_dot(x, wg_b[...])) * _dot(x, wu_b[...])
        y = _dot(hh.astype(BF16), wd_b[...])
        for i in range(bm // 8):
            for j in range(n_chunks):
                ys[pl.ds(i * 8 * n_chunks + j, 8, stride=n_chunks), :] = y[8 * i:8 * i + 8, j * LANES:(j + 1) * LANES]
        return carry

    @pl.when(e == 0)
    def _():
        gather(bm)

    @pl.when(n_blocks > 0)
    def _():
        wg_b[...] = wg_ref[0].astype(BF16)
        wu_b[...] = wu_ref[0].astype(BF16)
        wd_b[...] = wd_ref[0].astype(BF16)
        block(0, 0)
        lax.fori_loop(1, n_blocks, block, 0)

    @pl.when(e == pl.num_programs(1) - 1)
    def _():
        scatter(n_blocks * bm)


def _moe_sorted(first_sub, n_sub, src8, dst8, wl, h2r, h2r_x, wg, wu, wd, *, bm):
    n_ranges, rows, _ = h2r.shape
    n_chunks = D_MODEL // LANES
    rtok = (rows + h2r_x.shape[1]) // n_chunks
    cap = _list_window(rtok, bm)
    one = pl.Element(1)
    window = lambda r, e, first, nsub: (r, 0, first[r * N_EXPERTS + e] * bm)
    list_spec = pl.BlockSpec((one, one, pl.Element(cap)), window, memory_space=pltpu.SMEM)
    grid_spec = pltpu.PrefetchScalarGridSpec(
        num_scalar_prefetch=2,
        grid=(n_ranges, N_EXPERTS),
        in_specs=[list_spec, list_spec, list_spec,
                  pl.BlockSpec(memory_space=pl.ANY), pl.BlockSpec(memory_space=pl.ANY),
                  pl.BlockSpec((1, D_MODEL, EXPERT_DIM), lambda r, e, *_: (e, 0, 0)),
                  pl.BlockSpec((1, D_MODEL, EXPERT_DIM), lambda r, e, *_: (e, 0, 0)),
                  pl.BlockSpec((1, EXPERT_DIM, D_MODEL), lambda r, e, *_: (e, 0, 0))],
        out_specs=pl.BlockSpec((1, (rtok + bm) * n_chunks, LANES), lambda r, e, *_: (r, 0, 0),
                               pipeline_mode=pl.Buffered(1)),
        scratch_shapes=[pltpu.VMEM((rtok * n_chunks, LANES), F32), pltpu.SemaphoreType.DMA((2,)),
                        pltpu.VMEM((D_MODEL, EXPERT_DIM), BF16), pltpu.VMEM((D_MODEL, EXPERT_DIM), BF16),
                        pltpu.VMEM((EXPERT_DIM, D_MODEL), BF16),
                        pltpu.VMEM((bm * n_chunks, LANES), F32), pltpu.VMEM((bm * n_chunks, LANES), F32)])
    return pl.pallas_call(
        functools.partial(_moe_sorted_kernel, bm=bm),
        grid_spec=grid_spec,
        out_shape=jax.ShapeDtypeStruct((n_ranges, (rtok + bm) * n_chunks, LANES), F32),
        compiler_params=pltpu.CompilerParams(dimension_semantics=("arbitrary", "arbitrary"),
                                             vmem_limit_bytes=VMEM_LIMIT),
        name="moe_grouped",
    )(first_sub, n_sub, src8, dst8, wl, h2r, h2r_x, wg, wu, wd)


def _final_grouped_kernel(x1_ref, fr_ref, h_ref, sg_ref, su_ref, sd_ref, g2_ref, gain_ref, y_ref):
    shared = _dot(_swiglu(h_ref[...], sg_ref[...], su_ref[...]).astype(BF16), sd_ref[...].astype(BF16))
    f = _from_token_rows(fr_ref, x1_ref.shape[0]) + shared
    y_ref[...] = x1_ref[...] + g2_ref[0] * _rmsnorm(f, gain_ref[...])


def _final_grouped(x1, fr, h2b, sg, su, sd, g2, gain, *, tm, fr_index, mod_index):
    t = x1.shape[0]
    n_chunks = D_MODEL // LANES
    row = pl.BlockSpec((tm, D_MODEL), lambda i: (i, 0))
    full = lambda shape: pl.BlockSpec(shape, lambda i: (0,) * len(shape))
    return pl.pallas_call(
        _final_grouped_kernel,
        grid=(t // tm,),
        in_specs=[row,
                  pl.BlockSpec((1, tm * n_chunks, LANES), lambda i: (*fr_index(i), 0)),
                  row, full(sg.shape), full(su.shape), full(sd.shape),
                  pl.BlockSpec((1, g2.shape[1], D_MODEL), lambda i: (mod_index(i), 0, 0)),
                  pl.BlockSpec((1, D_MODEL), lambda i: (0, 0))],
        out_specs=row,
        out_shape=jax.ShapeDtypeStruct((t, D_MODEL), F32),
        compiler_params=pltpu.CompilerParams(dimension_semantics=("parallel",), vmem_limit_bytes=VMEM_LIMIT),
        name="final_grouped",
    )(x1, fr, h2b, sg, su, sd, g2, gain)


def _rope_tables(pos):
    half = HEAD_DIM // 2
    inv_freq = ROPE_THETA ** (-jnp.arange(half, dtype=F32) / half)
    ang = pos.astype(F32)[:, None] * inv_freq[None, :]
    cos, sin = jnp.cos(ang), jnp.sin(ang)
    reps = LANES // HEAD_DIM
    return jnp.tile(jnp.concatenate([cos, cos], axis=1), (1, reps)), jnp.tile(jnp.concatenate([-sin, sin], axis=1), (1, reps))


def kernel(x_prompt, x_sample, c_prompt, c_sample, cache_k, cache_v, state_pool, page_table, ada_w, ada_b, pre_mix_g, post_mix_g, pre_ffn_g, post_ffn_g, w_in, w_out, lambda_q1, lambda_k1, lambda_q2, lambda_k2, subln_g, pool_w, pool_scale, router_w, router_bias, expert_w_gate, expert_w_up, expert_w_down, shared_w_gate, shared_w_up, shared_w_down):
    depth = ada_w.shape[0]
    assert depth == 1, "single-layer step"
    nb, seq, _ = x_prompt.shape
    db, dec_seq, _ = x_sample.shape
    assert dec_seq == 1
    page_size = cache_k.shape[2]
    n_pages = page_table.shape[1]
    past_len = n_pages * page_size
    lam_init = 0.8 - 0.6 * math.exp(-0.3 * 0)
    l = 0

    c_all = jnp.concatenate([c_prompt, c_sample], axis=0)
    pad = (-c_all.shape[0]) % 16
    mod = _ada(jnp.pad(c_all, ((0, pad), (0, 0))), ada_w[l], ada_b[l][None, :])
    mod_p = [m.reshape(nb, 1, D_MODEL) for m in jnp.split(mod[:nb], 6, axis=-1)]
    mod_s = [m.reshape(1, db, D_MODEL) for m in jnp.split(mod[nb:nb + db], 6, axis=-1)]

    w_in_b = w_in[l].astype(BF16)
    w_out_b = w_out[l].astype(BF16)
    lam4 = jnp.stack([lambda_q1[l], lambda_k1[l], lambda_q2[l], lambda_k2[l]])
    row = lambda v: v.reshape(1, -1)
    rwt = router_w[l].T
    rb_col = router_bias[l].reshape(N_EXPERTS, 1)
    experts = (expert_w_gate[l], expert_w_up[l], expert_w_down[l], shared_w_gate[l], shared_w_up[l], shared_w_down[l])

    t_p = nb * seq
    tm = TM_TOKEN
    seq_tiles = seq // tm
    cos_p, sin_p = _rope_tables(jnp.arange(seq))
    xp = x_prompt.reshape(t_p, D_MODEL)
    k_p, v_p, z_p, qb, kb, vt = _premix(xp, mod_p[0], mod_p[1], row(pre_mix_g[l]), w_in_b, cos_p, sin_p,
                                        tm=tm, rows_per_mod=seq_tiles, seq_tiles=seq_tiles, tk=TK_ATTN)
    o_p = _attn_prompt(qb.reshape(nb, seq, ATTN_W), kb.reshape(nb, seq, ATTN_W), vt, lam4,
                       subln_g[l].reshape(V_DIM, 1), tq=TQ_ATTN, tk=TK_ATTN, hps=HEADS_PER_STEP,
                       lam_init=lam_init)
    x1_p, h2_p, idx_p, w_p, h2r, cnt_p = _postmix(o_p.reshape(t_p, ATTN_W), z_p, z_p, xp, mod_p[2], mod_p[3],
                                                  mod_p[4], w_out_b, pool_w[l], row(pool_scale[l]),
                                                  row(post_mix_g[l]), row(pre_ffn_g[l]), rwt, rb_col,
                                                  tm=tm, rows_per_mod=seq_tiles, seq_tiles=seq_tiles, past_len=None)

    cos_s, sin_s = _rope_tables(jnp.full((db,), past_len))
    xs = x_sample.reshape(db, D_MODEL)
    k_s, v_s, z_s, qb_s, kb_s = _premix(xs, mod_s[0], mod_s[1], row(pre_mix_g[l]), w_in_b, cos_s, sin_s,
                                        tm=db, rows_per_mod=1, seq_tiles=1, tk=0)
    page_rows = page_size * N_HEADS
    k2 = cache_k[l].reshape(-1, V_DIM)
    v2 = cache_v[l].reshape(-1, V_DIM)
    o_s = _attn_decode(page_table, qb_s.reshape(db, N_HEADS, V_DIM), kb_s.reshape(db, N_HEADS, V_DIM),
                       v_s.reshape(db, N_HEADS, V_DIM), lam4, row(subln_g[l]), k2, v2,
                       page_rows=page_rows, lam_init=lam_init)
    hist_s = jnp.transpose(state_pool[l], (1, 0, 2))
    x1_s, h2_s, idx_s, w_s, h2r_s, cnt_s = _postmix(o_s.reshape(db, ATTN_W), z_s, hist_s, xs, mod_s[2], mod_s[3],
                                                    mod_s[4], w_out_b, pool_w[l], row(pool_scale[l]),
                                                    row(post_mix_g[l]), row(pre_ffn_g[l]), rwt, rb_col,
                                                    tm=db, rows_per_mod=1, seq_tiles=1, past_len=past_len)

    per_range = lambda c: jnp.sum(c.reshape(nb, -1, N_EXPERTS), axis=1).astype(jnp.int32)
    first_sub, n_sub, src8, dst8, wl = _dispatch_tables(idx_p, w_p, per_range(cnt_p), idx_s, w_s,
                                                        cnt_s.reshape(N_EXPERTS).astype(jnp.int32), nb, seq, MOE_BM)
    fr = _moe_sorted(first_sub, n_sub, src8, dst8, wl, h2r, h2r_s, *experts[:3], bm=MOE_BM)
    final_tiles = seq // TM_FINAL
    y_p = _final_grouped(x1_p, fr, h2_p, *experts[3:], mod_p[5], row(post_ffn_g[l]), tm=TM_FINAL,
                         fr_index=lambda i: (i // final_tiles, i % final_tiles), mod_index=lambda i: i // final_tiles)
    y_s = _final_grouped(x1_s, fr, h2_s, *experts[3:], mod_s[5], row(post_ffn_g[l]), tm=db,
                         fr_index=lambda i: (nb - 1, seq // db), mod_index=lambda i: 0)

    pool_p = z_p.reshape(nb, seq, POOL_W)[:, seq - POOL_BUF:]
    pool_s = jnp.concatenate([state_pool[l][:, 1:], z_s[:, None, :]], axis=1)
    return (y_p.reshape(nb, seq, D_MODEL), y_s.reshape(db, 1, D_MODEL),
            k_p.reshape(1, nb, seq, N_HEADS, V_DIM), v_p.reshape(1, nb, seq, N_HEADS, V_DIM), pool_p[None],
            k_s.reshape(1, db, 1, N_HEADS, V_DIM), v_s.reshape(1, db, 1, N_HEADS, V_DIM), pool_s[None])
```

```python
import functools
import math

import jax
import jax.numpy as jnp
from jax import lax
from jax.experimental import pallas as pl
from jax.experimental.pallas import tpu as pltpu

F32 = jnp.float32
BF16 = jnp.bfloat16

D_MODEL = 1024
ATTN_W = 512
POOL_W = 512
HEAD_DIM = 64
N_HEADS = 4
V_DIM = 2 * HEAD_DIM
IN_COLS = 3 * ATTN_W + POOL_W
POOL_WINDOWS = (2, 4, 8, 16)
POOL_GW = 128
POOL_BUF = 15
POOL_HALO = 16
ROPE_THETA = 10000.0
N_EXPERTS = 64
TOP_K = 8
N_EXPERT_GROUPS = 8
GROUP_SIZE = N_EXPERTS // N_EXPERT_GROUPS
TOPK_GROUPS = 4
EXPERT_DIM = 256
ROUTED_SCALE = 2.5
EPS = 1e-6
LANES = 128
NEG_INF = float("-inf")
Q_SCALE = HEAD_DIM ** -0.5 * math.log2(math.e)
VMEM_LIMIT = 56 * 1024 * 1024

TM_TOKEN = 512
TQ_ATTN = 512
TK_ATTN = 256
HEADS_PER_STEP = 4
MOE_BM = 128
SCATTER_BATCH = 4
TM_FINAL = 512
DEC_CHUNK_PAGES = 32


def _silu(x):
    return x * jax.nn.sigmoid(x)


def _rmsnorm(x, g):
    ms = jnp.mean(x * x, axis=-1, keepdims=True)
    return x * lax.rsqrt(ms + EPS) * g


def _dot(a, b):
    return jnp.dot(a, b, preferred_element_type=F32)


def _dot_nt(a, b):
    return lax.dot_general(a, b, (((1,), (1,)), ((), ())), preferred_element_type=F32)


def _ada_kernel(c_ref, w_ref, b_ref, o_ref):
    s = _silu(c_ref[...])
    o_ref[...] = _dot(s.astype(BF16), w_ref[...].astype(BF16)) + b_ref[...]


def _ada(c_pad, w, b):
    rows, n = c_pad.shape[0], w.shape[1]
    tn = 1536
    return pl.pallas_call(
        _ada_kernel,
        grid=(n // tn,),
        in_specs=[pl.BlockSpec((rows, D_MODEL), lambda j: (0, 0)),
                  pl.BlockSpec((D_MODEL, tn), lambda j: (0, j)),
                  pl.BlockSpec((1, tn), lambda j: (0, j))],
        out_specs=pl.BlockSpec((rows, tn), lambda j: (0, j)),
        out_shape=jax.ShapeDtypeStruct((rows, n), F32),
        compiler_params=pltpu.CompilerParams(dimension_semantics=("arbitrary",), vmem_limit_bytes=VMEM_LIMIT),
        name="ada_mod",
    )(c_pad, w, b)


def _rope(x, cos, sin_signed):
    outs = []
    for c in range(x.shape[1] // LANES):
        blk = x[:, c * LANES:(c + 1) * LANES]
        lane = lax.broadcasted_iota(jnp.int32, blk.shape, 1)
        partner = jnp.where((lane % HEAD_DIM) < HEAD_DIM // 2,
                            pltpu.roll(blk, LANES - HEAD_DIM // 2, 1),
                            pltpu.roll(blk, HEAD_DIM // 2, 1))
        outs.append(blk * cos + partner * sin_signed)
    return jnp.concatenate(outs, axis=1)


def _premix_kernel(x_ref, sh_ref, sc_ref, g_ref, w_ref, cos_ref, sin_ref,
                   k_ref, v_ref, z_ref, qb_ref, kb_ref, *rest, n_kv):
    h = _rmsnorm(x_ref[...], g_ref[...]) * (1.0 + sc_ref[0]) + sh_ref[0]
    proj = _dot(h.astype(BF16), w_ref[...])
    cos, sin = cos_ref[...], sin_ref[...]
    q = _rope(proj[:, :ATTN_W], cos, sin)
    k = _rope(proj[:, ATTN_W:2 * ATTN_W], cos, sin)
    v = proj[:, 2 * ATTN_W:3 * ATTN_W]
    for out_ref, val in ((k_ref, k), (v_ref, v)):
        for i in range(val.shape[0] // 8):
            for hh in range(N_HEADS):
                out_ref[pl.ds(i * 8 * N_HEADS + hh, 8, stride=N_HEADS), :] = val[8 * i:8 * i + 8, hh * V_DIM:(hh + 1) * V_DIM]
    z_ref[...] = proj[:, 3 * ATTN_W:]
    qb_ref[...] = (q * Q_SCALE).astype(BF16)
    kb_ref[...] = k.astype(BF16)
    if n_kv:
        vt_ref = rest[0]
        vt = v.T.astype(BF16)
        tk = vt.shape[1] // n_kv
        for hh in range(N_HEADS):
            for c in range(n_kv):
                vt_ref[0, hh, c] = vt[hh * V_DIM:(hh + 1) * V_DIM, c * tk:(c + 1) * tk]


def _premix(x, shift, scale, gain, w_in_b, cos, sin, *, tm, rows_per_mod, seq_tiles, tk):
    t = x.shape[0]
    n_tiles = t // tm
    mod_rows = shift.shape[1]
    mod_idx = (lambda i: (i // rows_per_mod, 0, 0))
    n_kv = tm // tk if tk else 0
    out_shape = [jax.ShapeDtypeStruct((t * N_HEADS, V_DIM), F32), jax.ShapeDtypeStruct((t * N_HEADS, V_DIM), F32),
                 jax.ShapeDtypeStruct((t, POOL_W), F32), jax.ShapeDtypeStruct((t, ATTN_W), BF16),
                 jax.ShapeDtypeStruct((t, ATTN_W), BF16)]
    row_spec = pl.BlockSpec((tm, ATTN_W), lambda i: (i, 0))
    head_rows = pl.BlockSpec((tm * N_HEADS, V_DIM), lambda i: (i, 0))
    out_specs = [head_rows, head_rows, row_spec, row_spec, row_spec]
    if n_kv:
        nb = n_tiles // seq_tiles
        out_shape.append(jax.ShapeDtypeStruct((nb, N_HEADS, seq_tiles * n_kv, V_DIM, tk), BF16))
        out_specs.append(pl.BlockSpec((1, N_HEADS, n_kv, V_DIM, tk),
                                      lambda i: (i // seq_tiles, 0, i % seq_tiles, 0, 0)))
    return pl.pallas_call(
        functools.partial(_premix_kernel, n_kv=n_kv),
        grid=(n_tiles,),
        in_specs=[pl.BlockSpec((tm, D_MODEL), lambda i: (i, 0)),
                  pl.BlockSpec((1, mod_rows, D_MODEL), mod_idx),
                  pl.BlockSpec((1, mod_rows, D_MODEL), mod_idx),
                  pl.BlockSpec((1, D_MODEL), lambda i: (0, 0)),
                  pl.BlockSpec((D_MODEL, IN_COLS), lambda i: (0, 0)),
                  pl.BlockSpec((tm, LANES), lambda i: (i % seq_tiles, 0)),
                  pl.BlockSpec((tm, LANES), lambda i: (i % seq_tiles, 0))],
        out_specs=out_specs,
        out_shape=out_shape,
        compiler_params=pltpu.CompilerParams(dimension_semantics=("parallel",), vmem_limit_bytes=VMEM_LIMIT),
        name="premix",
    )(x, shift, scale, gain, w_in_b, cos, sin)


def _diff_lambda(lam_ref, lam_init):
    lam = lam_ref[...]
    s1 = jnp.sum(lam[0:1] * lam[1:2], axis=1, keepdims=True)
    s2 = jnp.sum(lam[2:3] * lam[3:4], axis=1, keepdims=True)
    return jnp.exp(s1) - jnp.exp(s2) + lam_init


def _split_maps(q):
    lane = lax.broadcasted_iota(jnp.int32, q.shape, 1)
    zero = jnp.zeros_like(q)
    return jnp.concatenate([jnp.where(lane < HEAD_DIM, q, zero), jnp.where(lane >= HEAD_DIM, q, zero)], axis=0)


def _attn_kernel(q_ref, k_ref, vt_ref, lam_ref, g_ref, o_ref, s_even, s_odd, acc_ref, *, tq, tk, hps, lam_init):
    i = pl.program_id(2)
    r = tq // tk
    heads = range(hps)
    cols = [slice(hh * V_DIM, (hh + 1) * V_DIM) for hh in heads]
    qpads = [_split_maps(q_ref[0, :, cols[hh]]) for hh in heads]

    def logits(j, buf):
        rows = pl.ds(pl.multiple_of(j * tk, tk), tk)
        for hh in heads:
            buf[hh] = _dot_nt(k_ref[0, rows, cols[hh]], qpads[hh])

    def consume(j, buf, carry, masked):
        def read(hh):
            s = buf[hh]
            if masked:
                kpos = j * tk + lax.broadcasted_iota(jnp.int32, s.shape, 0)
                qpos = i * tq + lax.broadcasted_iota(jnp.int32, s.shape, 1) % tq
                s = jnp.where(kpos <= qpos, s, NEG_INF)
            return s

        ps, alphas, out = [], [], []
        for hh in heads:
            m, l = carry[hh]
            m_new = jnp.maximum(m, jnp.max(read(hh), axis=0, keepdims=True))
            alpha = jnp.exp2(m - m_new)
            p = jnp.exp2(read(hh) - m_new)
            out.append((m_new, alpha * l + jnp.sum(p, axis=0, keepdims=True)))
            alphas.append(alpha)
            ps.append(p.astype(BF16))
        for hh in heads:
            acc_ref[hh] = alphas[hh] * acc_ref[hh] + _dot(vt_ref[0, hh, j], ps[hh])
        return tuple(out)

    def pair(p, carry):
        j = 2 * p
        logits(j + 1, s_odd)
        carry = consume(j, s_even, carry, False)
        logits(j + 2, s_even)
        return consume(j + 1, s_odd, carry, False)

    assert r == 2, "the written-out schedule handles two key blocks per query block"
    init = tuple((jnp.full((1, 2 * tq), -1e30, F32), jnp.zeros((1, 2 * tq), F32)) for _ in heads)
    acc_ref[...] = jnp.zeros(acc_ref.shape, F32)
    logits(0, s_even)
    carry = lax.fori_loop(0, i, pair, init)
    logits(2 * i + 1, s_odd)
    carry = consume(2 * i, s_even, carry, True)
    carry = consume(2 * i + 1, s_odd, carry, True)
    lam = _diff_lambda(lam_ref, lam_init)
    for hh in heads:
        o = acc_ref[hh] / carry[hh][1]
        od = o[:, :tq] - lam * o[:, tq:]
        ms = jnp.mean(od * od, axis=0, keepdims=True)
        on = od * lax.rsqrt(ms + EPS) * g_ref[...] * (1.0 - lam_init)
        o_ref[0, :, cols[hh]] = on.T.astype(BF16)


def _attn_prompt(qb, kb, vt, lam4, subln_col, *, tq, tk, hps, lam_init):
    b, s, _ = qb.shape
    nq, nkv = s // tq, s // tk
    w = hps * V_DIM
    return pl.pallas_call(
        functools.partial(_attn_kernel, tq=tq, tk=tk, hps=hps, lam_init=lam_init),
        grid=(b, N_HEADS // hps, nq),
        in_specs=[pl.BlockSpec((1, tq, w), lambda bb, h, i: (bb, i, h)),
                  pl.BlockSpec((1, s, w), lambda bb, h, i: (bb, 0, h)),
                  pl.BlockSpec((1, hps, nkv, V_DIM, tk), lambda bb, h, i: (bb, h, 0, 0, 0)),
                  pl.BlockSpec((4, HEAD_DIM), lambda bb, h, i: (0, 0)),
                  pl.BlockSpec((V_DIM, 1), lambda bb, h, i: (0, 0))],
        out_specs=pl.BlockSpec((1, tq, w), lambda bb, h, i: (bb, i, h)),
        out_shape=jax.ShapeDtypeStruct((b, s, ATTN_W), BF16),
        scratch_shapes=[pltpu.VMEM((hps, tk, 2 * tq), F32), pltpu.VMEM((hps, tk, 2 * tq), F32),
                        pltpu.VMEM((hps, V_DIM, 2 * tq), F32)],
        compiler_params=pltpu.CompilerParams(dimension_semantics=("parallel", "parallel", "arbitrary"),
                                             vmem_limit_bytes=VMEM_LIMIT),
        name="attn_prompt",
    )(qb, kb, vt, lam4, subln_col)


def _decode_kernel(pt_ref, q_ref, kn_ref, vn_ref, lam_ref, g_ref, k_hbm, v_hbm, o_ref,
                   kbuf, vbuf, sem, m_sc, l_sc, acc_sc, *, n_chunks, chunk_pages, page_rows, lam_init):
    b = pl.program_id(0)
    c = pl.program_id(1)
    nb = pl.num_programs(0)
    step = b * n_chunks + c
    slot = step % 2

    def chunk_copies(bb, cc, sl):
        copies = []
        for p in range(chunk_pages):
            phys = pt_ref[bb * (n_chunks * chunk_pages) + cc * chunk_pages + p]
            src = pl.ds(pl.multiple_of(phys * page_rows, page_rows), page_rows)
            dst = pl.ds(p * page_rows, page_rows)
            copies.append(pltpu.make_async_copy(k_hbm.at[src, :], kbuf.at[sl, dst, :], sem.at[0, sl]))
            copies.append(pltpu.make_async_copy(v_hbm.at[src, :], vbuf.at[sl, dst, :], sem.at[1, sl]))
        return copies

    @pl.when(step == 0)
    def _():
        for cp in chunk_copies(b, c, slot):
            cp.start()

    @pl.when(step + 1 < nb * n_chunks)
    def _():
        nxt = step + 1
        for cp in chunk_copies(nxt // n_chunks, nxt % n_chunks, 1 - slot):
            cp.start()

    @pl.when(c == 0)
    def _():
        m_sc[...] = jnp.full(m_sc.shape, -1e30, F32)
        l_sc[...] = jnp.zeros(l_sc.shape, F32)
        acc_sc[...] = jnp.zeros(acc_sc.shape, F32)

    qblk = _split_maps(q_ref[0])
    for cp in chunk_copies(b, c, slot):
        cp.wait()
    kc = kbuf[slot].astype(BF16)
    vc = vbuf[slot].astype(BF16)
    s = _dot_nt(qblk, kc)
    row_head = lax.broadcasted_iota(jnp.int32, s.shape, 0) % N_HEADS
    col_head = lax.broadcasted_iota(jnp.int32, s.shape, 1) % N_HEADS
    s = jnp.where(row_head == col_head, s, NEG_INF)
    m_old = m_sc[...]
    m_new = jnp.maximum(m_old, jnp.max(s, axis=1, keepdims=True))
    alpha = jnp.exp2(m_old - m_new)
    p = jnp.exp2(s - m_new)
    l_new = alpha * l_sc[...] + jnp.sum(p, axis=1, keepdims=True)
    acc_new = alpha * acc_sc[...] + _dot(p.astype(BF16), vc)
    m_sc[...] = m_new
    l_sc[...] = l_new
    acc_sc[...] = acc_new

    @pl.when(c == n_chunks - 1)
    def _():
        kn = jnp.concatenate([kn_ref[0], kn_ref[0]], axis=0).astype(F32)
        vn = jnp.concatenate([vn_ref[0], vn_ref[0]], axis=0).astype(F32)
        s_n = jnp.sum(qblk.astype(F32) * kn, axis=1, keepdims=True)
        m_f = jnp.maximum(m_new, s_n)
        a_f = jnp.exp2(m_new - m_f)
        p_n = jnp.exp2(s_n - m_f)
        l_f = a_f * l_new + p_n
        o = (a_f * acc_new + p_n * vn) / l_f
        od = o[:N_HEADS] - _diff_lambda(lam_ref, lam_init) * o[N_HEADS:]
        o_ref[0] = (_rmsnorm(od, g_ref[...]) * (1.0 - lam_init)).astype(BF16)


def _attn_decode(page_table, q3, kn3, vn3, lam4, subln_row, k2, v2, *, page_rows, lam_init):
    nb, n_pages = page_table.shape
    chunk_pages = DEC_CHUNK_PAGES
    n_chunks = n_pages // chunk_pages
    rows = chunk_pages * page_rows
    head_spec = pl.BlockSpec((1, N_HEADS, V_DIM), lambda b, c, pt: (b, 0, 0))
    grid_spec = pltpu.PrefetchScalarGridSpec(
        num_scalar_prefetch=1,
        grid=(nb, n_chunks),
        in_specs=[head_spec, head_spec, head_spec,
                  pl.BlockSpec((4, HEAD_DIM), lambda b, c, pt: (0, 0)),
                  pl.BlockSpec((1, V_DIM), lambda b, c, pt: (0, 0)),
                  pl.BlockSpec(memory_space=pl.ANY),
                  pl.BlockSpec(memory_space=pl.ANY)],
        out_specs=head_spec,
        scratch_shapes=[pltpu.VMEM((2, rows, V_DIM), F32), pltpu.VMEM((2, rows, V_DIM), F32),
                        pltpu.SemaphoreType.DMA((2, 2)),
                        pltpu.VMEM((2 * N_HEADS, 1), F32), pltpu.VMEM((2 * N_HEADS, 1), F32),
                        pltpu.VMEM((2 * N_HEADS, V_DIM), F32)])
    return pl.pallas_call(
        functools.partial(_decode_kernel, n_chunks=n_chunks, chunk_pages=chunk_pages,
                          page_rows=page_rows, lam_init=lam_init),
        grid_spec=grid_spec,
        out_shape=jax.ShapeDtypeStruct((nb, N_HEADS, V_DIM), BF16),
        compiler_params=pltpu.CompilerParams(dimension_semantics=("arbitrary", "arbitrary"),
                                             vmem_limit_bytes=VMEM_LIMIT),
        name="attn_decode",
    )(page_table.reshape(-1), q3, kn3, vn3, lam4, subln_row, k2, v2)


def _pool_project(d_groups, pw_ref, ps_ref):
    ys = [_dot(d.astype(BF16), pw_ref[g].astype(BF16)) for g, d in enumerate(d_groups)]
    return jnp.concatenate(ys, axis=1) * ps_ref[...]


def _argmax_first(x, iota, n):
    mx = jnp.max(x, axis=0, keepdims=True)
    ix = jnp.min(jnp.where(x == mx, iota, float(n)), axis=0, keepdims=True)
    return mx, ix


def _row_iota(shape):
    return lax.broadcasted_iota(jnp.int32, shape, 0).astype(F32)


def _route(h2, rwt_ref, rb_ref):
    rw = rwt_ref[...]
    rw_hi = rw.astype(BF16)
    rw_lo = (rw - rw_hi.astype(F32)).astype(BF16)
    h_hi = h2.astype(BF16)
    h_lo = (h2 - h_hi.astype(F32)).astype(BF16)
    logits = _dot_nt(rw_hi, h_hi) + (_dot_nt(rw_hi, h_lo) + _dot_nt(rw_lo, h_hi))
    scores = jax.nn.sigmoid(logits)
    sel = scores + rb_ref[...]
    tm = sel.shape[1]
    iota_g = _row_iota((GROUP_SIZE, tm))
    group_rows = []
    for g in range(N_EXPERT_GROUPS):
        xg = sel[g * GROUP_SIZE:(g + 1) * GROUP_SIZE]
        m1, i1 = _argmax_first(xg, iota_g, GROUP_SIZE)
        m2 = jnp.max(jnp.where(iota_g == i1, NEG_INF, xg), axis=0, keepdims=True)
        group_rows.append(m1 + m2)
    gscore = jnp.concatenate(group_rows, axis=0)
    iota_gr = _row_iota(gscore.shape)
    gmask = jnp.zeros(gscore.shape, F32)
    for _ in range(TOPK_GROUPS):
        _, ig = _argmax_first(gscore, iota_gr, N_EXPERT_GROUPS)
        hit = iota_gr == ig
        gmask = jnp.where(hit, 1.0, gmask)
        gscore = jnp.where(hit, NEG_INF, gscore)
    masked = jnp.concatenate(
        [jnp.where(gmask[g:g + 1] > 0.5, sel[g * GROUP_SIZE:(g + 1) * GROUP_SIZE], NEG_INF)
         for g in range(N_EXPERT_GROUPS)], axis=0)
    iota_e = _row_iota(masked.shape)
    w = jnp.zeros(masked.shape, F32)
    chosen = jnp.zeros(masked.shape, F32)
    idx_rows, score_rows = [], []
    for _ in range(TOP_K):
        _, ie = _argmax_first(masked, iota_e, N_EXPERTS)
        hit = iota_e == ie
        w = jnp.where(hit, scores, w)
        chosen = jnp.where(hit, 1.0, chosen)
        masked = jnp.where(hit, NEG_INF, masked)
        idx_rows.append(ie)
        score_rows.append(jnp.sum(jnp.where(hit, scores, 0.0), axis=0, keepdims=True))
    wsum = jnp.sum(w, axis=0, keepdims=True)
    idx_t = jnp.concatenate(idx_rows, axis=0).astype(jnp.int32)
    w_t = jnp.concatenate(score_rows, axis=0) / wsum * ROUTED_SCALE
    return idx_t, w_t, jnp.sum(chosen, axis=1, keepdims=True)


def _to_token_rows(x, out_ref):
    n_chunks = x.shape[1] // LANES
    for i in range(x.shape[0] // 8):
        for j in range(n_chunks):
            out_ref[0, pl.ds(i * 8 * n_chunks + j, 8, stride=n_chunks), :] = x[8 * i:8 * i + 8, j * LANES:(j + 1) * LANES]


def _from_token_rows(in_ref, tm):
    n_chunks = D_MODEL // LANES
    return jnp.concatenate([in_ref[0, pl.ds(j, tm, stride=n_chunks), :] for j in range(n_chunks)], axis=1)


def _mix_tail(o_b, pz, x, g1, sh2, sc2, wout_ref, gpost_ref, gpre_ref, rwt_ref, rb_ref,
              x1_ref, h2_ref, idx_ref, w_ref, h2r_ref, cnt_ref):
    m = _dot(o_b, wout_ref[:ATTN_W, :]) + _dot(pz.astype(BF16), wout_ref[ATTN_W:, :])
    x1 = x + g1 * _rmsnorm(m, gpost_ref[...])
    h2 = _rmsnorm(x1, gpre_ref[...]) * (1.0 + sc2) + sh2
    x1_ref[...] = x1
    h2_ref[...] = h2.astype(BF16)
    idx_ref[...], w_ref[...], cnt_ref[0] = _route(h2, rwt_ref, rb_ref)
    _to_token_rows(h2, h2r_ref)


def _postmix_prompt_kernel(o_ref, z_ref, zprev_ref, x_ref, g1_ref, sh2_ref, sc2_ref,
                           wout_ref, pw_ref, ps_ref, gpost_ref, gpre_ref, rwt_ref, rb_ref,
                           x1_ref, h2_ref, idx_ref, w_ref, h2r_ref, cnt_ref, zbuf, *, seq_tiles):
    tm = z_ref.shape[0]
    it = pl.program_id(0) % seq_tiles
    zt = z_ref[...]
    zbuf[0:POOL_HALO, :] = jnp.where(it == 0, 0.0, zprev_ref[...])
    zbuf[POOL_HALO:, :] = zt
    pos = it * tm + lax.broadcasted_iota(jnp.int32, (tm, 1), 0)
    d_groups = []
    for g, w in enumerate(POOL_WINDOWS):
        cols = slice(g * POOL_GW, (g + 1) * POOL_GW)
        win = zt[:, cols]
        for s in range(1, w):
            win = win + zbuf[POOL_HALO - s:POOL_HALO - s + tm, cols]
        cnt = jnp.minimum(pos + 1, w).astype(F32)
        d_groups.append(win / cnt - zt[:, cols])
    pz = _pool_project(d_groups, pw_ref, ps_ref)
    _mix_tail(o_ref[...], pz, x_ref[...], g1_ref[0], sh2_ref[0], sc2_ref[0], wout_ref, gpost_ref, gpre_ref,
              rwt_ref, rb_ref, x1_ref, h2_ref, idx_ref, w_ref, h2r_ref, cnt_ref)


def _postmix_sample_kernel(o_ref, z_ref, st_ref, x_ref, g1_ref, sh2_ref, sc2_ref,
                           wout_ref, pw_ref, ps_ref, gpost_ref, gpre_ref, rwt_ref, rb_ref,
                           x1_ref, h2_ref, idx_ref, w_ref, h2r_ref, cnt_ref, *, past_len):
    zt = z_ref[...]
    d_groups = []
    for g, w in enumerate(POOL_WINDOWS):
        cols = slice(g * POOL_GW, (g + 1) * POOL_GW)
        win = zt[:, cols]
        for s in range(1, w):
            win = win + st_ref[POOL_BUF - s][:, cols]
        d_groups.append(win / float(min(past_len + 1, w)) - zt[:, cols])
    pz = _pool_project(d_groups, pw_ref, ps_ref)
    _mix_tail(o_ref[...], pz, x_ref[...], g1_ref[0], sh2_ref[0], sc2_ref[0], wout_ref, gpost_ref, gpre_ref,
              rwt_ref, rb_ref, x1_ref, h2_ref, idx_ref, w_ref, h2r_ref, cnt_ref)


def _postmix(o, z, hist, x, g1, sh2, sc2, wout_b, pool_w, pool_scale, g_post, g_pre, rwt, rb_col,
             *, tm, rows_per_mod, seq_tiles, past_len):
    t = x.shape[0]
    n_tiles = t // tm
    mod_rows = g1.shape[1]
    rows8 = D_MODEL // LANES
    mod_spec = pl.BlockSpec((1, mod_rows, D_MODEL), lambda i: (i // rows_per_mod, 0, 0))
    full = lambda shape: pl.BlockSpec(shape, lambda i: (0,) * len(shape))
    if past_len is None:
        body = functools.partial(_postmix_prompt_kernel, seq_tiles=seq_tiles)
        halo_blocks = tm // POOL_HALO
        hist_spec = pl.BlockSpec((POOL_HALO, POOL_W), lambda i: (jnp.maximum(i * halo_blocks - 1, 0), 0))
        scratch = [pltpu.VMEM((POOL_HALO + tm, POOL_W), F32)]
    else:
        body = functools.partial(_postmix_sample_kernel, past_len=past_len)
        hist_spec = full(hist.shape)
        scratch = []
    return pl.pallas_call(
        body,
        grid=(n_tiles,),
        in_specs=[pl.BlockSpec((tm, ATTN_W), lambda i: (i, 0)),
                  pl.BlockSpec((tm, POOL_W), lambda i: (i, 0)),
                  hist_spec,
                  pl.BlockSpec((tm, D_MODEL), lambda i: (i, 0)),
                  mod_spec, mod_spec, mod_spec,
                  full((D_MODEL, D_MODEL)), full(pool_w.shape), full((1, POOL_W)),
                  full((1, D_MODEL)), full((1, D_MODEL)), full((N_EXPERTS, D_MODEL)), full((N_EXPERTS, 1))],
        out_specs=[pl.BlockSpec((tm, D_MODEL), lambda i: (i, 0)),
                   pl.BlockSpec((tm, D_MODEL), lambda i: (i, 0)),
                   pl.BlockSpec((TOP_K, tm), lambda i: (0, i)), pl.BlockSpec((TOP_K, tm), lambda i: (0, i)),
                   pl.BlockSpec((1, tm * rows8, LANES), lambda i: (i // seq_tiles, i % seq_tiles, 0)),
                   pl.BlockSpec((1, N_EXPERTS, 1), lambda i: (i, 0, 0))],
        out_shape=[jax.ShapeDtypeStruct((t, D_MODEL), F32), jax.ShapeDtypeStruct((t, D_MODEL), BF16),
                   jax.ShapeDtypeStruct((TOP_K, t), jnp.int32), jax.ShapeDtypeStruct((TOP_K, t), F32),
                   jax.ShapeDtypeStruct((n_tiles // seq_tiles, seq_tiles * tm * rows8, LANES), F32),
                   jax.ShapeDtypeStruct((n_tiles, N_EXPERTS, 1), F32)],
        scratch_shapes=scratch,
        compiler_params=pltpu.CompilerParams(dimension_semantics=("parallel",), vmem_limit_bytes=VMEM_LIMIT),
        name="postmix",
    )(o, z, hist, x, g1, sh2, sc2, wout_b, pool_w, pool_scale, g_post, g_pre, rwt, rb_col)


def _swiglu(xb, wg, wu):
    return _silu(_dot(xb, wg.astype(BF16))) * _dot(xb, wu.astype(BF16))


def _list_window(range_slots, bm):
    return (-(-range_slots // bm) + 3) * bm


def _dispatch_tables(idx_p, w_p, cnt_p, idx_x, w_x, cnt_x, n_ranges, rtok, bm):
    k, extra = idx_x.shape
    slots = rtok + extra
    per_p, per_x = k * rtok, k * extra
    i32 = jnp.int32
    e_p = idx_p.reshape(k, n_ranges, rtok).transpose(1, 0, 2).reshape(n_ranges, per_p)
    keys_p = jnp.sort(e_p * per_p + jnp.arange(per_p, dtype=i32), axis=1)
    keys_x = jnp.sort(idx_x.reshape(per_x) * per_x + jnp.arange(per_x, dtype=i32))
    cnt_x = jnp.where((jnp.arange(n_ranges) == n_ranges - 1)[:, None], cnt_x[None, :], 0)
    start_p = jnp.cumsum(cnt_p, axis=1) - cnt_p
    start_x = jnp.cumsum(cnt_x, axis=1) - cnt_x
    n_sub = (cnt_p + cnt_x + bm - 1) // bm
    first_sub = jnp.cumsum(n_sub, axis=1) - n_sub
    tot_sub = 1 + -(-(per_p + per_x) // bm) + N_EXPERTS
    sub = jnp.arange(tot_sub, dtype=i32) - 1
    begun = sub[None, :, None] >= first_sub[:, None, :]
    at_begun = lambda a: jnp.max(jnp.where(begun, a[:, None, :], 0), axis=2)
    at_next = lambda a, end: jnp.min(jnp.where(begun, end, a[:, None, :]), axis=2)
    lo_p, hi_p = at_begun(start_p), at_next(start_p, per_p)
    lo_x, hi_x = at_begun(start_x), at_next(start_x, jnp.sum(cnt_x, axis=1)[:, None, None])
    pos = ((sub - at_begun(first_sub)) * bm)[:, :, None] + jnp.arange(bm, dtype=i32)
    n_p = (hi_p - lo_p)[:, :, None]
    flat = lambda a: a.reshape(n_ranges, tot_sub * bm)
    in_p = flat(pos < n_p)
    valid = flat(jnp.logical_and(pos < n_p + (hi_x - lo_x)[:, :, None], (sub >= 0)[None, :, None]))
    src = flat(jnp.where(pos < n_p, lo_p[:, :, None] + pos, per_p + lo_x[:, :, None] + pos - n_p))
    keys = jnp.concatenate([keys_p, jnp.broadcast_to(keys_x, (n_ranges, per_x))], axis=1)
    key = jnp.take_along_axis(keys, jnp.clip(src, 0, per_p + per_x - 1), axis=1)
    slot = jnp.where(in_p, key % per_p, key % per_x)
    tok = jnp.where(in_p, slot % rtok, rtok + slot % extra)
    w = jnp.concatenate([w_p.reshape(k, n_ranges, rtok).transpose(1, 0, 2).reshape(n_ranges, per_p),
                         jnp.broadcast_to(w_x.reshape(per_x), (n_ranges, per_x))], axis=1)
    wl = jnp.where(valid, jnp.take_along_axis(w, jnp.where(in_p, slot, per_p + slot), axis=1), 0.0)
    spare = slots + jnp.arange(tot_sub * bm, dtype=i32) % bm
    tail = _list_window(slots, bm)
    padded = lambda a, fill: jnp.concatenate([a, jnp.broadcast_to(fill, (n_ranges, tail)).astype(a.dtype)], axis=1)
    src8 = padded(jnp.where(valid, tok, 0) * 8, 0)
    dst8 = padded(jnp.where(valid, tok, spare) * 8, (slots + jnp.arange(tail, dtype=i32) % bm) * 8)
    g = n_ranges * N_EXPERTS
    return first_sub.reshape(g), n_sub.reshape(g), src8[:, None, :], dst8[:, None, :], padded(wl, 0.0)[:, None, :]


def _moe_sorted_kernel(first_ref, nsub_ref, src_ref, tok_ref, w_ref, h_hbm, hx_hbm, wg_ref, wu_ref, wd_ref,
                       acc_ref, hbuf, sem, wg_b, wu_b, wd_b, xg, ys, *, bm):
    r = pl.program_id(0)
    e = pl.program_id(1)
    n_chunks = D_MODEL // LANES
    n_blocks = nsub_ref[r * N_EXPERTS + e]
    range_rows = h_hbm.shape[1]

    @pl.when(e == 0)
    def _():
        main = pltpu.make_async_copy(h_hbm.at[r], hbuf.at[pl.ds(0, range_rows), :], sem.at[0])
        main.start()

        @pl.when(r == pl.num_programs(0) - 1)
        def _():
            extra = pltpu.make_async_copy(hx_hbm.at[0], hbuf.at[pl.ds(range_rows, hx_hbm.shape[1]), :], sem.at[1])
            extra.start()
            extra.wait()

        acc_ref[...] = jnp.zeros(acc_ref.shape, F32)
        ys[...] = jnp.zeros(ys.shape, F32)
        main.wait()

    def gather(q0):
        for rr in range(bm):
            t8 = pl.multiple_of(src_ref[0, 0, q0 + rr], n_chunks)
            xg[rr * n_chunks:(rr + 1) * n_chunks, :] = hbuf[pl.ds(t8, n_chunks), :]

    def scatter(q0):
        for b0 in range(0, bm, SCATTER_BATCH):
            rows = range(b0, b0 + SCATTER_BATCH)
            toks = [pl.multiple_of(tok_ref[0, 0, q0 + rr], n_chunks) for rr in rows]
            wts = [w_ref[0, 0, q0 + rr] for rr in rows]
            olds = [acc_ref[0, pl.ds(t8, n_chunks), :] for t8 in toks]
            news = [o + wv * ys[rr * n_chunks:(rr + 1) * n_chunks, :] for o, wv, rr in zip(olds, wts, rows)]
            for t8, nv in zip(toks, news):
                acc_ref[0, pl.ds(t8, n_chunks), :] = nv

    def block(b, carry):
        x = jnp.concatenate([xg[pl.ds(j, bm, stride=n_chunks), :] for j in range(n_chunks)], axis=1).astype(BF16)
        gather((b + 2) * bm)
        scatter(b * bm)
        hh = _silu(_dot(x, wg_b[...])) * _dot(x, wu_b[...])
        y = _dot(hh.astype(BF16), wd_b[...])
        for i in range(bm // 8):
            for j in range(n_chunks):
                ys[pl.ds(i * 8 * n_chunks + j, 8, stride=n_chunks), :] = y[8 * i:8 * i + 8, j * LANES:(j + 1) * LANES]
        return carry

    @pl.when(e == 0)
    def _():
        gather(bm)

    @pl.when(n_blocks > 0)
    def _():
        wg_b[...] = wg_ref[0].astype(BF16)
        wu_b[...] = wu_ref[0].astype(BF16)
        wd_b[...] = wd_ref[0].astype(BF16)
        block(0, 0)
        lax.fori_loop(1, n_blocks, block, 0)

    @pl.when(e == pl.num_programs(1) - 1)
    def _():
        scatter(n_blocks * bm)


def _moe_sorted(first_sub, n_sub, src8, dst8, wl, h2r, h2r_x, wg, wu, wd, *, bm):
    n_ranges, rows, _ = h2r.shape
    n_chunks = D_MODEL // LANES
    rtok = (rows + h2r_x.shape[1]) // n_chunks
    cap = _list_window(rtok, bm)
    one = pl.Element(1)
    window = lambda r, e, first, nsub: (r, 0, first[r * N_EXPERTS + e] * bm)
    list_spec = pl.BlockSpec((one, one, pl.Element(cap)), window, memory_space=pltpu.SMEM)
    grid_spec = pltpu.PrefetchScalarGridSpec(
        num_scalar_prefetch=2,
        grid=(n_ranges, N_EXPERTS),
        in_specs=[list_spec, list_spec, list_spec,
                  pl.BlockSpec(memory_space=pl.ANY), pl.BlockSpec(memory_space=pl.ANY),
                  pl.BlockSpec((1, D_MODEL, EXPERT_DIM), lambda r, e, *_: (e, 0, 0)),
                  pl.BlockSpec((1, D_MODEL, EXPERT_DIM), lambda r, e, *_: (e, 0, 0)),
                  pl.BlockSpec((1, EXPERT_DIM, D_MODEL), lambda r, e, *_: (e, 0, 0))],
        out_specs=pl.BlockSpec((1, (rtok + bm) * n_chunks, LANES), lambda r, e, *_: (r, 0, 0),
                               pipeline_mode=pl.Buffered(1)),
        scratch_shapes=[pltpu.VMEM((rtok * n_chunks, LANES), F32), pltpu.SemaphoreType.DMA((2,)),
                        pltpu.VMEM((D_MODEL, EXPERT_DIM), BF16), pltpu.VMEM((D_MODEL, EXPERT_DIM), BF16),
                        pltpu.VMEM((EXPERT_DIM, D_MODEL), BF16),
                        pltpu.VMEM((bm * n_chunks, LANES), F32), pltpu.VMEM((bm * n_chunks, LANES), F32)])
    return pl.pallas_call(
        functools.partial(_moe_sorted_kernel, bm=bm),
        grid_spec=grid_spec,
        out_shape=jax.ShapeDtypeStruct((n_ranges, (rtok + bm) * n_chunks, LANES), F32),
        compiler_params=pltpu.CompilerParams(dimension_semantics=("arbitrary", "arbitrary"),
                                             vmem_limit_bytes=VMEM_LIMIT),
        name="moe_grouped",
    )(first_sub, n_sub, src8, dst8, wl, h2r, h2r_x, wg, wu, wd)


def _final_grouped_kernel(x1_ref, fr_ref, h_ref, sg_ref, su_ref, sd_ref, g2_ref, gain_ref, y_ref):
    shared = _dot(_swiglu(h_ref[...], sg_ref[...], su_ref[...]).astype(BF16), sd_ref[...].astype(BF16))
    f = _from_token_rows(fr_ref, x1_ref.shape[0]) + shared
    y_ref[...] = x1_ref[...] + g2_ref[0] * _rmsnorm(f, gain_ref[...])


def _final_grouped(x1, fr, h2b, sg, su, sd, g2, gain, *, tm, fr_index, mod_index):
    t = x1.shape[0]
    n_chunks = D_MODEL // LANES
    row = pl.BlockSpec((tm, D_MODEL), lambda i: (i, 0))
    full = lambda shape: pl.BlockSpec(shape, lambda i: (0,) * len(shape))
    return pl.pallas_call(
        _final_grouped_kernel,
        grid=(t // tm,),
        in_specs=[row,
                  pl.BlockSpec((1, tm * n_chunks, LANES), lambda i: (*fr_index(i), 0)),
                  row, full(sg.shape), full(su.shape), full(sd.shape),
                  pl.BlockSpec((1, g2.shape[1], D_MODEL), lambda i: (mod_index(i), 0, 0)),
                  pl.BlockSpec((1, D_MODEL), lambda i: (0, 0))],
        out_specs=row,
        out_shape=jax.ShapeDtypeStruct((t, D_MODEL), F32),
        compiler_params=pltpu.CompilerParams(dimension_semantics=("parallel",), vmem_limit_bytes=VMEM_LIMIT),
        name="final_grouped",
    )(x1, fr, h2b, sg, su, sd, g2, gain)


def _rope_tables(pos):
    half = HEAD_DIM // 2
    inv_freq = ROPE_THETA ** (-jnp.arange(half, dtype=F32) / half)
    ang = pos.astype(F32)[:, None] * inv_freq[None, :]
    cos, sin = jnp.cos(ang), jnp.sin(ang)
    reps = LANES // HEAD_DIM
    return jnp.tile(jnp.concatenate([cos, cos], axis=1), (1, reps)), jnp.tile(jnp.concatenate([-sin, sin], axis=1), (1, reps))


def kernel(x_prompt, x_sample, c_prompt, c_sample, cache_k, cache_v, state_pool, page_table, ada_w, ada_b, pre_mix_g, post_mix_g, pre_ffn_g, post_ffn_g, w_in, w_out, lambda_q1, lambda_k1, lambda_q2, lambda_k2, subln_g, pool_w, pool_scale, router_w, router_bias, expert_w_gate, expert_w_up, expert_w_down, shared_w_gate, shared_w_up, shared_w_down):
    depth = ada_w.shape[0]
    assert depth == 1, "single-layer step"
    nb, seq, _ = x_prompt.shape
    db, dec_seq, _ = x_sample.shape
    assert dec_seq == 1
    page_size = cache_k.shape[2]
    n_pages = page_table.shape[1]
    past_len = n_pages * page_size
    lam_init = 0.8 - 0.6 * math.exp(-0.3 * 0)
    l = 0

    c_all = jnp.concatenate([c_prompt, c_sample], axis=0)
    pad = (-c_all.shape[0]) % 16
    mod = _ada(jnp.pad(c_all, ((0, pad), (0, 0))), ada_w[l], ada_b[l][None, :])
    mod_p = [m.reshape(nb, 1, D_MODEL) for m in jnp.split(mod[:nb], 6, axis=-1)]
    mod_s = [m.reshape(1, db, D_MODEL) for m in jnp.split(mod[nb:nb + db], 6, axis=-1)]

    w_in_b = w_in[l].astype(BF16)
    w_out_b = w_out[l].astype(BF16)
    lam4 = jnp.stack([lambda_q1[l], lambda_k1[l], lambda_q2[l], lambda_k2[l]])
    row = lambda v: v.reshape(1, -1)
    rwt = router_w[l].T
    rb_col = router_bias[l].reshape(N_EXPERTS, 1)
    experts = (expert_w_gate[l], expert_w_up[l], expert_w_down[l], shared_w_gate[l], shared_w_up[l], shared_w_down[l])

    t_p = nb * seq
    tm = TM_TOKEN
    seq_tiles = seq // tm
    cos_p, sin_p = _rope_tables(jnp.arange(seq))
    xp = x_prompt.reshape(t_p, D_MODEL)
    k_p, v_p, z_p, qb, kb, vt = _premix(xp, mod_p[0], mod_p[1], row(pre_mix_g[l]), w_in_b, cos_p, sin_p,
                                        tm=tm, rows_per_mod=seq_tiles, seq_tiles=seq_tiles, tk=TK_ATTN)
    o_p = _attn_prompt(qb.reshape(nb, seq, ATTN_W), kb.reshape(nb, seq, ATTN_W), vt, lam4,
                       subln_g[l].reshape(V_DIM, 1), tq=TQ_ATTN, tk=TK_ATTN, hps=HEADS_PER_STEP,
                       lam_init=lam_init)
    x1_p, h2_p, idx_p, w_p, h2r, cnt_p = _postmix(o_p.reshape(t_p, ATTN_W), z_p, z_p, xp, mod_p[2], mod_p[3],
                                                  mod_p[4], w_out_b, pool_w[l], row(pool_scale[l]),
                                                  row(post_mix_g[l]), row(pre_ffn_g[l]), rwt, rb_col,
                                                  tm=tm, rows_per_mod=seq_tiles, seq_tiles=seq_tiles, past_len=None)

    cos_s, sin_s = _rope_tables(jnp.full((db,), past_len))
    xs = x_sample.reshape(db, D_MODEL)
    k_s, v_s, z_s, qb_s, kb_s = _premix(xs, mod_s[0], mod_s[1], row(pre_mix_g[l]), w_in_b, cos_s, sin_s,
                                        tm=db, rows_per_mod=1, seq_tiles=1, tk=0)
    page_rows = page_size * N_HEADS
    k2 = cache_k[l].reshape(-1, V_DIM)
    v2 = cache_v[l].reshape(-1, V_DIM)
    o_s = _attn_decode(page_table, qb_s.reshape(db, N_HEADS, V_DIM), kb_s.reshape(db, N_HEADS, V_DIM),
                       v_s.reshape(db, N_HEADS, V_DIM), lam4, row(subln_g[l]), k2, v2,
                       page_rows=page_rows, lam_init=lam_init)
    hist_s = jnp.transpose(state_pool[l], (1, 0, 2))
    x1_s, h2_s, idx_s, w_s, h2r_s, cnt_s = _postmix(o_s.reshape(db, ATTN_W), z_s, hist_s, xs, mod_s[2], mod_s[3],
                                                    mod_s[4], w_out_b, pool_w[l], row(pool_scale[l]),
                                                    row(post_mix_g[l]), row(pre_ffn_g[l]), rwt, rb_col,
                                                    tm=db, rows_per_mod=1, seq_tiles=1, past_len=past_len)

    per_range = lambda c: jnp.sum(c.reshape(nb, -1, N_EXPERTS), axis=1).astype(jnp.int32)
    first_sub, n_sub, src8, dst8, wl = _dispatch_tables(idx_p, w_p, per_range(cnt_p), idx_s, w_s,
                                                        cnt_s.reshape(N_EXPERTS).astype(jnp.int32), nb, seq, MOE_BM)
    fr = _moe_sorted(first_sub, n_sub, src8, dst8, wl, h2r, h2r_s, *experts[:3], bm=MOE_BM)
    final_tiles = seq // TM_FINAL
    y_p = _final_grouped(x1_p, fr, h2_p, *experts[3:], mod_p[5], row(post_ffn_g[l]), tm=TM_FINAL,
                         fr_index=lambda i: (i // final_tiles, i % final_tiles), mod_index=lambda i: i // final_tiles)
    y_s = _final_grouped(x1_s, fr, h2_s, *experts[3:], mod_s[5], row(post_ffn_g[l]), tm=db,
                         fr_index=lambda i: (nb - 1, seq // db), mod_index=lambda i: 0)

    pool_p = z_p.reshape(nb, seq, POOL_W)[:, seq - POOL_BUF:]
    pool_s = jnp.concatenate([state_pool[l][:, 1:], z_s[:, None, :]], axis=1)
    return (y_p.reshape(nb, seq, D_MODEL), y_s.reshape(db, 1, D_MODEL),
            k_p.reshape(1, nb, seq, N_HEADS, V_DIM), v_p.reshape(1, nb, seq, N_HEADS, V_DIM), pool_p[None],
            k_s.reshape(1, db, 1, N_HEADS, V_DIM), v_s.reshape(1, db, 1, N_HEADS, V_DIM), pool_s[None])
```
